```python
import jax, jax.numpy as jnp
from jax import lax
import numpy as np

D_MODEL = 2048
BATCH = 8
SEQ = 4096
DEPTH = 4

GRID_W = 64
CTX_LEN = 256
N_MIXERS = 2
N_RET_LAYERS = (DEPTH + 1) // 2
N_MLA_LAYERS = DEPTH // 2

RET_HEADS = 8
RET_DK = D_MODEL // RET_HEADS
RET_DV = 2 * D_MODEL // RET_HEADS
RET_CHUNK = 128
RET_ROPE_BASE = 10000.0
GN_EPS = 1e-6

MLA_HEADS = 16
MLA_Q_LORA = 512
MLA_KV_LORA = 512
MLA_D_NOPE = 128
MLA_D_ROPE = 64
MLA_D_V = 128
MLA_Q_BLOCK = 128
MLA_SCALE = (MLA_D_NOPE + MLA_D_ROPE) ** -0.5
AXIAL_ROPE_BASE = 10000.0
RMS_EPS = 1e-6

FFN_HIDDEN = (((8 * D_MODEL + 2) // 3) + 255) // 256 * 256

DEEPNORM_ALPHA = (2 * DEPTH) ** 0.25
DEEPNORM_BETA = (8 * DEPTH) ** -0.25
LN_EPS = 1e-5

kernel_name = 'hybrid_retention_mla_dit_trunk'


def layer_norm(x, g, b):
    xf = x.astype(jnp.float32)
    mu = jnp.mean(xf, axis=-1, keepdims=True)
    var = jnp.mean(jnp.square(xf - mu), axis=-1, keepdims=True)
    return ((xf - mu) * lax.rsqrt(var + LN_EPS) * g + b).astype(x.dtype)


def rms_norm(x, g):
    xf = x.astype(jnp.float32)
    return (xf * lax.rsqrt(jnp.mean(jnp.square(xf), axis=-1, keepdims=True) + RMS_EPS) * g).astype(x.dtype)


def rope_tables(pos, inv_freq):
    ang = pos.astype(jnp.float32)[:, None] * inv_freq[None, :]
    return jnp.cos(ang), jnp.sin(ang)


def apply_rope(x, cos, sin):
    x1, x2 = jnp.split(x, 2, axis=-1)
    c = cos[:, None, :]
    s = sin[:, None, :]
    return jnp.concatenate([x1 * c - x2 * s, x1 * s + x2 * c], axis=-1).astype(x.dtype)


def apply_rope_2d(x, row_cos, row_sin, col_cos, col_sin):
    xr, xc = jnp.split(x, 2, axis=-1)
    return jnp.concatenate([apply_rope(xr, row_cos, row_sin), apply_rope(xc, col_cos, col_sin)], axis=-1)


def modulation(cond, w, b):
    return jnp.split(jax.nn.silu(cond) @ w + b, 6, axis=-1)


def retention_chunkwise(q, k, v, log_gamma, s0):
    B, H, N, dk = q.shape
    dv = v.shape[-1]
    C = RET_CHUNK
    nc = N // C

    def to_chunks(t):
        return t.reshape(B, H, nc, C, t.shape[-1]).transpose(2, 0, 1, 3, 4)

    idx = jnp.arange(C, dtype=jnp.float32)
    diff = idx[:, None] - idx[None, :]
    intra = jnp.where(diff >= 0, jnp.exp(log_gamma[:, None, None] * jnp.maximum(diff, 0.0)), 0.0)
    q_dec = jnp.exp(log_gamma[:, None] * (idx + 1.0))[:, :, None]
    k_dec = jnp.exp(log_gamma[:, None] * (C - 1.0 - idx))[:, :, None]
    c_dec = jnp.exp(log_gamma * C)[:, None, None]

    def step(S, qkv):
        qc, kc, vc = qkv
        scores = jnp.einsum('bhid,bhjd->bhij', qc, kc) * intra
        o = jnp.einsum('bhij,bhjv->bhiv', scores, vc) + jnp.einsum('bhid,bhdv->bhiv', qc * q_dec, S)
        S = S * c_dec + jnp.einsum('bhjd,bhjv->bhdv', kc * k_dec, vc)
        return S, o

    S, o = lax.scan(step, s0, (to_chunks(q), to_chunks(k), to_chunks(v)))
    return o.transpose(1, 2, 0, 3, 4).reshape(B, H, N, dv), S


def head_group_norm(o):
    mu = jnp.mean(o, axis=-1, keepdims=True)
    var = jnp.mean(jnp.square(o - mu), axis=-1, keepdims=True)
    o = (o - mu) * lax.rsqrt(var + GN_EPS)
    B, H, N, dv = o.shape
    return o.transpose(0, 2, 1, 3).reshape(B, N, H * dv)


def retention_mixer(u_x, u_c, ret_cos, ret_sin, w_qkv, w_g, decay_logit, w_o, ctx_out):
    H, dk, dv = RET_HEADS, RET_DK, RET_DV

    def project(u):
        B, N, _ = u.shape
        q, k, v = jnp.split(u @ w_qkv, [H * dk, 2 * H * dk], axis=-1)
        return q.reshape(B, N, H, dk), k.reshape(B, N, H, dk) * (dk ** -0.5), v.reshape(B, N, H, dv)

    def heads_first(t):
        return t.astype(jnp.float32).transpose(0, 2, 1, 3)

    q_x, k_x, v_x = project(u_x)
    q_x = apply_rope(q_x, ret_cos, ret_sin)
    k_x = apply_rope(k_x, ret_cos, ret_sin)
    q_x, k_x, v_x = heads_first(q_x), heads_first(k_x), heads_first(v_x)
    q_c, k_c, v_c = project(u_c)
    q_c, k_c, v_c = heads_first(q_c), heads_first(k_c), heads_first(v_c)

    log_gamma = jax.nn.log_sigmoid(decay_logit.astype(jnp.float32))
    s0 = jnp.zeros((q_x.shape[0], H, dk, dv), jnp.float32)
    flip = lambda t: jnp.flip(t, axis=2)
    oc_f, sc_f = retention_chunkwise(q_c, k_c, v_c, log_gamma[0], s0)
    ox_f, _ = retention_chunkwise(q_x, k_x, v_x, log_gamma[0], sc_f)
    oc_b, sc_b = retention_chunkwise(flip(q_c), flip(k_c), flip(v_c), log_gamma[1], s0)
    ox_b, _ = retention_chunkwise(flip(q_x), flip(k_x), flip(v_x), log_gamma[1], sc_b)

    def combine(u, o_f, o_b):
        g_f, g_b = jnp.split(u @ w_g, 2, axis=-1)
        y = (jax.nn.silu(g_f) * head_group_norm(o_f).astype(u.dtype)
             + jax.nn.silu(g_b) * head_group_norm(o_b).astype(u.dtype))
        return y @ w_o

    y_x = combine(u_x, ox_f, flip(ox_b))
    y_c = combine(u_c, oc_f, flip(oc_b)) if ctx_out else None
    return y_x, y_c


def mla_attend(qn, qr, kn, kr, v):
    s = jnp.einsum('bqhd,bkhd->bhqk', qn, kn) + jnp.einsum('bqhd,bkd->bhqk', qr, kr)
    p = jax.nn.softmax(s.astype(jnp.float32) * MLA_SCALE, axis=-1)
    return jnp.einsum('bhqk,bkhd->bqhd', p.astype(v.dtype), v)


def mla_block_attention(qn, qr, kn, kr, v):
    B, N, H, _ = qn.shape
    nb = N // MLA_Q_BLOCK

    def blocks(t):
        return t.reshape(B, nb, MLA_Q_BLOCK, *t.shape[2:]).swapaxes(0, 1)

    o = lax.map(lambda qs: mla_attend(qs[0], qs[1], kn, kr, v), (blocks(qn), blocks(qr)))
    return o.swapaxes(0, 1).reshape(B, N, H, MLA_D_V)


def mla_mixer(u_x, u_c, row_cos, row_sin, col_cos, col_sin,
              w_dq, g_q, w_uq, w_dkv, g_kv, w_ukv, w_o, ctx_out):
    H, dn, dr, dv = MLA_HEADS, MLA_D_NOPE, MLA_D_ROPE, MLA_D_V

    def queries(u):
        B, N, _ = u.shape
        q = (rms_norm(u @ w_dq, g_q) @ w_uq).reshape(B, N, H, dn + dr)
        return q[..., :dn], q[..., dn:]

    def keys_values(u):
        B, N, _ = u.shape
        ckv = u @ w_dkv
        c_kv = rms_norm(ckv[..., :MLA_KV_LORA], g_kv)
        kr = ckv[..., MLA_KV_LORA:]
        kv = (c_kv @ w_ukv).reshape(B, N, H, dn + dv)
        return kv[..., :dn], kr, kv[..., dn:]

    qn_x, qr_x = queries(u_x)
    qr_x = apply_rope_2d(qr_x, row_cos, row_sin, col_cos, col_sin)
    kn_x, kr_x, v_x = keys_values(u_x)
    kr_x = apply_rope_2d(kr_x[:, :, None, :], row_cos, row_sin, col_cos, col_sin)[:, :, 0, :]
    kn_c, kr_c, v_c = keys_values(u_c)

    kn = jnp.concatenate([kn_x, kn_c], axis=1)
    kr = jnp.concatenate([kr_x, kr_c], axis=1)
    v = jnp.concatenate([v_x, v_c], axis=1)
    B, N, _ = u_x.shape
    o_x = mla_block_attention(qn_x, qr_x, kn, kr, v)
    y_x = o_x.reshape(B, N, H * dv) @ w_o
    y_c = None
    if ctx_out:
        qn_c, qr_c = queries(u_c)
        o_c = mla_attend(qn_c, qr_c, kn_c, kr_c, v_c)
        y_c = o_c.reshape(B, u_c.shape[1], H * dv) @ w_o
    return y_x, y_c


def swiglu(u, w_in, w_out):
    a, b = jnp.split(u @ w_in, 2, axis=-1)
    return (jax.nn.silu(a) * b) @ w_out


def _fwd_setup_inputs(seed: int = 0) -> dict:
    key = jax.random.key(seed)
    ks = jax.random.split(key, 21)
    f32 = jnp.float32
    D, F = D_MODEL, FFN_HIDDEN

    def nrm(k, shape, scale):
        return jax.random.normal(k, shape, f32) * scale

    ret_qkv_w = 2 * RET_HEADS * RET_DK + RET_HEADS * RET_DV
    base_logit = jnp.log(2.0 ** (5.0 + jnp.arange(RET_HEADS, dtype=f32)) - 1.0)
    return {
        'x': nrm(ks[0], (BATCH, SEQ, D), 1.0),
        'c': nrm(ks[1], (BATCH, D), 1.0),
        'ctx': nrm(ks[2], (BATCH, CTX_LEN, D), 1.0),
        'c_ctx': nrm(ks[3], (D,), 1.0),
        'ada_w': nrm(ks[4], (DEPTH, D, 6 * D), 0.5 * D ** -0.5),
        'ada_b': nrm(ks[5], (DEPTH, 6 * D), 0.02),
        'ln_g': 1.0 + nrm(ks[6], (DEPTH, 2, D), 0.02),
        'ln_b': nrm(ks[7], (DEPTH, 2, D), 0.02),
        'ret_w_qkv': nrm(ks[8], (N_RET_LAYERS, D, ret_qkv_w), D ** -0.5),
        'ret_w_g': nrm(ks[9], (N_RET_LAYERS, D, 2 * RET_HEADS * RET_DV), D ** -0.5),
        'ret_decay_logit': base_logit + nrm(ks[10], (N_RET_LAYERS, 2, RET_HEADS), 0.1),
        'ret_w_o': nrm(ks[11], (N_RET_LAYERS, RET_HEADS * RET_DV, D), DEEPNORM_BETA * (RET_HEADS * RET_DV) ** -0.5),
        'mla_w_dq': nrm(ks[12], (N_MLA_LAYERS, D, MLA_Q_LORA), D ** -0.5),
        'mla_g_q': 1.0 + nrm(ks[13], (N_MLA_LAYERS, MLA_Q_LORA), 0.02),
        'mla_w_uq': nrm(ks[14], (N_MLA_LAYERS, MLA_Q_LORA, MLA_HEADS * (MLA_D_NOPE + MLA_D_ROPE)), MLA_Q_LORA ** -0.5),
        'mla_w_dkv': nrm(ks[15], (N_MLA_LAYERS, D, MLA_KV_LORA + MLA_D_ROPE), D ** -0.5),
        'mla_g_kv': 1.0 + nrm(ks[16], (N_MLA_LAYERS, MLA_KV_LORA), 0.02),
        'mla_w_ukv': nrm(ks[17], (N_MLA_LAYERS, MLA_KV_LORA, MLA_HEADS * (MLA_D_NOPE + MLA_D_V)), MLA_KV_LORA ** -0.5),
        'mla_w_o': nrm(ks[18], (N_MLA_LAYERS, MLA_HEADS * MLA_D_V, D), DEEPNORM_BETA * (MLA_HEADS * MLA_D_V) ** -0.5),
        'ffn_w_in': nrm(ks[19], (DEPTH, D, 2 * F), D ** -0.5),
        'ffn_w_out': nrm(ks[20], (DEPTH, F, D), DEEPNORM_BETA * F ** -0.5),
    }


def _fwd_reference(x, c, ctx, c_ctx, ada_w, ada_b, ln_g, ln_b,
              ret_w_qkv, ret_w_g, ret_decay_logit, ret_w_o,
              mla_w_dq, mla_g_q, mla_w_uq, mla_w_dkv, mla_g_kv, mla_w_ukv, mla_w_o,
              ffn_w_in, ffn_w_out):
    N = x.shape[1]
    ROWS = N // GRID_W
    rows = jnp.repeat(jnp.arange(ROWS), GRID_W)
    cols = jnp.tile(jnp.arange(GRID_W), ROWS)
    t = jnp.arange(N)

    axial_dim = MLA_D_ROPE // 2
    axial_inv = AXIAL_ROPE_BASE ** (-jnp.arange(axial_dim // 2, dtype=jnp.float32) * 2.0 / axial_dim)
    row_cos, row_sin = rope_tables(rows, axial_inv)
    col_cos, col_sin = rope_tables(cols, axial_inv)
    ret_inv = RET_ROPE_BASE ** (-jnp.linspace(0.0, 1.0, RET_DK // 2, dtype=jnp.float32))
    ret_cos, ret_sin = rope_tables(t, ret_inv)

    h_x, h_c = x, ctx
    for i in range(DEPTH):
        ctx_out = i < DEPTH - 1
        sh_a, sc_a, g_a, sh_f, sc_f, g_f = [m[:, None, :] for m in modulation(c, ada_w[i], ada_b[i])]
        csh_a, csc_a, cg_a, csh_f, csc_f, cg_f = modulation(c_ctx, ada_w[i], ada_b[i])

        u_x = h_x * (1.0 + sc_a) + sh_a
        u_c = h_c * (1.0 + csc_a) + csh_a
        j = i // N_MIXERS
        if i % N_MIXERS == 0:
            y_x, y_c = retention_mixer(u_x, u_c, ret_cos, ret_sin, ret_w_qkv[j], ret_w_g[j],
                                       ret_decay_logit[j], ret_w_o[j], ctx_out)
        else:
            y_x, y_c = mla_mixer(u_x, u_c, row_cos, row_sin, col_cos, col_sin,
                                 mla_w_dq[j], mla_g_q[j], mla_w_uq[j], mla_w_dkv[j], mla_g_kv[j],
                                 mla_w_ukv[j], mla_w_o[j], ctx_out)
        h_x = layer_norm(DEEPNORM_ALPHA * h_x + g_a * y_x, ln_g[i, 0], ln_b[i, 0])
        f_x = swiglu(h_x * (1.0 + sc_f) + sh_f, ffn_w_in[i], ffn_w_out[i])
        h_x = layer_norm(DEEPNORM_ALPHA * h_x + g_f * f_x, ln_g[i, 1], ln_b[i, 1])
        if ctx_out:
            h_c = layer_norm(DEEPNORM_ALPHA * h_c + cg_a * y_c, ln_g[i, 0], ln_b[i, 0])
            f_c = swiglu(h_c * (1.0 + csc_f) + csh_f, ffn_w_in[i], ffn_w_out[i])
            h_c = layer_norm(DEEPNORM_ALPHA * h_c + cg_f * f_c, ln_g[i, 1], ln_b[i, 1])
    return h_x


import jax as _jax
import jax.numpy as _jnp

TWIN_FORMAT = 'train_step'
FWD_PARAMS = ['x', 'c', 'ctx', 'c_ctx', 'ada_w', 'ada_b', 'ln_g', 'ln_b', 'ret_w_qkv', 'ret_w_g', 'ret_decay_logit', 'ret_w_o', 'mla_w_dq', 'mla_g_q', 'mla_w_uq', 'mla_w_dkv', 'mla_g_kv', 'mla_w_ukv', 'mla_w_o', 'ffn_w_in', 'ffn_w_out']
TWIN_WEIGHTS = ['c_ctx', 'ada_w', 'ada_b', 'ln_g', 'ln_b', 'ret_w_qkv', 'ret_w_g', 'ret_decay_logit', 'ret_w_o', 'mla_w_dq', 'mla_g_q', 'mla_w_uq', 'mla_w_dkv', 'mla_g_kv', 'mla_w_ukv', 'mla_w_o', 'ffn_w_in', 'ffn_w_out']
TWIN_DIFF_INPUT = 'x'
TWIN_INPUTS = ['x', 'c', 'ctx', 'c_ctx', 'ada_w', 'ada_b', 'ln_g', 'ln_b', 'ret_w_qkv', 'ret_w_g', 'ret_decay_logit', 'ret_w_o', 'mla_w_dq', 'mla_g_q', 'mla_w_uq', 'mla_w_dkv', 'mla_g_kv', 'mla_w_ukv', 'mla_w_o', 'ffn_w_in', 'ffn_w_out', 'loss_target', 'm_c_ctx', 'm_ada_w', 'm_ada_b', 'm_ln_g', 'm_ln_b', 'm_ret_w_qkv', 'm_ret_w_g', 'm_ret_decay_logit', 'm_ret_w_o', 'm_mla_w_dq', 'm_mla_g_q', 'm_mla_w_uq', 'm_mla_w_dkv', 'm_mla_g_kv', 'm_mla_w_ukv', 'm_mla_w_o', 'm_ffn_w_in', 'm_ffn_w_out', 'v_c_ctx', 'v_ada_w', 'v_ada_b', 'v_ln_g', 'v_ln_b', 'v_ret_w_qkv', 'v_ret_w_g', 'v_ret_decay_logit', 'v_ret_w_o', 'v_mla_w_dq', 'v_mla_g_q', 'v_mla_w_uq', 'v_mla_w_dkv', 'v_mla_g_kv', 'v_mla_w_ukv', 'v_mla_w_o', 'v_ffn_w_in', 'v_ffn_w_out']
TWIN_OUTPUTS = ['loss', 'grad_x', 'grad_c_ctx', 'grad_ada_w', 'grad_ada_b', 'grad_ln_g', 'grad_ln_b', 'grad_ret_w_qkv', 'grad_ret_w_g', 'grad_ret_decay_logit', 'grad_ret_w_o', 'grad_mla_w_dq', 'grad_mla_g_q', 'grad_mla_w_uq', 'grad_mla_w_dkv', 'grad_mla_g_kv', 'grad_mla_w_ukv', 'grad_mla_w_o', 'grad_ffn_w_in', 'grad_ffn_w_out', 'delta_c_ctx', 'delta_ada_w', 'delta_ada_b', 'delta_ln_g', 'delta_ln_b', 'delta_ret_w_qkv', 'delta_ret_w_g', 'delta_ret_decay_logit', 'delta_ret_w_o', 'delta_mla_w_dq', 'delta_mla_g_q', 'delta_mla_w_uq', 'delta_mla_w_dkv', 'delta_mla_g_kv', 'delta_mla_w_ukv', 'delta_mla_w_o', 'delta_ffn_w_in', 'delta_ffn_w_out', 'new_m_c_ctx', 'new_m_ada_w', 'new_m_ada_b', 'new_m_ln_g', 'new_m_ln_b', 'new_m_ret_w_qkv', 'new_m_ret_w_g', 'new_m_ret_decay_logit', 'new_m_ret_w_o', 'new_m_mla_w_dq', 'new_m_mla_g_q', 'new_m_mla_w_uq', 'new_m_mla_w_dkv', 'new_m_mla_g_kv', 'new_m_mla_w_ukv', 'new_m_mla_w_o', 'new_m_ffn_w_in', 'new_m_ffn_w_out', 'new_v_c_ctx', 'new_v_ada_w', 'new_v_ada_b', 'new_v_ln_g', 'new_v_ln_b', 'new_v_ret_w_qkv', 'new_v_ret_w_g', 'new_v_ret_decay_logit', 'new_v_ret_w_o', 'new_v_mla_w_dq', 'new_v_mla_g_q', 'new_v_mla_w_uq', 'new_v_mla_w_dkv', 'new_v_mla_g_kv', 'new_v_mla_w_ukv', 'new_v_mla_w_o', 'new_v_ffn_w_in', 'new_v_ffn_w_out']
TWIN_LEAF_KINDS = {'loss': 'loss', 'grad_x': 'grad_x', 'grad_c_ctx': 'grad_w', 'grad_ada_w': 'grad_w', 'grad_ada_b': 'grad_w', 'grad_ln_g': 'grad_w', 'grad_ln_b': 'grad_w', 'grad_ret_w_qkv': 'grad_w', 'grad_ret_w_g': 'grad_w', 'grad_ret_decay_logit': 'grad_w', 'grad_ret_w_o': 'grad_w', 'grad_mla_w_dq': 'grad_w', 'grad_mla_g_q': 'grad_w', 'grad_mla_w_uq': 'grad_w', 'grad_mla_w_dkv': 'grad_w', 'grad_mla_g_kv': 'grad_w', 'grad_mla_w_ukv': 'grad_w', 'grad_mla_w_o': 'grad_w', 'grad_ffn_w_in': 'grad_w', 'grad_ffn_w_out': 'grad_w', 'delta_c_ctx': 'delta_w', 'delta_ada_w': 'delta_w', 'delta_ada_b': 'delta_w', 'delta_ln_g': 'delta_w', 'delta_ln_b': 'delta_w', 'delta_ret_w_qkv': 'delta_w', 'delta_ret_w_g': 'delta_w', 'delta_ret_decay_logit': 'delta_w', 'delta_ret_w_o': 'delta_w', 'delta_mla_w_dq': 'delta_w', 'delta_mla_g_q': 'delta_w', 'delta_mla_w_uq': 'delta_w', 'delta_mla_w_dkv': 'delta_w', 'delta_mla_g_kv': 'delta_w', 'delta_mla_w_ukv': 'delta_w', 'delta_mla_w_o': 'delta_w', 'delta_ffn_w_in': 'delta_w', 'delta_ffn_w_out': 'delta_w', 'new_m_c_ctx': 'new_m', 'new_m_ada_w': 'new_m', 'new_m_ada_b': 'new_m', 'new_m_ln_g': 'new_m', 'new_m_ln_b': 'new_m', 'new_m_ret_w_qkv': 'new_m', 'new_m_ret_w_g': 'new_m', 'new_m_ret_decay_logit': 'new_m', 'new_m_ret_w_o': 'new_m', 'new_m_mla_w_dq': 'new_m', 'new_m_mla_g_q': 'new_m', 'new_m_mla_w_uq': 'new_m', 'new_m_mla_w_dkv': 'new_m', 'new_m_mla_g_kv': 'new_m', 'new_m_mla_w_ukv': 'new_m', 'new_m_mla_w_o': 'new_m', 'new_m_ffn_w_in': 'new_m', 'new_m_ffn_w_out': 'new_m', 'new_v_c_ctx': 'new_v', 'new_v_ada_w': 'new_v', 'new_v_ada_b': 'new_v', 'new_v_ln_g': 'new_v', 'new_v_ln_b': 'new_v', 'new_v_ret_w_qkv': 'new_v', 'new_v_ret_w_g': 'new_v', 'new_v_ret_decay_logit': 'new_v', 'new_v_ret_w_o': 'new_v', 'new_v_mla_w_dq': 'new_v', 'new_v_mla_g_q': 'new_v', 'new_v_mla_w_uq': 'new_v', 'new_v_mla_w_dkv': 'new_v', 'new_v_mla_g_kv': 'new_v', 'new_v_mla_w_ukv': 'new_v', 'new_v_mla_w_o': 'new_v', 'new_v_ffn_w_in': 'new_v', 'new_v_ffn_w_out': 'new_v'}


def _forward(args):
    return _fwd_reference(*[args[k] for k in FWD_PARAMS])


def _output_shape():
    def fwd():
        inp = _fwd_setup_inputs(0)
        return _fwd_reference(*[inp[k] for k in FWD_PARAMS])
    out = _jax.eval_shape(fwd)
    return out.shape, out.dtype

N_MICROBATCH = 1
ADAM_LR = 0.001
ADAM_B1 = 0.9
ADAM_B2 = 0.999
ADAM_EPS = 1e-08
ADAM_WD = 0.01
ADAM_STEP = 10
PER_EXAMPLE_BATCH_AXIS = {'x': 0, 'c': 0, 'ctx': 0, 'loss_target': 0}
SHARED_INPUTS = []
_WEIGHT_DTYPES = {'c_ctx': _jnp.float32, 'ada_w': _jnp.float32, 'ada_b': _jnp.float32, 'ln_g': _jnp.float32, 'ln_b': _jnp.float32, 'ret_w_qkv': _jnp.float32, 'ret_w_g': _jnp.float32, 'ret_decay_logit': _jnp.float32, 'ret_w_o': _jnp.float32, 'mla_w_dq': _jnp.float32, 'mla_g_q': _jnp.float32, 'mla_w_uq': _jnp.float32, 'mla_w_dkv': _jnp.float32, 'mla_g_kv': _jnp.float32, 'mla_w_ukv': _jnp.float32, 'mla_w_o': _jnp.float32, 'ffn_w_in': _jnp.float32, 'ffn_w_out': _jnp.float32}
MOMENT_SCALE = {'c_ctx': 6.920298e-03, 'ada_w': 7.332083e-03, 'ada_b': 1.250074e-02, 'ln_g': 5.710767e+00, 'ln_b': 3.987988e-01, 'ret_w_qkv': 5.738574e-03, 'ret_w_g': 3.191288e-03, 'ret_decay_logit': 3.808605e-02, 'ret_w_o': 1.511959e-02, 'mla_w_dq': 1.172159e-03, 'mla_g_q': 1.156809e-03, 'mla_w_uq': 4.774483e-04, 'mla_w_dkv': 4.538926e-03, 'mla_g_kv': 4.900684e-03, 'mla_w_ukv': 1.687941e-03, 'mla_w_o': 5.820294e-03, 'ffn_w_in': 2.917258e-03, 'ffn_w_out': 1.133282e-02}


def _to_microbatches(a, axis):
    t = _jnp.moveaxis(a, axis, 0)
    t = t.reshape((N_MICROBATCH, t.shape[0] // N_MICROBATCH) + t.shape[1:])
    return _jnp.moveaxis(t, 1, axis + 1)


def setup_inputs(seed: int = 0) -> dict:
    inp = _fwd_setup_inputs(seed)
    key = _jax.random.fold_in(_jax.random.key(seed), 7919)
    shape, _ = _output_shape()
    out = dict(inp)
    out["loss_target"] = _jax.random.normal(_jax.random.fold_in(key, 0), shape, _jnp.float32)
    for i, name in enumerate(TWIN_WEIGHTS):
        w = inp[name].astype(_jnp.float32)
        if MOMENT_SCALE is None:
            s = _jnp.sqrt(_jnp.mean(_jnp.square(w)) + 1e-30)
        else:
            s = MOMENT_SCALE[name]
        km, kv = _jax.random.split(_jax.random.fold_in(key, i + 1))
        out[name] = w
        out["m_" + name] = s * _jax.random.normal(km, w.shape, _jnp.float32)
        out["v_" + name] = (s * s) * _jax.random.uniform(kv, w.shape, _jnp.float32, 0.5, 1.5)
    if N_MICROBATCH > 1:
        for name, axis in PER_EXAMPLE_BATCH_AXIS.items():
            out[name] = _to_microbatches(out[name], axis)
    return {'x': out['x'], 'c': out['c'], 'ctx': out['ctx'], 'c_ctx': out['c_ctx'], 'ada_w': out['ada_w'], 'ada_b': out['ada_b'], 'ln_g': out['ln_g'], 'ln_b': out['ln_b'], 'ret_w_qkv': out['ret_w_qkv'], 'ret_w_g': out['ret_w_g'], 'ret_decay_logit': out['ret_decay_logit'], 'ret_w_o': out['ret_w_o'], 'mla_w_dq': out['mla_w_dq'], 'mla_g_q': out['mla_g_q'], 'mla_w_uq': out['mla_w_uq'], 'mla_w_dkv': out['mla_w_dkv'], 'mla_g_kv': out['mla_g_kv'], 'mla_w_ukv': out['mla_w_ukv'], 'mla_w_o': out['mla_w_o'], 'ffn_w_in': out['ffn_w_in'], 'ffn_w_out': out['ffn_w_out'], 'loss_target': out['loss_target'], 'm_c_ctx': out['m_c_ctx'], 'm_ada_w': out['m_ada_w'], 'm_ada_b': out['m_ada_b'], 'm_ln_g': out['m_ln_g'], 'm_ln_b': out['m_ln_b'], 'm_ret_w_qkv': out['m_ret_w_qkv'], 'm_ret_w_g': out['m_ret_w_g'], 'm_ret_decay_logit': out['m_ret_decay_logit'], 'm_ret_w_o': out['m_ret_w_o'], 'm_mla_w_dq': out['m_mla_w_dq'], 'm_mla_g_q': out['m_mla_g_q'], 'm_mla_w_uq': out['m_mla_w_uq'], 'm_mla_w_dkv': out['m_mla_w_dkv'], 'm_mla_g_kv': out['m_mla_g_kv'], 'm_mla_w_ukv': out['m_mla_w_ukv'], 'm_mla_w_o': out['m_mla_w_o'], 'm_ffn_w_in': out['m_ffn_w_in'], 'm_ffn_w_out': out['m_ffn_w_out'], 'v_c_ctx': out['v_c_ctx'], 'v_ada_w': out['v_ada_w'], 'v_ada_b': out['v_ada_b'], 'v_ln_g': out['v_ln_g'], 'v_ln_b': out['v_ln_b'], 'v_ret_w_qkv': out['v_ret_w_qkv'], 'v_ret_w_g': out['v_ret_w_g'], 'v_ret_decay_logit': out['v_ret_decay_logit'], 'v_ret_w_o': out['v_ret_w_o'], 'v_mla_w_dq': out['v_mla_w_dq'], 'v_mla_g_q': out['v_mla_g_q'], 'v_mla_w_uq': out['v_mla_w_uq'], 'v_mla_w_dkv': out['v_mla_w_dkv'], 'v_mla_g_kv': out['v_mla_g_kv'], 'v_mla_w_ukv': out['v_mla_w_ukv'], 'v_mla_w_o': out['v_mla_w_o'], 'v_ffn_w_in': out['v_ffn_w_in'], 'v_ffn_w_out': out['v_ffn_w_out']}


def _loss(weights, diff, rest, loss_target):
    with _jax.named_scope("forward"):
        args = {**rest, TWIN_DIFF_INPUT: diff, **{k: w.astype(_WEIGHT_DTYPES[k]) for k, w in weights.items()}}
        y = _forward(args)
    with _jax.named_scope("loss_head"):
        err = _jnp.square(y.astype(_jnp.float32) - loss_target)
        return 0.5 * _jnp.sum(_jnp.mean(err, axis=-1)) if err.ndim else 0.5 * err


def _adamw(w, g, m, v):
    m = ADAM_B1 * m + (1.0 - ADAM_B1) * g
    v = ADAM_B2 * v + (1.0 - ADAM_B2) * _jnp.square(g)
    m_hat = m / (1.0 - ADAM_B1 ** ADAM_STEP)
    v_hat = v / (1.0 - ADAM_B2 ** ADAM_STEP)
    delta = -ADAM_LR * (m_hat / (_jnp.sqrt(v_hat) + ADAM_EPS) + ADAM_WD * w)
    return delta, m, v


def reference(x, c, ctx, c_ctx, ada_w, ada_b, ln_g, ln_b, ret_w_qkv, ret_w_g, ret_decay_logit, ret_w_o, mla_w_dq, mla_g_q, mla_w_uq, mla_w_dkv, mla_g_kv, mla_w_ukv, mla_w_o, ffn_w_in, ffn_w_out, loss_target, m_c_ctx, m_ada_w, m_ada_b, m_ln_g, m_ln_b, m_ret_w_qkv, m_ret_w_g, m_ret_decay_logit, m_ret_w_o, m_mla_w_dq, m_mla_g_q, m_mla_w_uq, m_mla_w_dkv, m_mla_g_kv, m_mla_w_ukv, m_mla_w_o, m_ffn_w_in, m_ffn_w_out, v_c_ctx, v_ada_w, v_ada_b, v_ln_g, v_ln_b, v_ret_w_qkv, v_ret_w_g, v_ret_decay_logit, v_ret_w_o, v_mla_w_dq, v_mla_g_q, v_mla_w_uq, v_mla_w_dkv, v_mla_g_kv, v_mla_w_ukv, v_mla_w_o, v_ffn_w_in, v_ffn_w_out):
    given = dict(x=x, c=c, ctx=ctx, c_ctx=c_ctx, ada_w=ada_w, ada_b=ada_b, ln_g=ln_g, ln_b=ln_b, ret_w_qkv=ret_w_qkv, ret_w_g=ret_w_g, ret_decay_logit=ret_decay_logit, ret_w_o=ret_w_o, mla_w_dq=mla_w_dq, mla_g_q=mla_g_q, mla_w_uq=mla_w_uq, mla_w_dkv=mla_w_dkv, mla_g_kv=mla_g_kv, mla_w_ukv=mla_w_ukv, mla_w_o=mla_w_o, ffn_w_in=ffn_w_in, ffn_w_out=ffn_w_out, loss_target=loss_target, m_c_ctx=m_c_ctx, m_ada_w=m_ada_w, m_ada_b=m_ada_b, m_ln_g=m_ln_g, m_ln_b=m_ln_b, m_ret_w_qkv=m_ret_w_qkv, m_ret_w_g=m_ret_w_g, m_ret_decay_logit=m_ret_decay_logit, m_ret_w_o=m_ret_w_o, m_mla_w_dq=m_mla_w_dq, m_mla_g_q=m_mla_g_q, m_mla_w_uq=m_mla_w_uq, m_mla_w_dkv=m_mla_w_dkv, m_mla_g_kv=m_mla_g_kv, m_mla_w_ukv=m_mla_w_ukv, m_mla_w_o=m_mla_w_o, m_ffn_w_in=m_ffn_w_in, m_ffn_w_out=m_ffn_w_out, v_c_ctx=v_c_ctx, v_ada_w=v_ada_w, v_ada_b=v_ada_b, v_ln_g=v_ln_g, v_ln_b=v_ln_b, v_ret_w_qkv=v_ret_w_qkv, v_ret_w_g=v_ret_w_g, v_ret_decay_logit=v_ret_decay_logit, v_ret_w_o=v_ret_w_o, v_mla_w_dq=v_mla_w_dq, v_mla_g_q=v_mla_g_q, v_mla_w_uq=v_mla_w_uq, v_mla_w_dkv=v_mla_w_dkv, v_mla_g_kv=v_mla_g_kv, v_mla_w_ukv=v_mla_w_ukv, v_mla_w_o=v_mla_w_o, v_ffn_w_in=v_ffn_w_in, v_ffn_w_out=v_ffn_w_out)
    weights = {n: given[n] for n in TWIN_WEIGHTS}
    shared = {n: given[n] for n in SHARED_INPUTS}
    per_example = {n: given[n] for n in ['x', 'c', 'ctx']}
    grad_fn = _jax.value_and_grad(_loss, argnums=(0, 1))

    def one_microbatch(ex, loss_target):
        ex = dict(ex)
        diff = ex.pop(TWIN_DIFF_INPUT)
        return grad_fn(weights, diff, {**shared, **ex}, loss_target)

    if N_MICROBATCH == 1:
        loss, (grad_w, grad_x) = one_microbatch(per_example, given["loss_target"])
    else:
        def body(carry, xs):
            loss_sum, grad_sum = carry
            l_k, (gw_k, gx_k) = one_microbatch(xs[0], xs[1])
            with _jax.named_scope("update"):
                return (loss_sum + l_k, _jax.tree.map(_jnp.add, grad_sum, gw_k)), gx_k

        init = (_jnp.zeros((), _jnp.float32), _jax.tree.map(_jnp.zeros_like, weights))
        (loss, grad_w), grad_x = _jax.lax.scan(body, init, (per_example, given["loss_target"]))
    with _jax.named_scope("update"):
        delta_w, new_m, new_v = {}, {}, {}
        for n in TWIN_WEIGHTS:
            delta_w[n], new_m[n], new_v[n] = _adamw(weights[n], grad_w[n], given["m_" + n], given["v_" + n])
    return (loss, grad_x, *[grad_w[n] for n in TWIN_WEIGHTS], *[delta_w[n] for n in TWIN_WEIGHTS],
            *[new_m[n] for n in TWIN_WEIGHTS], *[new_v[n] for n in TWIN_WEIGHTS])
```

```python
import functools
import math

import jax
import jax.numpy as jnp
from jax import lax
from jax.experimental import pallas as pl
from jax.experimental.pallas import tpu as pltpu

F32, BF16 = jnp.float32, jnp.bfloat16
MESH = pl.DeviceIdType.MESH
V7X_VMEM_LIMIT_BYTES = 56 * 1024 * 1024
LANES = 128

GRID_W = 64
RET_HEADS = 8
RET_CHUNK = 128
RET_ROPE_BASE = 10000.0
GN_EPS = 1e-6
MLA_HEADS = 16
MLA_LORA = 512
MLA_D_NOPE = 128
MLA_D_ROPE = 64
MLA_D_V = 128
MLA_SCALE = (MLA_D_NOPE + MLA_D_ROPE) ** -0.5
AXIAL_ROPE_BASE = 10000.0
RMS_EPS = 1e-6
DEPTH = 4
DEEPNORM_ALPHA = (2 * DEPTH) ** 0.25
LN_EPS = 1e-5
ADAM_LR, ADAM_B1, ADAM_B2, ADAM_EPS, ADAM_WD, ADAM_STEP = 0.001, 0.9, 0.999, 1e-08, 0.01, 10
N_CHIPS = 4
N_DEV = 8


def _pick(dim, target, mult):
    best = None
    for d in range(mult, min(dim, target) + 1, mult):
        if dim % d == 0:
            best = d
    return dim if best is None else best


def _params(*sem):
    return pltpu.CompilerParams(dimension_semantics=sem, vmem_limit_bytes=V7X_VMEM_LIMIT_BYTES)


def _sigmoid(x):
    return 1.0 / (1.0 + jnp.exp(-x))


def _mm_nn(a, w, out_dtype=F32, name="mm_nn"):
    M, K = a.shape
    J, K2, n = w.shape
    assert K == K2
    tn = _pick(n, 1408, LANES)
    tk = _pick(K, 512, LANES)
    tm = _pick(M, max(16, (6 << 20) // (tn * 4)), 16)
    npj, nk = n // tn, K // tk

    def body(a_ref, w_ref, o_ref, acc):
        k = pl.program_id(2)

        @pl.when(k == 0)
        def _():
            acc[...] = jnp.zeros_like(acc)

        acc[...] += jnp.dot(a_ref[...].astype(BF16), w_ref[...].astype(BF16), preferred_element_type=F32)

        @pl.when(k == nk - 1)
        def _():
            o_ref[...] = acc[...].astype(o_ref.dtype)

    return pl.pallas_call(
        body, grid=(M // tm, J * npj, nk),
        in_specs=[pl.BlockSpec((tm, tk), lambda i, j, k: (i, k)),
                  pl.BlockSpec((None, tk, tn), lambda i, j, k: (j // npj, k, j % npj))],
        out_specs=pl.BlockSpec((tm, tn), lambda i, j, k: (i, j)),
        out_shape=jax.ShapeDtypeStruct((M, J * n), out_dtype),
        scratch_shapes=[pltpu.VMEM((tm, tn), F32)],
        compiler_params=_params("parallel", "parallel", "arbitrary"), name=name)(a, w)


def _mm_nt(dy, w, add=None, name="mm_nt"):
    M, N = dy.shape
    J, K, n = w.shape
    assert N == J * n
    tko = _pick(K, 512, LANES)
    tc = _pick(n, 512, LANES)
    tm = _pick(M, max(16, (6 << 20) // (tko * 4)), 16)
    npj = n // tc
    nc = J * npj
    has_add = add is not None

    def body(*refs):
        if has_add:
            dy_ref, w_ref, add_ref, o_ref, acc = refs
        else:
            dy_ref, w_ref, o_ref, acc = refs
        c = pl.program_id(2)

        @pl.when(c == 0)
        def _():
            acc[...] = jnp.zeros_like(acc)

        acc[...] += lax.dot_general(dy_ref[...].astype(BF16), w_ref[...].astype(BF16),
                                    (((1,), (1,)), ((), ())), preferred_element_type=F32)

        @pl.when(c == nc - 1)
        def _():
            if has_add:
                o_ref[...] = acc[...] + add_ref[...]
            else:
                o_ref[...] = acc[...]

    in_specs = [pl.BlockSpec((tm, tc), lambda i, ko, c: (i, c)),
                pl.BlockSpec((None, tko, tc), lambda i, ko, c: (c // npj, ko, c % npj))]
    args = [dy, w]
    if has_add:
        in_specs.append(pl.BlockSpec((tm, tko), lambda i, ko, c: (i, ko)))
        args.append(add)
    return pl.pallas_call(
        body, grid=(M // tm, K // tko, nc), in_specs=in_specs,
        out_specs=pl.BlockSpec((tm, tko), lambda i, ko, c: (i, ko)),
        out_shape=jax.ShapeDtypeStruct((M, K), F32),
        scratch_shapes=[pltpu.VMEM((tm, tko), F32)],
        compiler_params=_params("parallel", "parallel", "arbitrary"), name=name)(*args)


def _mm_tn(a, dy, J, name="mm_tn"):
    M, K = a.shape
    M2, N = dy.shape
    assert M == M2 and N % J == 0
    n = N // J
    tko = _pick(K, 1024, LANES)
    tn = _pick(n, 1408, LANES)
    tmc = _pick(M, 1088, 16)
    npj, nm = n // tn, M // tmc

    def body(a_ref, dy_ref, o_ref, acc):
        m = pl.program_id(2)

        @pl.when(m == 0)
        def _():
            acc[...] = jnp.zeros_like(acc)

        acc[...] += lax.dot_general(a_ref[...].astype(BF16), dy_ref[...].astype(BF16),
                                    (((0,), (0,)), ((), ())), preferred_element_type=F32)

        @pl.when(m == nm - 1)
        def _():
            o_ref[...] = acc[...]

    return pl.pallas_call(
        body, grid=(K // tko, J * npj, nm),
        in_specs=[pl.BlockSpec((tmc, tko), lambda ko, j, m: (m, ko)),
                  pl.BlockSpec((tmc, tn), lambda ko, j, m: (m, j))],
        out_specs=pl.BlockSpec((None, tko, tn), lambda ko, j, m: (j // npj, ko, j % npj)),
        out_shape=jax.ShapeDtypeStruct((J, K, n), F32),
        scratch_shapes=[pltpu.VMEM((tko, tn), F32)],
        compiler_params=_params("parallel", "parallel", "arbitrary"), name=name)(a, dy)


def _rowwise(name, fn, T, tr, n_xt, row_in, grp_in=(), const_in=(), row_out=(), gsum_w=(), tsum_w=(), ncol=1):
    nt = T // tr
    n_in = len(row_in) + len(grp_in) + len(const_in)
    n_ro, n_gs = len(row_out), len(gsum_w)
    assert ncol == 1 or not (gsum_w or tsum_w)

    def body(*refs):
        t = pl.program_id(0)
        vals = [r[...] for r in refs[:n_in]]
        rv = vals[:len(row_in)]
        gv = vals[len(row_in):len(row_in) + len(grp_in)]
        cv = vals[len(row_in) + len(grp_in):]
        ro, gs, ts = fn(rv, gv, cv)
        outs = refs[n_in:]
        for ref, val in zip(outs[:n_ro], ro):
            ref[...] = val.astype(ref.dtype)
        first_g = jnp.logical_or(t == 0, t == n_xt)
        for ref, val, first in ([(r, v, first_g) for r, v in zip(outs[n_ro:n_ro + n_gs], gs)]
                                + [(r, v, t == 0) for r, v in zip(outs[n_ro + n_gs:], ts)]):
            s = jnp.sum(val, axis=0, keepdims=True)

            @pl.when(first)
            def _(ref=ref, s=s):
                ref[...] = s

            @pl.when(jnp.logical_not(first))
            def _(ref=ref, s=s):
                ref[...] += s

    in_specs, args = [], []
    for spec in row_in:
        arr, width = spec[:2]
        lead = spec[2] if len(spec) > 2 else None
        step = spec[3] if len(spec) > 3 else 1
        if lead is not None:
            in_specs.append(pl.BlockSpec((None, tr, width), lambda t, cb, lead=lead, step=step: (lead, t, cb * step)))
        else:
            in_specs.append(pl.BlockSpec((tr, width), lambda t, cb, step=step: (t, cb * step)))
        args.append(arr)
    for arr in grp_in:
        in_specs.append(pl.BlockSpec((None,) + arr.shape[1:], lambda t, cb: (jnp.where(t >= n_xt, 1, 0), 0, 0, 0)))
        args.append(arr)
    for arr in const_in:
        in_specs.append(pl.BlockSpec(arr.shape, lambda t, cb: (0, 0, 0)))
        args.append(arr)
    out_specs, out_shape = [], []
    for wtot, wblk, dt in row_out:
        out_specs.append(pl.BlockSpec((tr, wblk), lambda t, cb: (t, cb)))
        out_shape.append(jax.ShapeDtypeStruct((T, wtot), dt))
    for w in gsum_w:
        out_specs.append(pl.BlockSpec((None, 1, w), lambda t, cb: (jnp.where(t >= n_xt, 1, 0), 0, 0)))
        out_shape.append(jax.ShapeDtypeStruct((2, 1, w), F32))
    for w in tsum_w:
        out_specs.append(pl.BlockSpec((1, w), lambda t, cb: (0, 0)))
        out_shape.append(jax.ShapeDtypeStruct((1, w), F32))
    sem = ("arbitrary", "arbitrary") if (gsum_w or tsum_w) else ("parallel", "parallel")
    return pl.pallas_call(body, grid=(nt, ncol), in_specs=in_specs, out_specs=out_specs, out_shape=out_shape,
                          compiler_params=_params(*sem), name=name)(*args)


def _modulate(h, mod, T, tr, n_xt):
    D = h.shape[1]

    def fn(rv, gv, cv):
        m = gv[0]
        return [rv[0] * (1.0 + m[1]) + m[0]], [], []

    return _rowwise("modulate", fn, T, tr, n_xt, [(h, D)], [mod], row_out=[(D, D, BF16)])[0]


def _ln_stats(r):
    mu = jnp.mean(r, axis=-1, keepdims=True)
    xc = r - mu
    var = jnp.mean(xc * xc, axis=-1, keepdims=True)
    rstd = lax.rsqrt(var + LN_EPS)
    return xc * rstd, rstd


def _ln_fwd(h, y, mod, lnp, gate_row, mod_next, next_rows, T, tr, n_xt):
    D = h.shape[1]
    with_u = mod_next is not None

    def fn(rv, gv, cv):
        r = DEEPNORM_ALPHA * rv[0] + gv[0][gate_row] * rv[1]
        xhat, _ = _ln_stats(r)
        out = xhat * cv[0][0] + cv[0][1]
        if not with_u:
            return [out], [], []
        mn = gv[1]
        return [out, out * (1.0 + mn[next_rows[1]]) + mn[next_rows[0]]], [], []

    grp = [mod, mod_next] if with_u else [mod]
    outs = [(D, D, F32), (D, D, BF16)] if with_u else [(D, D, F32)]
    res = _rowwise("ln_fwd", fn, T, tr, n_xt, [(h, D), (y, D)], grp, [lnp], row_out=outs)
    return (res[0], res[1]) if with_u else (res[0], None)


def _ln_bwd(h, y, dout, mod, lnp, gate_row, T, tr, n_xt):
    D = h.shape[1]

    def fn(rv, gv, cv):
        g = gv[0][gate_row]
        r = DEEPNORM_ALPHA * rv[0] + g * rv[1]
        xhat, rstd = _ln_stats(r)
        dxh = rv[2] * cv[0][0]
        m1 = jnp.mean(dxh, axis=-1, keepdims=True)
        m2 = jnp.mean(dxh * xhat, axis=-1, keepdims=True)
        dr = rstd * (dxh - m1 - xhat * m2)
        return [DEEPNORM_ALPHA * dr, g * dr], [dr * rv[1]], [rv[2] * xhat, rv[2]]

    return _rowwise("ln_bwd", fn, T, tr, n_xt, [(h, D), (y, D), (dout, D)], [mod], [lnp],
                    row_out=[(D, D, F32), (D, D, BF16)], gsum_w=[D], tsum_w=[D, D])


def _mod_bwd(dh_res, du, h, mod, scale_row, T, tr, n_xt):
    D = h.shape[1]

    def fn(rv, gv, cv):
        return [rv[0] + rv[1] * (1.0 + gv[0][scale_row])], [rv[1] * rv[2], rv[1]], []

    return _rowwise("mod_bwd", fn, T, tr, n_xt, [(dh_res, D), (du, D), (h, D)], [mod],
                    row_out=[(D, D, F32)], gsum_w=[D, D])


def _swiglu_fwd(a, b, T, tr, n_xt):
    F = a.shape[1]
    wc = _pick(F, 1408, LANES)

    def fn(rv, gv, cv):
        return [rv[0] * _sigmoid(rv[0]) * rv[1]], [], []

    return _rowwise("swiglu_fwd", fn, T, tr, n_xt, [(a, wc), (b, wc)], row_out=[(F, wc, BF16)], ncol=F // wc)[0]


def _swiglu_bwd(a, b, dact, T, tr, n_xt):
    F = a.shape[1]
    wc = _pick(F, 1408, LANES)

    def fn(rv, gv, cv):
        av, bv, dv = rv
        sg = _sigmoid(av)
        return [dv * bv * (sg * (1.0 + av * (1.0 - sg))), dv * av * sg], [], []

    return _rowwise("swiglu_bwd", fn, T, tr, n_xt, [(a, wc), (b, wc), (dact, wc)],
                    row_out=[(F, wc, BF16), (F, wc, BF16)], ncol=F // wc)


def _gn(o):
    mu = jnp.mean(o, axis=-1, keepdims=True)
    xc = o - mu
    var = jnp.mean(xc * xc, axis=-1, keepdims=True)
    rstd = lax.rsqrt(var + GN_EPS)
    return xc * rstd, rstd


def _gate_fwd(gf, gb, o, T, tr, n_xt, dv):
    W = gf.shape[1]

    def fn(rv, gv, cv):
        xf, _ = _gn(rv[2])
        xb, _ = _gn(rv[3])
        return [rv[0] * _sigmoid(rv[0]) * xf + rv[1] * _sigmoid(rv[1]) * xb], [], []

    return _rowwise("gate_fwd", fn, T, tr, n_xt, [(gf, dv), (gb, dv), (o, dv, 0), (o, dv, 1)],
                    row_out=[(W, dv, BF16)], ncol=W // dv)[0]


def _gate_bwd(gf, gb, o, dy, T, tr, n_xt, dv):
    W = gf.shape[1]

    def fn(rv, gv, cv):
        outs_g, outs_o = [], []
        for g, ov in ((rv[0], rv[2]), (rv[1], rv[3])):
            xh, rstd = _gn(ov)
            sg = _sigmoid(g)
            outs_g.append(rv[4] * xh * (sg * (1.0 + g * (1.0 - sg))))
            dxh = rv[4] * g * sg
            m1 = jnp.mean(dxh, axis=-1, keepdims=True)
            m2 = jnp.mean(dxh * xh, axis=-1, keepdims=True)
            outs_o.append(rstd * (dxh - m1 - xh * m2))
        return outs_g + outs_o, [], []

    return _rowwise("gate_bwd", fn, T, tr, n_xt,
                    [(gf, dv), (gb, dv), (o, dv, 0), (o, dv, 1), (dy, dv)],
                    row_out=[(W, dv, BF16)] * 4, ncol=W // dv)


def _add_dirs(dq, dk, dv_, T, tr, n_xt):
    ws = [dq.shape[2], dk.shape[2], dv_.shape[2]]

    def fn(rv, gv, cv):
        return [rv[0] + rv[1], rv[2] + rv[3], rv[4] + rv[5]], [], []

    row_in = []
    for arr, w in zip((dq, dk, dv_), ws):
        row_in += [(arr, w, 0), (arr, w, 1)]
    return _rowwise("add_dirs", fn, T, tr, n_xt, row_in, row_out=[(w, w, BF16) for w in ws])


def _swap16(x):
    lane = lax.broadcasted_iota(jnp.int32, x.shape, x.ndim - 1)
    return jnp.where(lane % 32 < 16, pltpu.roll(x, LANES - 16, x.ndim - 1), pltpu.roll(x, 16, x.ndim - 1))


def _rope2d(x, cos, sin, transpose):
    if transpose:
        return x * cos + _swap16(x * sin)
    return x * cos + _swap16(x) * sin


def _rms(x, g):
    rstd = lax.rsqrt(jnp.mean(x * x, axis=-1, keepdims=True) + RMS_EPS)
    return x * rstd, rstd


def _lora_fwd(lora, gq_gkv, cos, sin, T, tr, n_xt):
    L = MLA_LORA

    def fn(rv, gv, cv):
        x = rv[0]
        xq, _ = _rms(x[:, :L], None)
        xkv, _ = _rms(x[:, L:2 * L], None)
        return [xq * cv[0][0], xkv * cv[0][1], _rope2d(x[:, 2 * L:], rv[1], rv[2], False)], [], []

    return _rowwise("lora_fwd", fn, T, tr, n_xt, [(lora, 2 * L + LANES), (cos, LANES), (sin, LANES)],
                    const_in=[gq_gkv], row_out=[(L, L, BF16), (L, L, BF16), (LANES, LANES, BF16)])


def _lora_bwd(lora, dq, dkv, dkr, gq_gkv, cos, sin, T, tr, n_xt):
    L = MLA_LORA

    def fn(rv, gv, cv):
        x = rv[0]
        outs, sums = [], []
        for xs, dy, g in ((x[:, :L], rv[1], cv[0][0]), (x[:, L:2 * L], rv[2], cv[0][1])):
            xh, rstd = _rms(xs, None)
            dxh = dy * g
            outs.append(rstd * (dxh - xh * jnp.mean(dxh * xh, axis=-1, keepdims=True)))
            sums.append(dy * xh)
        outs.append(_rope2d(rv[3], rv[4], rv[5], True))
        return [jnp.concatenate(outs, axis=1)], [], sums

    W = 2 * L + LANES
    return _rowwise("lora_bwd", fn, T, tr, n_xt,
                    [(lora, W), (dq, L), (dkv, L), (dkr, LANES), (cos, LANES), (sin, LANES)],
                    const_in=[gq_gkv], row_out=[(W, W, BF16)], tsum_w=[L, L])


def _qrope(q, cos, sin, transpose, T, tr, n_xt):
    W = q.shape[1]

    def fn(rv, gv, cv):
        x = rv[0]
        return [jnp.concatenate([x[:, :LANES], _rope2d(x[:, LANES:], rv[1], rv[2], transpose)], axis=1)], [], []

    return _rowwise("qrope_bwd" if transpose else "qrope_fwd", fn, T, tr, n_xt,
                    [(q, 2 * LANES), (cos, LANES, None, 0), (sin, LANES, None, 0)],
                    row_out=[(W, 2 * LANES, BF16)], ncol=W // (2 * LANES))[0]


def _loss_grad(h, target, T, tr, n_xt):
    D = h.shape[1]
    nt = T // tr

    def body(h_ref, t_ref, dh_ref, s_ref):
        t = pl.program_id(0)
        diff = jnp.where(t < n_xt, h_ref[...] - t_ref[...], 0.0)
        dh_ref[...] = diff * (1.0 / D)
        s = jnp.sum(diff * diff, axis=0, keepdims=True)

        @pl.when(t == 0)
        def _():
            s_ref[...] = s

        @pl.when(t != 0)
        def _():
            s_ref[...] += s

    return pl.pallas_call(
        body, grid=(nt,),
        in_specs=[pl.BlockSpec((tr, D), lambda t: (t, 0)),
                  pl.BlockSpec((tr, D), lambda t: (jnp.minimum(t, n_xt - 1), 0))],
        out_specs=[pl.BlockSpec((tr, D), lambda t: (t, 0)), pl.BlockSpec((1, D), lambda t: (0, 0))],
        out_shape=[jax.ShapeDtypeStruct((T, D), F32), jax.ShapeDtypeStruct((1, D), F32)],
        compiler_params=_params("arbitrary"), name="loss_grad")(h, target)


def _adamw(w, g, m, v):
    shape = w.shape
    C = shape[-1] if w.ndim > 1 else shape[0]
    R = w.size // C
    tr = _pick(R, max(8, (2 << 20) // (C * 4)), 8)
    c1 = 1.0 - ADAM_B1 ** ADAM_STEP
    c2 = 1.0 - ADAM_B2 ** ADAM_STEP

    def body(w_ref, g_ref, m_ref, v_ref, d_ref, nm_ref, nv_ref):
        gv = g_ref[...]
        nm = ADAM_B1 * m_ref[...] + (1.0 - ADAM_B1) * gv
        nv = ADAM_B2 * v_ref[...] + (1.0 - ADAM_B2) * (gv * gv)
        nm_ref[...] = nm
        nv_ref[...] = nv
        d_ref[...] = -ADAM_LR * ((nm / c1) / (jnp.sqrt(nv / c2) + ADAM_EPS) + ADAM_WD * w_ref[...])

    spec = pl.BlockSpec((tr, C), lambda i: (i, 0))
    outs = pl.pallas_call(
        body, grid=(R // tr,), in_specs=[spec] * 4, out_specs=[spec] * 3,
        out_shape=[jax.ShapeDtypeStruct((R, C), F32)] * 3,
        compiler_params=_params("parallel"), name="adamw")(*[t.reshape(R, C) for t in (w, g, m, v)])
    return tuple(o.reshape(shape) for o in outs)


def _ret_chunk(d, s, ncx, ncc):
    fwd = jnp.where(s < ncc, ncx + s, s - ncc)
    bwd = jnp.where(s < ncc, ncx + ncc - 1 - s, ncx - 1 - (s - ncc))
    return jnp.where(d == 0, fwd, bwd)


def _rot_half(x, cos, sin, transpose):
    half = x.shape[-1] // 2
    if transpose:
        return x * cos + pltpu.roll(x * sin, half, x.ndim - 1)
    return x * cos + pltpu.roll(x, half, x.ndim - 1) * sin


def _dot_nt(a, b):
    return lax.dot_general(a, b, (((1,), (1,)), ((), ())), preferred_element_type=F32)


def _dot_tn(a, b):
    return lax.dot_general(a, b, (((0,), (0,)), ((), ())), preferred_element_type=F32)


def _dot(a, b):
    return jnp.dot(a, b, preferred_element_type=F32)


def _ret_fwd(q, k, v, tabs, ncx, ncc):
    T, C, H = q.shape[0], RET_CHUNK, RET_HEADS
    dk, dv = q.shape[1] // H, v.shape[1] // H
    ns = ncx + ncc
    kscale = dk ** -0.5

    def body(q_ref, k_ref, v_ref, cos_ref, sin_ref, intra_ref, qd_ref, kd_ref, cd_ref, o_ref, st_ref, s_scr):
        @pl.when(pl.program_id(2) == 0)
        def _():
            s_scr[...] = jnp.zeros_like(s_scr)

        cos, sin = cos_ref[...], sin_ref[...]
        qv = _rot_half(q_ref[...], cos, sin, False)
        kv = _rot_half(k_ref[...], cos, sin, False) * kscale
        vb = v_ref[...].astype(BF16)
        S = s_scr[...]
        st_ref[...] = S
        scores = _dot_nt(qv.astype(BF16), kv.astype(BF16)) * intra_ref[...]
        o_ref[...] = _dot(scores.astype(BF16), vb) + _dot((qv * qd_ref[...]).astype(BF16), S.astype(BF16))
        s_scr[...] = S * cd_ref[...] + _dot_tn((kv * kd_ref[...]).astype(BF16), vb)

    chunk = lambda d, h, s: _ret_chunk(d, s, ncx, ncc)
    tab = lambda shape: pl.BlockSpec((None, None) + shape, lambda d, h, s: (d, h, 0, 0))
    return pl.pallas_call(
        body, grid=(2, H, ns),
        in_specs=[pl.BlockSpec((C, dk), lambda d, h, s: (chunk(d, h, s), h)),
                  pl.BlockSpec((C, dk), lambda d, h, s: (chunk(d, h, s), h)),
                  pl.BlockSpec((C, dv), lambda d, h, s: (chunk(d, h, s), h)),
                  pl.BlockSpec((C, dk), lambda d, h, s: (chunk(d, h, s), 0)),
                  pl.BlockSpec((C, dk), lambda d, h, s: (chunk(d, h, s), 0)),
                  tab((C, C)), tab((C, dk)), tab((C, dk)), tab((1, dv))],
        out_specs=[pl.BlockSpec((None, C, dv), lambda d, h, s: (d, chunk(d, h, s), h)),
                   pl.BlockSpec((None, None, None, dk, dv), lambda d, h, s: (d, h, s, 0, 0))],
        out_shape=[jax.ShapeDtypeStruct((2, T, H * dv), F32), jax.ShapeDtypeStruct((2, H, ns, dk, dv), F32)],
        scratch_shapes=[pltpu.VMEM((dk, dv), F32)],
        compiler_params=_params("parallel", "parallel", "arbitrary"), name="ret_fwd",
    )(q, k, v, tabs["cos"], tabs["sin"], tabs["intra"], tabs["qd"], tabs["kd"], tabs["cd"])


def _ret_bwd(q, k, v, do, states, tabs, ncx, ncc):
    T, C, H = q.shape[0], RET_CHUNK, RET_HEADS
    dk, dv = q.shape[1] // H, v.shape[1] // H
    ns = ncx + ncc
    kscale = dk ** -0.5

    def body(q_ref, k_ref, v_ref, do_ref, st_ref, cos_ref, sin_ref, intra_ref, qd_ref, kd_ref, cd_ref,
             dm_ref, wq_ref, wk_ref, dq_ref, dk_ref, dv_ref, dl_ref, ds_scr):
        first = pl.program_id(2) == 0

        @pl.when(first)
        def _():
            ds_scr[...] = jnp.zeros_like(ds_scr)

        cos, sin = cos_ref[...], sin_ref[...]
        qv = _rot_half(q_ref[...], cos, sin, False)
        kv = _rot_half(k_ref[...], cos, sin, False) * kscale
        qb, kb = qv.astype(BF16), kv.astype(BF16)
        vb = v_ref[...].astype(BF16)
        dob = do_ref[...]
        intra, qd, kd, cd = intra_ref[...], qd_ref[...], kd_ref[...], cd_ref[...]
        S, dS = st_ref[...], ds_scr[...]
        Sb, dSb = S.astype(BF16), dS.astype(BF16)
        P = _dot_nt(qb, kb) * intra
        dP_raw = _dot_nt(dob, vb)
        dPb = (dP_raw * intra).astype(BF16)
        dq_cross = _dot_nt(dob, Sb) * qd
        dq_rot = _dot(dPb, kb) + dq_cross
        dk_state = _dot_nt(vb, dSb) * kd
        dk_rot = _dot_tn(dPb, qb) + dk_state
        dv_ref[...] = _dot_tn(P.astype(BF16), dob) + _dot((kv * kd).astype(BF16), dSb)
        dq_ref[...] = _rot_half(dq_rot, cos, sin, True)
        dk_ref[...] = _rot_half(dk_rot, cos, sin, True) * kscale
        dlam = (jnp.sum(dm_ref[...] * P * dP_raw) + jnp.sum(wq_ref[...] * qv * dq_cross)
                + C * jnp.sum(cd * S * dS) + jnp.sum(wk_ref[...] * kv * dk_state))
        dl = jnp.full(dl_ref.shape, dlam, F32)

        @pl.when(first)
        def _():
            dl_ref[...] = dl

        @pl.when(jnp.logical_not(first))
        def _():
            dl_ref[...] += dl

        ds_scr[...] = cd * dS + _dot_tn((qv * qd).astype(BF16), dob)

    chunk = lambda d, h, s: _ret_chunk(d, ns - 1 - s, ncx, ncc)
    tab = lambda shape: pl.BlockSpec((None, None) + shape, lambda d, h, s: (d, h, 0, 0))
    dtab = lambda shape: pl.BlockSpec((None,) + shape, lambda d, h, s: (d, 0, 0))
    return pl.pallas_call(
        body, grid=(2, H, ns),
        in_specs=[pl.BlockSpec((C, dk), lambda d, h, s: (chunk(d, h, s), h)),
                  pl.BlockSpec((C, dk), lambda d, h, s: (chunk(d, h, s), h)),
                  pl.BlockSpec((C, dv), lambda d, h, s: (chunk(d, h, s), h)),
                  pl.BlockSpec((None, C, dv), lambda d, h, s: (d, chunk(d, h, s), h)),
                  pl.BlockSpec((None, None, None, dk, dv), lambda d, h, s: (d, h, ns - 1 - s, 0, 0)),
                  pl.BlockSpec((C, dk), lambda d, h, s: (chunk(d, h, s), 0)),
                  pl.BlockSpec((C, dk), lambda d, h, s: (chunk(d, h, s), 0)),
                  tab((C, C)), tab((C, dk)), tab((C, dk)), tab((1, dv)),
                  dtab((C, C)), dtab((C, dk)), dtab((C, dk))],
        out_specs=[pl.BlockSpec((None, C, dk), lambda d, h, s: (d, chunk(d, h, s), h)),
                   pl.BlockSpec((None, C, dk), lambda d, h, s: (d, chunk(d, h, s), h)),
                   pl.BlockSpec((None, C, dv), lambda d, h, s: (d, chunk(d, h, s), h)),
                   pl.BlockSpec((None, None, 1, LANES), lambda d, h, s: (d, h, 0, 0))],
        out_shape=[jax.ShapeDtypeStruct((2, T, H * dk), F32), jax.ShapeDtypeStruct((2, T, H * dk), F32),
                   jax.ShapeDtypeStruct((2, T, H * dv), F32), jax.ShapeDtypeStruct((2, H, 1, LANES), F32)],
        scratch_shapes=[pltpu.VMEM((dk, dv), F32)],
        compiler_params=_params("parallel", "parallel", "arbitrary"), name="ret_bwd",
    )(q, k, v, do, states, tabs["cos"], tabs["sin"], tabs["intra"], tabs["qd"], tabs["kd"], tabs["cd"],
      tabs["dmat"], tabs["wq"], tabs["wk"])


def _ret_tables(decay_logit, Nx, Nc, dk, dv):
    C, H = RET_CHUNK, RET_HEADS
    inv = RET_ROPE_BASE ** (-jnp.linspace(0.0, 1.0, dk // 2, dtype=F32))
    ang = jnp.arange(Nx, dtype=F32)[:, None] * inv[None, :]
    cos, sin = jnp.cos(ang), jnp.sin(ang)
    cosf = jnp.concatenate([jnp.concatenate([cos, cos], 1), jnp.ones((Nc, dk), F32)], 0)
    sinf = jnp.concatenate([jnp.concatenate([-sin, sin], 1), jnp.zeros((Nc, dk), F32)], 0)
    lg = jax.nn.log_sigmoid(decay_logit.astype(F32))
    idx = jnp.arange(C, dtype=F32)
    diff = idx[:, None] - idx[None, :]
    dmat = jnp.stack([jnp.maximum(diff, 0.0), jnp.maximum(-diff, 0.0)])
    mask = jnp.stack([diff >= 0, diff <= 0])
    intra = jnp.where(mask[:, None], jnp.exp(lg[:, :, None, None] * dmat[:, None]), 0.0)
    wq = jnp.stack([idx + 1.0, C - idx])
    wk = jnp.stack([C - 1.0 - idx, idx])
    qd = jnp.exp(lg[:, :, None] * wq[:, None, :])
    kd = jnp.exp(lg[:, :, None] * wk[:, None, :])
    cd = jnp.exp(lg * C)
    bc = lambda t, w: jnp.broadcast_to(t[..., None], t.shape + (w,))
    return dict(cos=cosf, sin=sinf, intra=intra, qd=bc(qd, dk), kd=bc(kd, dk),
                cd=jnp.broadcast_to(cd[:, :, None, None], (2, H, 1, dv)),
                dmat=dmat, wq=bc(wq, dk), wk=bc(wk, dk), lg=lg)


def _attn_fwd(q, kn, v, kr, Nx, tq):
    T, H = q.shape[0], MLA_HEADS
    n_xq = Nx // tq

    def body(q_ref, kn_ref, v_ref, kr_ref, o_ref, lse_ref):
        def attend(lo):
            kcat = jnp.concatenate([kn_ref[lo:, :], kr_ref[lo:, :]], axis=1)
            s = _dot_nt(q_ref[...], kcat) * MLA_SCALE
            m = jnp.max(s, axis=-1, keepdims=True)
            p = jnp.exp(s - m)
            l = jnp.sum(p, axis=-1, keepdims=True)
            o_ref[...] = (_dot(p.astype(BF16), v_ref[lo:, :]) / l).astype(o_ref.dtype)
            lse_ref[...] = m + jnp.log(l)

        @pl.when(pl.program_id(1) < n_xq)
        def _():
            attend(0)

        @pl.when(pl.program_id(1) >= n_xq)
        def _():
            attend(Nx)

    return pl.pallas_call(
        body, grid=(H, T // tq),
        in_specs=[pl.BlockSpec((tq, 2 * LANES), lambda h, i: (i, h)),
                  pl.BlockSpec((T, LANES), lambda h, i: (0, h)),
                  pl.BlockSpec((T, LANES), lambda h, i: (0, h)),
                  pl.BlockSpec((T, LANES), lambda h, i: (0, 0))],
        out_specs=[pl.BlockSpec((tq, LANES), lambda h, i: (i, h)),
                   pl.BlockSpec((None, tq, 1), lambda h, i: (h, i, 0))],
        out_shape=[jax.ShapeDtypeStruct((T, H * LANES), BF16), jax.ShapeDtypeStruct((H, T, 1), F32)],
        compiler_params=_params("parallel", "arbitrary"), name="attn_fwd")(q, kn, v, kr)


def _attn_bwd(q, kn, v, kr, do, lse, Nx, tq):
    T, H = q.shape[0], MLA_HEADS
    n_xq, nq = Nx // tq, T // tq

    def body(q_ref, kn_ref, v_ref, kr_ref, do_ref, lse_ref, dq_ref, dkn_ref, dv_ref, dkr_ref, dk_acc, dv_acc):
        h, i = pl.program_id(0), pl.program_id(1)

        @pl.when(i == 0)
        def _():
            dk_acc[...] = jnp.zeros_like(dk_acc)
            dv_acc[...] = jnp.zeros_like(dv_acc)

        def attend(lo):
            kcat = jnp.concatenate([kn_ref[lo:, :], kr_ref[lo:, :]], axis=1)
            qb, dob = q_ref[...], do_ref[...]
            p = jnp.exp(_dot_nt(qb, kcat) * MLA_SCALE - lse_ref[...])
            dp = _dot_nt(dob, v_ref[lo:, :])
            delta = jnp.sum(p * dp, axis=-1, keepdims=True)
            dsb = (p * (dp - delta) * MLA_SCALE).astype(BF16)
            dq_ref[...] = _dot(dsb, kcat)
            dk_acc[lo:, :] += _dot_tn(dsb, qb)
            dv_acc[lo:, :] += _dot_tn(p.astype(BF16), dob)

        @pl.when(i < n_xq)
        def _():
            attend(0)

        @pl.when(i >= n_xq)
        def _():
            attend(Nx)

        @pl.when(i == nq - 1)
        def _():
            dkn_ref[...] = dk_acc[:, :LANES].astype(dkn_ref.dtype)
            dv_ref[...] = dv_acc[...].astype(dv_ref.dtype)

        @pl.when(jnp.logical_and(i == nq - 1, h == 0))
        def _():
            dkr_ref[...] = dk_acc[:, LANES:]

        @pl.when(jnp.logical_and(i == nq - 1, h != 0))
        def _():
            dkr_ref[...] += dk_acc[:, LANES:]

    return pl.pallas_call(
        body, grid=(H, nq),
        in_specs=[pl.BlockSpec((tq, 2 * LANES), lambda h, i: (i, h)),
                  pl.BlockSpec((T, LANES), lambda h, i: (0, h)),
                  pl.BlockSpec((T, LANES), lambda h, i: (0, h)),
                  pl.BlockSpec((T, LANES), lambda h, i: (0, 0)),
                  pl.BlockSpec((tq, LANES), lambda h, i: (i, h)),
                  pl.BlockSpec((None, tq, 1), lambda h, i: (h, i, 0))],
        out_specs=[pl.BlockSpec((tq, 2 * LANES), lambda h, i: (i, h)),
                   pl.BlockSpec((T, LANES), lambda h, i: (0, h)),
                   pl.BlockSpec((T, LANES), lambda h, i: (0, h)),
                   pl.BlockSpec((T, LANES), lambda h, i: (0, 0))],
        out_shape=[jax.ShapeDtypeStruct((T, H * 2 * LANES), F32), jax.ShapeDtypeStruct((T, H * LANES), BF16),
                   jax.ShapeDtypeStruct((T, H * LANES), BF16), jax.ShapeDtypeStruct((T, LANES), F32)],
        scratch_shapes=[pltpu.VMEM((T, 2 * LANES), F32), pltpu.VMEM((T, LANES), F32)],
        compiler_params=_params("arbitrary", "arbitrary"), name="attn_bwd")(q, kn, v, kr, do, lse)


def _mla_tables(Nx, Nc):
    ad = MLA_D_ROPE // 2
    inv = AXIAL_ROPE_BASE ** (-jnp.arange(ad // 2, dtype=F32) * 2.0 / ad)
    t = jnp.arange(Nx)
    rang = (t // GRID_W).astype(F32)[:, None] * inv[None, :]
    cang = (t % GRID_W).astype(F32)[:, None] * inv[None, :]
    rc, rs, cc, cs = jnp.cos(rang), jnp.sin(rang), jnp.cos(cang), jnp.sin(cang)
    pad1, pad0 = jnp.ones((Nx, LANES - MLA_D_ROPE), F32), jnp.zeros((Nx, LANES - MLA_D_ROPE), F32)
    cos = jnp.concatenate([rc, rc, cc, cc, pad1], 1)
    sin = jnp.concatenate([-rs, rs, -cs, cs, pad0], 1)
    return (jnp.concatenate([cos, jnp.ones((Nc, LANES), F32)], 0),
            jnp.concatenate([sin, jnp.zeros((Nc, LANES), F32)], 0))


def _place():
    x, y, c = lax.axis_index("x"), lax.axis_index("y"), lax.axis_index("c")
    return x, y, c


def _all_gather8(v):
    R = v.shape[0]

    def body(v_ref, g_ref, s_ref, send_sems, recv_sems):
        x, y, c = _place()
        me = 4 * x + 2 * y + c
        g_ref[me] = v_ref[...]
        copies = []
        for k in range(1, N_DEV):
            peer = (x ^ (k >> 2), y ^ ((k >> 1) & 1), c ^ (k & 1))
            copies.append(pltpu.make_async_remote_copy(
                src_ref=v_ref, dst_ref=g_ref.at[me], send_sem=send_sems.at[k - 1], recv_sem=recv_sems.at[k - 1],
                device_id=peer, device_id_type=MESH))
        for cp in copies:
            cp.start()
        for cp in copies:
            cp.wait_recv()
        for cp in copies:
            cp.wait_send()
        acc = g_ref[0]
        for k in range(1, N_DEV):
            acc = acc + g_ref[k]
        s_ref[...] = acc

    vm = pl.BlockSpec(memory_space=pltpu.VMEM)
    return pl.pallas_call(
        body, in_specs=[vm], out_specs=[vm, vm],
        out_shape=[jax.ShapeDtypeStruct((N_DEV, R, LANES), F32), jax.ShapeDtypeStruct((R, LANES), F32)],
        scratch_shapes=[pltpu.SemaphoreType.DMA((N_DEV - 1,)), pltpu.SemaphoreType.DMA((N_DEV - 1,))],
        compiler_params=pltpu.CompilerParams(vmem_limit_bytes=V7X_VMEM_LIMIT_BYTES), name="all_gather8")(v)


def _other_chips(x, y):
    return [(1 - x, y), (x, 1 - y), (1 - x, 1 - y)]


def _gather_chips(w):
    R, C = w.shape
    half = R // 2
    assert R % 2 == 0

    def body(w_ref, o_ref, send_sems, recv_sems, local_sem):
        x, y, c = _place()
        chip = 2 * x + y
        mine, sibling = pl.ds(c * half, half), (x, y, 1 - c)
        chips = _other_chips(x, y)

        def copy(k, src, dst, to):
            return pltpu.make_async_remote_copy(src_ref=src, dst_ref=dst, send_sem=send_sems.at[k],
                                                recv_sem=recv_sems.at[k], device_id=to, device_id_type=MESH)

        local = pltpu.make_async_copy(w_ref, o_ref.at[chip], local_sem)
        local.start()
        first = [copy(j, w_ref.at[mine], o_ref.at[chip, mine], (*ch, c)) for j, ch in enumerate(chips)]
        for cp in first:
            cp.start()
        landed = [o_ref.at[2 * ch[0] + ch[1], mine] for ch in chips]
        passed = [copy(3 + j, landed[j], landed[j], sibling) for j in range(3)]
        for j in range(3):
            copy(j, landed[j], landed[j], sibling).wait_recv()
            passed[j].start()
        for j, ch in enumerate(chips):
            theirs = o_ref.at[2 * ch[0] + ch[1], pl.ds((1 - c) * half, half)]
            copy(3 + j, theirs, theirs, sibling).wait_recv()
        for cp in first + passed:
            cp.wait_send()
        local.wait()

    hbm = pl.BlockSpec(memory_space=pl.ANY)
    return pl.pallas_call(
        body, in_specs=[hbm], out_specs=hbm, out_shape=jax.ShapeDtypeStruct((N_CHIPS, R, C), w.dtype),
        scratch_shapes=[pltpu.SemaphoreType.DMA((6,)), pltpu.SemaphoreType.DMA((6,)), pltpu.SemaphoreType.DMA],
        name="gather_chips")(w)


def _swap_halves(g):
    J, R, C = g.shape
    half = R // 2

    def body(g_ref, o_ref, send_sem, recv_sem):
        x, y, c = _place()
        cp = pltpu.make_async_remote_copy(src_ref=g_ref.at[:, pl.ds((1 - c) * half, half), :], dst_ref=o_ref,
                                          send_sem=send_sem, recv_sem=recv_sem, device_id=(x, y, 1 - c),
                                          device_id_type=MESH)
        cp.start()
        cp.wait()

    hbm = pl.BlockSpec(memory_space=pl.ANY)
    return pl.pallas_call(
        body, in_specs=[hbm], out_specs=hbm, out_shape=jax.ShapeDtypeStruct((J, half, C), g.dtype),
        scratch_shapes=[pltpu.SemaphoreType.DMA, pltpu.SemaphoreType.DMA], name="swap_halves")(g)


def _add_half(g, r, c_idx):
    J, R, C = g.shape
    half = R // 2
    tr = _pick(half, max(8, (2 << 20) // (C * 4)), 8)
    nb = half // tr

    def body(c_ref, g_ref, r_ref, o_ref):
        o_ref[...] = g_ref[...] + r_ref[...]

    return pl.pallas_call(
        body,
        grid_spec=pltpu.PrefetchScalarGridSpec(
            num_scalar_prefetch=1, grid=(J, nb),
            in_specs=[pl.BlockSpec((None, tr, C), lambda j, i, c_ref: (j, c_ref[0] * nb + i, 0)),
                      pl.BlockSpec((None, tr, C), lambda j, i, c_ref: (j, i, 0))],
            out_specs=pl.BlockSpec((None, tr, C), lambda j, i, c_ref: (j, i, 0))),
        out_shape=jax.ShapeDtypeStruct((J, half, C), F32),
        compiler_params=_params("parallel", "parallel"), name="add_half")(c_idx, g, r)


def _scatter_chips(h):
    J, R, C = h.shape

    def body(h_ref, o_ref, send_sems, recv_sems, local_sem):
        x, y, c = _place()
        chip = 2 * x + y
        chips = _other_chips(x, y)
        local = pltpu.make_async_copy(h_ref.at[chip], o_ref.at[chip], local_sem)
        local.start()
        copies = [pltpu.make_async_remote_copy(
            src_ref=h_ref.at[2 * ch[0] + ch[1]], dst_ref=o_ref.at[chip], send_sem=send_sems.at[j],
            recv_sem=recv_sems.at[j], device_id=(*ch, c), device_id_type=MESH) for j, ch in enumerate(chips)]
        for cp in copies:
            cp.start()
        for cp in copies:
            cp.wait_recv()
        for cp in copies:
            cp.wait_send()
        local.wait()

    hbm = pl.BlockSpec(memory_space=pl.ANY)
    return pl.pallas_call(
        body, in_specs=[hbm], out_specs=hbm, out_shape=jax.ShapeDtypeStruct((J, R, C), h.dtype),
        scratch_shapes=[pltpu.SemaphoreType.DMA((3,)), pltpu.SemaphoreType.DMA((3,)), pltpu.SemaphoreType.DMA],
        name="scatter_chips")(h)


def _add4(p):
    J, R, C = p.shape
    tr = _pick(R, max(8, (2 << 20) // (C * 4)), 8)

    def body(p_ref, o_ref):
        o_ref[...] = ((p_ref[0] + p_ref[1]) + p_ref[2]) + p_ref[3]

    return pl.pallas_call(
        body, grid=(R // tr,), in_specs=[pl.BlockSpec((J, tr, C), lambda i: (0, i, 0))],
        out_specs=pl.BlockSpec((tr, C), lambda i: (i, 0)), out_shape=jax.ShapeDtypeStruct((R, C), F32),
        compiler_params=_params("parallel"), name="add4")(p)


def _join_halves(s):
    half, C = s.shape

    def body(s_ref, o_ref, send_sem, recv_sem, local_sem):
        x, y, c = _place()
        mine = pl.ds(c * half, half)
        local = pltpu.make_async_copy(s_ref, o_ref.at[mine], local_sem)
        local.start()
        cp = pltpu.make_async_remote_copy(src_ref=s_ref, dst_ref=o_ref.at[mine], send_sem=send_sem,
                                          recv_sem=recv_sem, device_id=(x, y, 1 - c), device_id_type=MESH)
        cp.start()
        cp.wait()
        local.wait()

    hbm = pl.BlockSpec(memory_space=pl.ANY)
    return pl.pallas_call(
        body, in_specs=[hbm], out_specs=hbm, out_shape=jax.ShapeDtypeStruct((2 * half, C), s.dtype),
        scratch_shapes=[pltpu.SemaphoreType.DMA, pltpu.SemaphoreType.DMA, pltpu.SemaphoreType.DMA],
        name="join_halves")(s)


def _reduce_scatter(g, c_idx):
    h = _add_half(g, _swap_halves(g), c_idx)
    return _join_halves(_add4(_scatter_chips(h)))


def _pack(parts):
    flat = jnp.concatenate([p.reshape(-1).astype(F32) for p in parts])
    n = flat.shape[0]
    rows = -(-n // (8 * LANES)) * 8
    return jnp.pad(flat, (0, rows * LANES - n)).reshape(rows, LANES)


def _unpack(buf, shapes):
    flat = buf.reshape(buf.shape[:-2] + (-1,))
    out, off = [], 0
    for s in shapes:
        n = math.prod(s)
        out.append(flat[..., off:off + n].reshape(buf.shape[:-2] + tuple(s)))
        off += n
    return out


def _mod_table(mod_x, mod_c):
    return jnp.stack([mod_x.reshape(6, 1, -1), mod_c.reshape(6, 1, -1)])


def kernel(x, c, ctx, c_ctx, ada_w, ada_b, ln_g, ln_b, ret_w_qkv, ret_w_g, ret_decay_logit, ret_w_o, mla_w_dq, mla_g_q, mla_w_uq, mla_w_dkv, mla_g_kv, mla_w_ukv, mla_w_o, ffn_w_in, ffn_w_out, loss_target, m_c_ctx, m_ada_w, m_ada_b, m_ln_g, m_ln_b, m_ret_w_qkv, m_ret_w_g, m_ret_decay_logit, m_ret_w_o, m_mla_w_dq, m_mla_g_q, m_mla_w_uq, m_mla_w_dkv, m_mla_g_kv, m_mla_w_ukv, m_mla_w_o, m_ffn_w_in, m_ffn_w_out, v_c_ctx, v_ada_w, v_ada_b, v_ln_g, v_ln_b, v_ret_w_qkv, v_ret_w_g, v_ret_decay_logit, v_ret_w_o, v_mla_w_dq, v_mla_g_q, v_mla_w_uq, v_mla_w_dkv, v_mla_g_kv, v_mla_w_ukv, v_mla_w_o, v_ffn_w_in, v_ffn_w_out):
    Nx, D = x.shape[1], x.shape[2]
    Nc = ctx.shape[1]
    T = Nx + Nc
    tr = 256 if (Nx % 256 == 0 and Nc % 256 == 0) else 128
    n_xt = Nx // tr
    C = RET_CHUNK
    ncx, ncc = Nx // C, Nc // C
    H = RET_HEADS
    dk, dv = D // H, 2 * D // H
    L = MLA_LORA
    HM = MLA_HEADS
    D6 = 6 * D
    n6 = D6 // N_CHIPS
    Dq = D // N_CHIPS
    xi, yi, ci = lax.axis_index("x"), lax.axis_index("y"), lax.axis_index("c")
    chip = 2 * xi + yi
    dev = 4 * xi + 2 * yi + ci
    c_idx = jnp.reshape(ci, (1,)).astype(jnp.int32)
    tile = dict(T=T, tr=tr, n_xt=n_xt)

    shapes1 = [(D,), (DEPTH, 2, Dq), (DEPTH, 2, Dq), (2, L // N_CHIPS), (2, L // N_CHIPS)]
    g1, _ = _all_gather8(_pack([c[0], ln_g, ln_b, mla_g_q, mla_g_kv]))
    c_all, lng_s, lnb_s, gq_s, gkv_s = _unpack(g1, shapes1)
    by_chip = lambda t: jnp.moveaxis(t[0::2], 0, -2).reshape(t.shape[1:-1] + (-1,))
    ln_g_full, ln_b_full = by_chip(lng_s), by_chip(lnb_s)
    gq_full, gkv_full = by_chip(gq_s), by_chip(gkv_s)

    cond = jnp.concatenate([c_all, c_ctx[None]], 0)
    silu_cond = cond * jax.nn.sigmoid(cond)
    s16 = jnp.pad(silu_cond, ((0, 16 - (N_DEV + 1)), (0, 0))).astype(BF16)
    mods = []
    for i in range(DEPTH):
        bias = lax.dynamic_slice_in_dim(ada_b[i], chip * n6, n6)
        mods.append(_mm_nn(s16, ada_w[i][None], name="ada_fwd")[:N_DEV + 1] + bias[None])
    g2, _ = _all_gather8(_pack([jnp.stack(mods)]))
    (mod_all,) = _unpack(g2, [(DEPTH, N_DEV + 1, n6)])
    mod_all = jnp.moveaxis(mod_all[0::2], 0, -2).reshape(DEPTH, N_DEV + 1, D6)
    mod_tabs = [_mod_table(lax.dynamic_index_in_dim(mod_all[i], dev, 0, False), mod_all[i, N_DEV])
                for i in range(DEPTH)]
    lnps = [[jnp.stack([ln_g_full[i, s], ln_b_full[i, s]])[:, None, :] for s in range(2)] for i in range(DEPTH)]

    def gathered(w_l):
        return _gather_chips(w_l.astype(BF16))

    def mla_pack(j):
        return jnp.concatenate([mla_w_dq[j], mla_w_uq[j], jnp.pad(mla_w_dkv[j], ((0, 0), (0, 64))),
                                mla_w_ukv[j], mla_w_o[j]], axis=1)

    mla_cols = [L, 3 * L // 2, L + LANES, 2 * L, D]

    def mla_unpack_weights(buf):
        offs = [0]
        for wdt in mla_cols:
            offs.append(offs[-1] + wdt)
        dq_, uq_, dkv_, ukv_, wo_ = [buf[:, :, offs[k]:offs[k + 1]] for k in range(5)]
        w_dq = dq_.reshape(D, L)
        w_dkv = dkv_.reshape(D, L + LANES)
        w_lora = jnp.concatenate([w_dq, w_dkv], axis=1)
        w_uq = jnp.moveaxis(uq_, 0, 1).reshape(L, HM, MLA_D_NOPE + MLA_D_ROPE)
        wq_cat = jnp.pad(w_uq, ((0, 0), (0, 0), (0, 2 * LANES - MLA_D_NOPE - MLA_D_ROPE))).reshape(L, HM * 2 * LANES)
        w_ukv = jnp.moveaxis(ukv_, 0, 1).reshape(L, HM, MLA_D_NOPE + MLA_D_V)
        w_kn = w_ukv[:, :, :MLA_D_NOPE].reshape(L, HM * MLA_D_NOPE)
        w_v = w_ukv[:, :, MLA_D_NOPE:].reshape(L, HM * MLA_D_V)
        w_o = wo_.reshape(HM * MLA_D_V, D)
        return dict(lora=w_lora[None], q=wq_cat[None], kn=w_kn[None], v=w_v[None], o=w_o[None])

    def mla_pack_grads(g):
        d_lora = g["lora"][0]
        d_dq = d_lora[:, :L].reshape(N_CHIPS, Dq, L)
        d_dkv = d_lora[:, L:].reshape(N_CHIPS, Dq, L + LANES)
        d_uq = g["q"][0].reshape(L, HM, 2 * LANES)[:, :, :MLA_D_NOPE + MLA_D_ROPE]
        d_uq = jnp.moveaxis(d_uq.reshape(L, N_CHIPS, -1), 1, 0)
        d_ukv = jnp.concatenate([g["kn"][0].reshape(L, HM, MLA_D_NOPE), g["v"][0].reshape(L, HM, MLA_D_V)], axis=2)
        d_ukv = jnp.moveaxis(d_ukv.reshape(L, N_CHIPS, -1), 1, 0)
        d_o = g["o"][0].reshape(N_CHIPS, L, D)
        return jnp.concatenate([d_dq, d_uq, d_dkv, d_ukv, d_o], axis=2)

    assert Dq == L, "the packed MLA buffer assumes D_MODEL / 4 == 512 rows per shard"

    W = []
    for i in range(DEPTH):
        j = i // 2
        lw = {}
        if i % 2 == 0:
            qkv = gathered(ret_w_qkv[j])
            gg = gathered(ret_w_g[j])
            lw.update(q=qkv[0:1], k=qkv[1:2], v=qkv[2:4], gf=gg[0:2], gb=gg[2:4],
                      o=gathered(ret_w_o[j]).reshape(1, 2 * D, D))
        else:
            lw.update(mla_unpack_weights(gathered(mla_pack(j))))
        w_in = gathered(ffn_w_in[i])
        lw.update(a=w_in[0:2], b=w_in[2:4], out=gathered(ffn_w_out[i]).reshape(1, -1, D))
        W.append(lw)

    ret_tabs = [_ret_tables(ret_decay_logit[j], Nx, Nc, dk, dv) for j in range(2)]
    mla_cos, mla_sin = _mla_tables(Nx, Nc)
    gqkv = [jnp.stack([gq_full[j], gkv_full[j]])[:, None, :] for j in range(2)]
    tq_f, tq_b = tr, 128

    h = jnp.concatenate([x[0], ctx[0]], axis=0)
    u = _modulate(h, mod_tabs[0], **tile)
    saved = []
    for i in range(DEPTH):
        j, lw, mod, sv = i // 2, W[i], mod_tabs[i], {}
        sv.update(h=h, u=u)
        if i % 2 == 0:
            q = _mm_nn(u, lw["q"], name="ret_q")
            k = _mm_nn(u, lw["k"], name="ret_k")
            v = _mm_nn(u, lw["v"], name="ret_v")
            gf = _mm_nn(u, lw["gf"], name="ret_gf")
            gb = _mm_nn(u, lw["gb"], name="ret_gb")
            o, states = _ret_fwd(q, k, v, ret_tabs[j], ncx, ncc)
            yg = _gate_fwd(gf, gb, o, dv=dv, **tile)
            y = _mm_nn(yg, lw["o"], name="ret_o")
            sv.update(q=q, k=k, v=v, gf=gf, gb=gb, o=o, states=states, yg=yg)
        else:
            lora = _mm_nn(u, lw["lora"], name="mla_lora")
            cqn, ckvn, kr = _lora_fwd(lora, gqkv[j], mla_cos, mla_sin, **tile)
            qcat = _mm_nn(cqn, lw["q"], name="mla_q")
            qrot = _qrope(qcat, mla_cos, mla_sin, False, **tile)
            kn = _mm_nn(ckvn, lw["kn"], out_dtype=BF16, name="mla_kn")
            vv = _mm_nn(ckvn, lw["v"], out_dtype=BF16, name="mla_v")
            att, lse = _attn_fwd(qrot, kn, vv, kr, Nx, tq_f)
            y = _mm_nn(att, lw["o"], name="mla_o")
            sv.update(lora=lora, cqn=cqn, ckvn=ckvn, kr=kr, qrot=qrot, kn=kn, vv=vv, att=att, lse=lse)
        h1, u2 = _ln_fwd(h, y, mod, lnps[i][0], 2, mod, (3, 4), **tile)
        a = _mm_nn(u2, lw["a"], name="ffn_a")
        b = _mm_nn(u2, lw["b"], name="ffn_b")
        act = _swiglu_fwd(a, b, **tile)
        f = _mm_nn(act, lw["out"], name="ffn_out")
        last = i == DEPTH - 1
        h2, u_next = _ln_fwd(h1, f, mod, lnps[i][1], 5, None if last else mod_tabs[i + 1], (0, 1), **tile)
        sv.update(y=y, h1=h1, u2=u2, a=a, b=b, act=act, f=f)
        saved.append(sv)
        h, u = h2, u_next

    dh, err_cols = _loss_grad(h, loss_target[0], **tile)
    loss = lax.psum(0.5 * jnp.sum(err_cols) / D, ("x", "y", "c"))

    g_big = {}
    d_mods, d_lng, d_lnb = [None] * DEPTH, [None] * DEPTH, [None] * DEPTH
    d_gq, d_gkv, d_lam = [None] * 2, [None] * 2, [None] * 2
    for i in reversed(range(DEPTH)):
        j, lw, mod, sv = i // 2, W[i], mod_tabs[i], saved[i]
        dh1_res, df, dg_f, dlg1, dlb1 = _ln_bwd(sv["h1"], sv["f"], dh, mod, lnps[i][1], 5, **tile)
        dact = _mm_nt(df, lw["out"], name="ffn_out_nt")
        g_out = _mm_tn(sv["act"], df, 1, name="ffn_out_tn").reshape(N_CHIPS, -1, D)
        da, db = _swiglu_bwd(sv["a"], sv["b"], dact, **tile)
        du2 = _mm_nt(db, lw["b"], add=_mm_nt(da, lw["a"], name="ffn_a_nt"), name="ffn_b_nt")
        g_in = jnp.concatenate([_mm_tn(sv["u2"], da, 2, name="ffn_a_tn"), _mm_tn(sv["u2"], db, 2, name="ffn_b_tn")], 0)
        dh1, dsc_f, dsh_f = _mod_bwd(dh1_res, du2, sv["h1"], mod, 4, **tile)
        dh_res, dy, dg_a, dlg0, dlb0 = _ln_bwd(sv["h"], sv["y"], dh1, mod, lnps[i][0], 2, **tile)
        uu = sv["u"]
        if i % 2 == 0:
            dyg = _mm_nt(dy, lw["o"], name="ret_o_nt")
            g_o = _mm_tn(sv["yg"], dy, 1, name="ret_o_tn").reshape(N_CHIPS, -1, D)
            dgf, dgb, do_f, do_b = _gate_bwd(sv["gf"], sv["gb"], sv["o"], dyg, dv=dv, **tile)
            dq2, dk2, dv2, dlam = _ret_bwd(sv["q"], sv["k"], sv["v"], jnp.stack([do_f, do_b]), sv["states"],
                                           ret_tabs[j], ncx, ncc)
            dq, dkk, dvv = _add_dirs(dq2, dk2, dv2, **tile)
            du = _mm_nt(dq, lw["q"], name="ret_q_nt")
            du = _mm_nt(dkk, lw["k"], add=du, name="ret_k_nt")
            du = _mm_nt(dvv, lw["v"], add=du, name="ret_v_nt")
            du = _mm_nt(dgf, lw["gf"], add=du, name="ret_gf_nt")
            du = _mm_nt(dgb, lw["gb"], add=du, name="ret_gb_nt")
            g_qkv = jnp.concatenate([_mm_tn(uu, dq, 1, name="ret_q_tn"), _mm_tn(uu, dkk, 1, name="ret_k_tn"),
                                     _mm_tn(uu, dvv, 2, name="ret_v_tn")], 0)
            g_g = jnp.concatenate([_mm_tn(uu, dgf, 2, name="ret_gf_tn"), _mm_tn(uu, dgb, 2, name="ret_gb_tn")], 0)
            g_big.setdefault("ret_w_qkv", {})[j] = g_qkv
            g_big.setdefault("ret_w_g", {})[j] = g_g
            g_big.setdefault("ret_w_o", {})[j] = g_o
            d_lam[j] = dlam[:, :, 0, 0]
        else:
            datt = _mm_nt(dy, lw["o"], name="mla_o_nt").astype(BF16)
            gm = dict(o=_mm_tn(sv["att"], dy, 1, name="mla_o_tn"))
            dqcat, dkn, dvv, dkr = _attn_bwd(sv["qrot"], sv["kn"], sv["vv"], sv["kr"], datt, sv["lse"], Nx, tq_b)
            dqraw = _qrope(dqcat, mla_cos, mla_sin, True, **tile)
            dcqn = _mm_nt(dqraw, lw["q"], name="mla_q_nt")
            gm["q"] = _mm_tn(sv["cqn"], dqraw, 1, name="mla_q_tn")
            dckvn = _mm_nt(dvv, lw["v"], add=_mm_nt(dkn, lw["kn"], name="mla_kn_nt"), name="mla_v_nt")
            gm["kn"] = _mm_tn(sv["ckvn"], dkn, 1, name="mla_kn_tn")
            gm["v"] = _mm_tn(sv["ckvn"], dvv, 1, name="mla_v_tn")
            dlora, dgq, dgkv = _lora_bwd(sv["lora"], dcqn, dckvn, dkr, gqkv[j], mla_cos, mla_sin, **tile)
            du = _mm_nt(dlora, lw["lora"], name="mla_lora_nt")
            gm["lora"] = _mm_tn(uu, dlora, 1, name="mla_lora_tn")
            g_big.setdefault("mla", {})[j] = mla_pack_grads(gm)
            d_gq[j], d_gkv[j] = dgq[0], dgkv[0]
        dh, dsc_a, dsh_a = _mod_bwd(dh_res, du, sv["h"], mod, 1, **tile)
        g_big.setdefault("ffn_w_in", {})[i] = g_in
        g_big.setdefault("ffn_w_out", {})[i] = g_out
        d_mods[i] = jnp.concatenate([dsh_a, dsc_a, dg_a, dsh_f, dsc_f, dg_f], axis=2)[:, 0, :]
        d_lng[i] = jnp.concatenate([dlg0, dlg1], 0)
        d_lnb[i] = jnp.concatenate([dlb0, dlb1], 0)

    grad_x = dh[:Nx][None]

    d_mods = jnp.stack(d_mods)
    dlogit = jnp.stack([d_lam[j] * jax.nn.sigmoid(-ret_decay_logit[j]) for j in range(2)])
    shapes3 = [(DEPTH, D6), (DEPTH, D6), (DEPTH, 2, D), (DEPTH, 2, D), (2, L), (2, L), (2, 2, H)]
    g3, s3 = _all_gather8(_pack([d_mods[:, 0], d_mods[:, 1], jnp.stack(d_lng), jnp.stack(d_lnb),
                                 jnp.stack(d_gq), jnp.stack(d_gkv), dlogit]))
    dmod_x_all = _unpack(g3, shapes3)[0]
    dmod_x_sum, dmod_c_sum, g_lng, g_lnb, g_gq, g_gkv, g_decay = _unpack(s3, shapes3)
    grad_ada_b = dmod_x_sum + dmod_c_sum
    dmod9 = jnp.concatenate([jnp.moveaxis(dmod_x_all, 0, 1), dmod_c_sum[:, None]], axis=1)
    dmod16 = jnp.pad(lax.dynamic_slice_in_dim(dmod9, chip * n6, n6, axis=2), ((0, 0), (0, 16 - (N_DEV + 1)), (0, 0)))
    dmod16 = dmod16.astype(BF16)
    grad_ada_w = jnp.stack([_mm_tn(s16, dmod16[i], 1, name="ada_tn")[0] for i in range(DEPTH)])
    dsilu = _mm_nt(jnp.moveaxis(dmod16, 0, 1).reshape(16, DEPTH * n6), ada_w, name="ada_nt")
    _, s4 = _all_gather8(_pack([dsilu[N_DEV]]))
    sg = jax.nn.sigmoid(c_ctx)
    grad_c_ctx = (0.5 * s4.reshape(-1)[:D]) * (sg * (1.0 + c_ctx * (1.0 - sg)))

    my_cols = lambda t, n: lax.dynamic_slice_in_dim(t, chip * n, n, axis=t.ndim - 1)
    grad_ln_g, grad_ln_b = my_cols(g_lng, Dq), my_cols(g_lnb, Dq)
    grad_gq, grad_gkv = my_cols(g_gq, L // N_CHIPS), my_cols(g_gkv, L // N_CHIPS)

    def rs(name, n_layers, shard_shape):
        return jnp.stack([_reduce_scatter(g_big[name][l].reshape(N_CHIPS, -1, shard_shape[-1]), c_idx)
                          .reshape(shard_shape) for l in range(n_layers)])

    grad_ret_w_qkv = rs("ret_w_qkv", 2, ret_w_qkv.shape[1:])
    grad_ret_w_g = rs("ret_w_g", 2, ret_w_g.shape[1:])
    grad_ret_w_o = rs("ret_w_o", 2, ret_w_o.shape[1:])
    grad_ffn_w_in = rs("ffn_w_in", DEPTH, ffn_w_in.shape[1:])
    grad_ffn_w_out = rs("ffn_w_out", DEPTH, ffn_w_out.shape[1:])
    mla_red = [_reduce_scatter(g_big["mla"][l], c_idx) for l in range(2)]
    offs = [0]
    for wdt in mla_cols:
        offs.append(offs[-1] + wdt)
    mla_parts = [jnp.stack([mla_red[l][:, offs[k]:offs[k + 1]] for l in range(2)]) for k in range(5)]
    grad_mla_w_dq, grad_mla_w_uq, grad_mla_w_ukv, grad_mla_w_o = mla_parts[0], mla_parts[1], mla_parts[3], mla_parts[4]
    grad_mla_w_dkv = mla_parts[2][:, :, :L + MLA_D_ROPE]

    grads = [grad_c_ctx, grad_ada_w, grad_ada_b, grad_ln_g, grad_ln_b, grad_ret_w_qkv, grad_ret_w_g, g_decay,
             grad_ret_w_o, grad_mla_w_dq, grad_gq, grad_mla_w_uq, grad_mla_w_dkv, grad_gkv, grad_mla_w_ukv,
             grad_mla_w_o, grad_ffn_w_in, grad_ffn_w_out]
    weights = [c_ctx, ada_w, ada_b, ln_g, ln_b, ret_w_qkv, ret_w_g, ret_decay_logit, ret_w_o, mla_w_dq, mla_g_q,
               mla_w_uq, mla_w_dkv, mla_g_kv, mla_w_ukv, mla_w_o, ffn_w_in, ffn_w_out]
    ms = [m_c_ctx, m_ada_w, m_ada_b, m_ln_g, m_ln_b, m_ret_w_qkv, m_ret_w_g, m_ret_decay_logit, m_ret_w_o,
          m_mla_w_dq, m_mla_g_q, m_mla_w_uq, m_mla_w_dkv, m_mla_g_kv, m_mla_w_ukv, m_mla_w_o, m_ffn_w_in, m_ffn_w_out]
    vs = [v_c_ctx, v_ada_w, v_ada_b, v_ln_g, v_ln_b, v_ret_w_qkv, v_ret_w_g, v_ret_decay_logit, v_ret_w_o,
          v_mla_w_dq, v_mla_g_q, v_mla_w_uq, v_mla_w_dkv, v_mla_g_kv, v_mla_w_ukv, v_mla_w_o, v_ffn_w_in, v_ffn_w_out]
    upd = [_adamw(w_, g_, m_, v_) for w_, g_, m_, v_ in zip(weights, grads, ms, vs)]
    return (loss, grad_x, *grads, *[u_[0] for u_ in upd], *[u_[1] for u_ in upd], *[u_[2] for u_ in upd])
```

```python
import functools
import math

import jax
import jax.numpy as jnp
from jax import lax
from jax.experimental import pallas as pl
from jax.experimental.pallas import tpu as pltpu

F32, BF16 = jnp.float32, jnp.bfloat16
MESH = pl.DeviceIdType.MESH
V7X_VMEM_LIMIT_BYTES = 56 * 1024 * 1024
LANES = 128

GRID_W = 64
RET_HEADS = 8
RET_CHUNK = 128
RET_ROPE_BASE = 10000.0
GN_EPS = 1e-6
MLA_HEADS = 16
MLA_LORA = 512
MLA_D_NOPE = 128
MLA_D_ROPE = 64
MLA_D_V = 128
MLA_SCALE = (MLA_D_NOPE + MLA_D_ROPE) ** -0.5
AXIAL_ROPE_BASE = 10000.0
RMS_EPS = 1e-6
DEPTH = 4
DEEPNORM_ALPHA = (2 * DEPTH) ** 0.25
LN_EPS = 1e-5
ADAM_LR, ADAM_B1, ADAM_B2, ADAM_EPS, ADAM_WD, ADAM_STEP = 0.001, 0.9, 0.999, 1e-08, 0.01, 10
N_CHIPS = 4
N_DEV = 8


def _pick(dim, target, mult):
    best = None
    for d in range(mult, min(dim, target) + 1, mult):
        if dim % d == 0:
            best = d
    return dim if best is None else best


def _params(*sem):
    return pltpu.CompilerParams(dimension_semantics=sem, vmem_limit_bytes=V7X_VMEM_LIMIT_BYTES)


def _sigmoid(x):
    return 1.0 / (1.0 + jnp.exp(-x))


def _accumulate(step, nsteps, acc, part, write):
    if nsteps == 1:
        write(part)
        return

    @pl.when(step == 0)
    def _():
        acc[...] = part

    @pl.when(jnp.logical_and(step > 0, step < nsteps - 1))
    def _():
        acc[...] += part

    @pl.when(step == nsteps - 1)
    def _():
        write(acc[...] + part)


def _mm_nn(a, w, out_dtype=F32, name="mm_nn"):
    M, K = a.shape
    J, K2, n = w.shape
    assert K == K2
    tn = _pick(n, 1408, LANES)
    tk = _pick(K, 2048, LANES)
    tm = _pick(M, max(16, (6 << 20) // (tn * 4)), 16)
    npj, nk = n // tn, K // tk

    def body(a_ref, w_ref, o_ref, acc):
        part = jnp.dot(a_ref[...].astype(BF16), w_ref[...].astype(BF16), preferred_element_type=F32)

        def write(total):
            o_ref[...] = total.astype(o_ref.dtype)

        _accumulate(pl.program_id(2), nk, acc, part, write)

    return pl.pallas_call(
        body, grid=(M // tm, J * npj, nk),
        in_specs=[pl.BlockSpec((tm, tk), lambda i, j, k: (i, k)),
                  pl.BlockSpec((None, tk, tn), lambda i, j, k: (j // npj, k, j % npj))],
        out_specs=pl.BlockSpec((tm, tn), lambda i, j, k: (i, j)),
        out_shape=jax.ShapeDtypeStruct((M, J * n), out_dtype),
        scratch_shapes=[pltpu.VMEM((tm, tn), F32)],
        compiler_params=_params("parallel", "parallel", "arbitrary"), name=name)(a, w)


def _mm_nt(dy, w, add=None, name="mm_nt"):
    M, N = dy.shape
    J, K, n = w.shape
    assert N == J * n
    tko = _pick(K, 512, LANES)
    tc = _pick(n, 2048, LANES)
    tm = _pick(M, max(16, (4 << 20) // (tko * 4)), 16)
    npj = n // tc
    nc = J * npj
    has_add = add is not None

    def body(*refs):
        if has_add:
            dy_ref, w_ref, add_ref, o_ref, acc = refs
        else:
            dy_ref, w_ref, o_ref, acc = refs
        part = lax.dot_general(dy_ref[...].astype(BF16), w_ref[...].astype(BF16),
                               (((1,), (1,)), ((), ())), preferred_element_type=F32)

        def write(total):
            o_ref[...] = total + add_ref[...] if has_add else total

        _accumulate(pl.program_id(2), nc, acc, part, write)

    in_specs = [pl.BlockSpec((tm, tc), lambda i, ko, c: (i, c)),
                pl.BlockSpec((None, tko, tc), lambda i, ko, c: (c // npj, ko, c % npj))]
    args = [dy, w]
    if has_add:
        in_specs.append(pl.BlockSpec((tm, tko), lambda i, ko, c: (i, ko)))
        args.append(add)
    return pl.pallas_call(
        body, grid=(M // tm, K // tko, nc), in_specs=in_specs,
        out_specs=pl.BlockSpec((tm, tko), lambda i, ko, c: (i, ko)),
        out_shape=jax.ShapeDtypeStruct((M, K), F32),
        scratch_shapes=[pltpu.VMEM((tm, tko), F32)],
        compiler_params=_params("parallel", "parallel", "arbitrary"), name=name)(*args)


def _mm_tn(a, dy, J, name="mm_tn"):
    M, K = a.shape
    M2, N = dy.shape
    assert M == M2 and N % J == 0
    n = N // J
    tko = _pick(K, 1024, LANES)
    tn = _pick(n, 1408, LANES)
    tmc = _pick(M, 2176, 16)
    npj, nm = n // tn, M // tmc

    def body(a_ref, dy_ref, o_ref, acc):
        part = lax.dot_general(a_ref[...].astype(BF16), dy_ref[...].astype(BF16),
                               (((0,), (0,)), ((), ())), preferred_element_type=F32)

        def write(total):
            o_ref[...] = total

        _accumulate(pl.program_id(2), nm, acc, part, write)

    return pl.pallas_call(
        body, grid=(K // tko, J * npj, nm),
        in_specs=[pl.BlockSpec((tmc, tko), lambda ko, j, m: (m, ko)),
                  pl.BlockSpec((tmc, tn), lambda ko, j, m: (m, j))],
        out_specs=pl.BlockSpec((None, tko, tn), lambda ko, j, m: (j // npj, ko, j % npj)),
        out_shape=jax.ShapeDtypeStruct((J, K, n), F32),
        scratch_shapes=[pltpu.VMEM((tko, tn), F32)],
        compiler_params=_params("parallel", "parallel", "arbitrary"), name=name)(a, dy)


def _rowwise(name, fn, T, tr, n_xt, row_in, grp_in=(), const_in=(), row_out=(), gsum_w=(), tsum_w=(), ncol=1):
    nt = T // tr
    n_in = len(row_in) + len(grp_in) + len(const_in)
    n_ro, n_gs = len(row_out), len(gsum_w)
    assert ncol == 1 or not (gsum_w or tsum_w)

    def body(*refs):
        t = pl.program_id(0)
        vals = [r[...] for r in refs[:n_in]]
        rv = vals[:len(row_in)]
        gv = vals[len(row_in):len(row_in) + len(grp_in)]
        cv = vals[len(row_in) + len(grp_in):]
        ro, gs, ts = fn(rv, gv, cv)
        outs = refs[n_in:]
        for ref, val in zip(outs[:n_ro], ro):
            ref[...] = val.astype(ref.dtype)
        first_g = jnp.logical_or(t == 0, t == n_xt)
        for ref, val, first in ([(r, v, first_g) for r, v in zip(outs[n_ro:n_ro + n_gs], gs)]
                                + [(r, v, t == 0) for r, v in zip(outs[n_ro + n_gs:], ts)]):
            s = jnp.sum(val, axis=0, keepdims=True)

            @pl.when(first)
            def _(ref=ref, s=s):
                ref[...] = s

            @pl.when(jnp.logical_not(first))
            def _(ref=ref, s=s):
                ref[...] += s

    in_specs, args = [], []
    for spec in row_in:
        arr, width = spec[:2]
        lead = spec[2] if len(spec) > 2 else None
        step = spec[3] if len(spec) > 3 else 1
        if lead is not None:
            in_specs.append(pl.BlockSpec((None, tr, width), lambda t, cb, lead=lead, step=step: (lead, t, cb * step)))
        else:
            in_specs.append(pl.BlockSpec((tr, width), lambda t, cb, step=step: (t, cb * step)))
        args.append(arr)
    for arr in grp_in:
        in_specs.append(pl.BlockSpec((None,) + arr.shape[1:], lambda t, cb: (jnp.where(t >= n_xt, 1, 0), 0, 0, 0)))
        args.append(arr)
    for arr in const_in:
        in_specs.append(pl.BlockSpec(arr.shape, lambda t, cb: (0, 0, 0)))
        args.append(arr)
    out_specs, out_shape = [], []
    for wtot, wblk, dt in row_out:
        out_specs.append(pl.BlockSpec((tr, wblk), lambda t, cb: (t, cb)))
        out_shape.append(jax.ShapeDtypeStruct((T, wtot), dt))
    for w in gsum_w:
        out_specs.append(pl.BlockSpec((None, 1, w), lambda t, cb: (jnp.where(t >= n_xt, 1, 0), 0, 0)))
        out_shape.append(jax.ShapeDtypeStruct((2, 1, w), F32))
    for w in tsum_w:
        out_specs.append(pl.BlockSpec((1, w), lambda t, cb: (0, 0)))
        out_shape.append(jax.ShapeDtypeStruct((1, w), F32))
    sem = ("arbitrary", "arbitrary") if (gsum_w or tsum_w) else ("parallel", "parallel")
    return pl.pallas_call(body, grid=(nt, ncol), in_specs=in_specs, out_specs=out_specs, out_shape=out_shape,
                          compiler_params=_params(*sem), name=name)(*args)


def _modulate(h, mod, T, tr, n_xt):
    D = h.shape[1]

    def fn(rv, gv, cv):
        m = gv[0]
        return [rv[0] * (1.0 + m[1]) + m[0]], [], []

    return _rowwise("modulate", fn, T, tr, n_xt, [(h, D)], [mod], row_out=[(D, D, BF16)])[0]


def _ln_stats(r):
    mu = jnp.mean(r, axis=-1, keepdims=True)
    xc = r - mu
    var = jnp.mean(xc * xc, axis=-1, keepdims=True)
    rstd = lax.rsqrt(var + LN_EPS)
    return xc * rstd, rstd


def _ln_fwd(h, y, mod, lnp, gate_row, mod_next, next_rows, T, tr, n_xt):
    D = h.shape[1]
    with_u = mod_next is not None

    def fn(rv, gv, cv):
        r = DEEPNORM_ALPHA * rv[0] + gv[0][gate_row] * rv[1]
        xhat, _ = _ln_stats(r)
        out = xhat * cv[0][0] + cv[0][1]
        if not with_u:
            return [out], [], []
        mn = gv[1]
        return [out, out * (1.0 + mn[next_rows[1]]) + mn[next_rows[0]]], [], []

    grp = [mod, mod_next] if with_u else [mod]
    outs = [(D, D, F32), (D, D, BF16)] if with_u else [(D, D, F32)]
    res = _rowwise("ln_fwd", fn, T, tr, n_xt, [(h, D), (y, D)], grp, [lnp], row_out=outs)
    return (res[0], res[1]) if with_u else (res[0], None)


def _ln_bwd(h, y, dout, mod, lnp, gate_row, T, tr, n_xt):
    D = h.shape[1]

    def fn(rv, gv, cv):
        g = gv[0][gate_row]
        r = DEEPNORM_ALPHA * rv[0] + g * rv[1]
        xhat, rstd = _ln_stats(r)
        dxh = rv[2] * cv[0][0]
        m1 = jnp.mean(dxh, axis=-1, keepdims=True)
        m2 = jnp.mean(dxh * xhat, axis=-1, keepdims=True)
        dr = rstd * (dxh - m1 - xhat * m2)
        return [DEEPNORM_ALPHA * dr, g * dr], [dr * rv[1]], [rv[2] * xhat, rv[2]]

    return _rowwise("ln_bwd", fn, T, tr, n_xt, [(h, D), (y, D), (dout, D)], [mod], [lnp],
                    row_out=[(D, D, F32), (D, D, BF16)], gsum_w=[D], tsum_w=[D, D])


def _mod_bwd(dh_res, du, h, mod, scale_row, T, tr, n_xt):
    D = h.shape[1]

    def fn(rv, gv, cv):
        return [rv[0] + rv[1] * (1.0 + gv[0][scale_row])], [rv[1] * rv[2], rv[1]], []

    return _rowwise("mod_bwd", fn, T, tr, n_xt, [(dh_res, D), (du, D), (h, D)], [mod],
                    row_out=[(D, D, F32)], gsum_w=[D, D])


def _swiglu_fwd(a, b, T, tr, n_xt):
    F = a.shape[1]
    wc = _pick(F, 1408, LANES)

    def fn(rv, gv, cv):
        return [rv[0] * _sigmoid(rv[0]) * rv[1]], [], []

    return _rowwise("swiglu_fwd", fn, T, tr, n_xt, [(a, wc), (b, wc)], row_out=[(F, wc, BF16)], ncol=F // wc)[0]


def _swiglu_bwd(a, b, dact, T, tr, n_xt):
    F = a.shape[1]
    wc = _pick(F, 1408, LANES)

    def fn(rv, gv, cv):
        av, bv, dv = rv
        sg = _sigmoid(av)
        return [dv * bv * (sg * (1.0 + av * (1.0 - sg))), dv * av * sg], [], []

    return _rowwise("swiglu_bwd", fn, T, tr, n_xt, [(a, wc), (b, wc), (dact, wc)],
                    row_out=[(F, wc, BF16), (F, wc, BF16)], ncol=F // wc)


def _gn(o):
    mu = jnp.mean(o, axis=-1, keepdims=True)
    xc = o - mu
    var = jnp.mean(xc * xc, axis=-1, keepdims=True)
    rstd = lax.rsqrt(var + GN_EPS)
    return xc * rstd, rstd


def _gate_fwd(gf, gb, o, T, tr, n_xt, dv):
    W = gf.shape[1]

    def fn(rv, gv, cv):
        xf, _ = _gn(rv[2])
        xb, _ = _gn(rv[3])
        return [rv[0] * _sigmoid(rv[0]) * xf + rv[1] * _sigmoid(rv[1]) * xb], [], []

    return _rowwise("gate_fwd", fn, T, tr, n_xt, [(gf, dv), (gb, dv), (o, dv, 0), (o, dv, 1)],
                    row_out=[(W, dv, BF16)], ncol=W // dv)[0]


def _gate_bwd(gf, gb, o, dy, T, tr, n_xt, dv):
    W = gf.shape[1]

    def fn(rv, gv, cv):
        outs_g, outs_o = [], []
        for g, ov in ((rv[0], rv[2]), (rv[1], rv[3])):
            xh, rstd = _gn(ov)
            sg = _sigmoid(g)
            outs_g.append(rv[4] * xh * (sg * (1.0 + g * (1.0 - sg))))
            dxh = rv[4] * g * sg
            m1 = jnp.mean(dxh, axis=-1, keepdims=True)
            m2 = jnp.mean(dxh * xh, axis=-1, keepdims=True)
            outs_o.append(rstd * (dxh - m1 - xh * m2))
        return outs_g + outs_o, [], []

    return _rowwise("gate_bwd", fn, T, tr, n_xt,
                    [(gf, dv), (gb, dv), (o, dv, 0), (o, dv, 1), (dy, dv)],
                    row_out=[(W, dv, BF16)] * 4, ncol=W // dv)


def _add_dirs(dq, dk, dv_, T, tr, n_xt):
    ws = [dq.shape[2], dk.shape[2], dv_.shape[2]]

    def fn(rv, gv, cv):
        return [rv[0] + rv[1], rv[2] + rv[3], rv[4] + rv[5]], [], []

    row_in = []
    for arr, w in zip((dq, dk, dv_), ws):
        row_in += [(arr, w, 0), (arr, w, 1)]
    return _rowwise("add_dirs", fn, T, tr, n_xt, row_in, row_out=[(w, w, BF16) for w in ws])


def _swap16(x):
    lane = lax.broadcasted_iota(jnp.int32, x.shape, x.ndim - 1)
    return jnp.where(lane % 32 < 16, pltpu.roll(x, LANES - 16, x.ndim - 1), pltpu.roll(x, 16, x.ndim - 1))


def _rope2d(x, cos, sin, transpose):
    if transpose:
        return x * cos + _swap16(x * sin)
    return x * cos + _swap16(x) * sin


def _rms(x, g):
    rstd = lax.rsqrt(jnp.mean(x * x, axis=-1, keepdims=True) + RMS_EPS)
    return x * rstd, rstd


def _lora_fwd(lora, gq_gkv, cos, sin, T, tr, n_xt):
    L = MLA_LORA

    def fn(rv, gv, cv):
        x = rv[0]
        xq, _ = _rms(x[:, :L], None)
        xkv, _ = _rms(x[:, L:2 * L], None)
        return [xq * cv[0][0], xkv * cv[0][1], _rope2d(x[:, 2 * L:], rv[1], rv[2], False)], [], []

    return _rowwise("lora_fwd", fn, T, tr, n_xt, [(lora, 2 * L + LANES), (cos, LANES), (sin, LANES)],
                    const_in=[gq_gkv], row_out=[(L, L, BF16), (L, L, BF16), (LANES, LANES, BF16)])


def _lora_bwd(lora, dq, dkv, dkr, gq_gkv, cos, sin, T, tr, n_xt):
    L = MLA_LORA

    def fn(rv, gv, cv):
        x = rv[0]
        outs, sums = [], []
        for xs, dy, g in ((x[:, :L], rv[1], cv[0][0]), (x[:, L:2 * L], rv[2], cv[0][1])):
            xh, rstd = _rms(xs, None)
            dxh = dy * g
            outs.append(rstd * (dxh - xh * jnp.mean(dxh * xh, axis=-1, keepdims=True)))
            sums.append(dy * xh)
        outs.append(_rope2d(rv[3], rv[4], rv[5], True))
        return [jnp.concatenate(outs, axis=1)], [], sums

    W = 2 * L + LANES
    return _rowwise("lora_bwd", fn, T, tr, n_xt,
                    [(lora, W), (dq, L), (dkv, L), (dkr, LANES), (cos, LANES), (sin, LANES)],
                    const_in=[gq_gkv], row_out=[(W, W, BF16)], tsum_w=[L, L])


def _qrope(q, cos, sin, transpose, T, tr, n_xt):
    W = q.shape[1]

    def fn(rv, gv, cv):
        x = rv[0]
        return [jnp.concatenate([x[:, :LANES], _rope2d(x[:, LANES:], rv[1], rv[2], transpose)], axis=1)], [], []

    return _rowwise("qrope_bwd" if transpose else "qrope_fwd", fn, T, tr, n_xt,
                    [(q, 2 * LANES), (cos, LANES, None, 0), (sin, LANES, None, 0)],
                    row_out=[(W, 2 * LANES, BF16)], ncol=W // (2 * LANES))[0]


def _loss_grad(h, target, T, tr, n_xt):
    D = h.shape[1]
    nt = T // tr

    def body(h_ref, t_ref, dh_ref, s_ref):
        t = pl.program_id(0)
        diff = jnp.where(t < n_xt, h_ref[...] - t_ref[...], 0.0)
        dh_ref[...] = diff * (1.0 / D)
        s = jnp.sum(diff * diff, axis=0, keepdims=True)

        @pl.when(t == 0)
        def _():
            s_ref[...] = s

        @pl.when(t != 0)
        def _():
            s_ref[...] += s

    return pl.pallas_call(
        body, grid=(nt,),
        in_specs=[pl.BlockSpec((tr, D), lambda t: (t, 0)),
                  pl.BlockSpec((tr, D), lambda t: (jnp.minimum(t, n_xt - 1), 0))],
        out_specs=[pl.BlockSpec((tr, D), lambda t: (t, 0)), pl.BlockSpec((1, D), lambda t: (0, 0))],
        out_shape=[jax.ShapeDtypeStruct((T, D), F32), jax.ShapeDtypeStruct((1, D), F32)],
        compiler_params=_params("arbitrary"), name="loss_grad")(h, target)


def _adamw(w, g, m, v):
    shape = w.shape
    C = shape[-1] if w.ndim > 1 else shape[0]
    R = w.size // C
    tr = _pick(R, max(8, (2 << 20) // (C * 4)), 8)
    c1 = 1.0 - ADAM_B1 ** ADAM_STEP
    c2 = 1.0 - ADAM_B2 ** ADAM_STEP

    def body(w_ref, g_ref, m_ref, v_ref, d_ref, nm_ref, nv_ref):
        gv = g_ref[...]
        nm = ADAM_B1 * m_ref[...] + (1.0 - ADAM_B1) * gv
        nv = ADAM_B2 * v_ref[...] + (1.0 - ADAM_B2) * (gv * gv)
        nm_ref[...] = nm
        nv_ref[...] = nv
        d_ref[...] = -ADAM_LR * ((nm / c1) / (jnp.sqrt(nv / c2) + ADAM_EPS) + ADAM_WD * w_ref[...])

    spec = pl.BlockSpec((tr, C), lambda i: (i, 0))
    outs = pl.pallas_call(
        body, grid=(R // tr,), in_specs=[spec] * 4, out_specs=[spec] * 3,
        out_shape=[jax.ShapeDtypeStruct((R, C), F32)] * 3,
        compiler_params=_params("parallel"), name="adamw")(*[t.reshape(R, C) for t in (w, g, m, v)])
    return tuple(o.reshape(shape) for o in outs)


def _ret_chunk(d, s, ncx, ncc):
    fwd = jnp.where(s < ncc, ncx + s, s - ncc)
    bwd = jnp.where(s < ncc, ncx + ncc - 1 - s, ncx - 1 - (s - ncc))
    return jnp.where(d == 0, fwd, bwd)


def _rot_half(x, cos, sin, transpose):
    half = x.shape[-1] // 2
    if transpose:
        return x * cos + pltpu.roll(x * sin, half, x.ndim - 1)
    return x * cos + pltpu.roll(x, half, x.ndim - 1) * sin


def _dot_nt(a, b):
    return lax.dot_general(a, b, (((1,), (1,)), ((), ())), preferred_element_type=F32)


def _dot_tn(a, b):
    return lax.dot_general(a, b, (((0,), (0,)), ((), ())), preferred_element_type=F32)


def _dot(a, b):
    return jnp.dot(a, b, preferred_element_type=F32)


def _ret_fwd(q, k, v, tabs, ncx, ncc):
    T, C, H = q.shape[0], RET_CHUNK, RET_HEADS
    dk, dv = q.shape[1] // H, v.shape[1] // H
    ns = ncx + ncc
    kscale = dk ** -0.5

    def body(q_ref, k_ref, v_ref, cos_ref, sin_ref, intra_ref, qd_ref, kd_ref, cd_ref, o_ref, st_ref, s_scr):
        @pl.when(pl.program_id(2) == 0)
        def _():
            s_scr[...] = jnp.zeros_like(s_scr)

        cos, sin = cos_ref[...], sin_ref[...]
        qv = _rot_half(q_ref[...], cos, sin, False)
        kv = _rot_half(k_ref[...], cos, sin, False) * kscale
        vb = v_ref[...].astype(BF16)
        S = s_scr[...]
        st_ref[...] = S
        scores = _dot_nt(qv.astype(BF16), kv.astype(BF16)) * intra_ref[...]
        o_ref[...] = _dot(scores.astype(BF16), vb) + _dot((qv * qd_ref[...]).astype(BF16), S.astype(BF16))
        s_scr[...] = S * cd_ref[...] + _dot_tn((kv * kd_ref[...]).astype(BF16), vb)

    chunk = lambda d, h, s: _ret_chunk(d, s, ncx, ncc)
    tab = lambda shape: pl.BlockSpec((None, None) + shape, lambda d, h, s: (d, h, 0, 0))
    return pl.pallas_call(
        body, grid=(2, H, ns),
        in_specs=[pl.BlockSpec((C, dk), lambda d, h, s: (chunk(d, h, s), h)),
                  pl.BlockSpec((C, dk), lambda d, h, s: (chunk(d, h, s), h)),
                  pl.BlockSpec((C, dv), lambda d, h, s: (chunk(d, h, s), h)),
                  pl.BlockSpec((C, dk), lambda d, h, s: (chunk(d, h, s), 0)),
                  pl.BlockSpec((C, dk), lambda d, h, s: (chunk(d, h, s), 0)),
                  tab((C, C)), tab((C, dk)), tab((C, dk)), tab((1, dv))],
        out_specs=[pl.BlockSpec((None, C, dv), lambda d, h, s: (d, chunk(d, h, s), h)),
                   pl.BlockSpec((None, None, None, dk, dv), lambda d, h, s: (d, h, s, 0, 0))],
        out_shape=[jax.ShapeDtypeStruct((2, T, H * dv), F32), jax.ShapeDtypeStruct((2, H, ns, dk, dv), F32)],
        scratch_shapes=[pltpu.VMEM((dk, dv), F32)],
        compiler_params=_params("parallel", "parallel", "arbitrary"), name="ret_fwd",
    )(q, k, v, tabs["cos"], tabs["sin"], tabs["intra"], tabs["qd"], tabs["kd"], tabs["cd"])


def _ret_bwd(q, k, v, do, states, tabs, ncx, ncc):
    T, C, H = q.shape[0], RET_CHUNK, RET_HEADS
    dk, dv = q.shape[1] // H, v.shape[1] // H
    ns = ncx + ncc
    kscale = dk ** -0.5

    def body(q_ref, k_ref, v_ref, do_ref, st_ref, cos_ref, sin_ref, intra_ref, qd_ref, kd_ref, cd_ref,
             dm_ref, wq_ref, wk_ref, dq_ref, dk_ref, dv_ref, dl_ref, ds_scr):
        first = pl.program_id(2) == 0

        @pl.when(first)
        def _():
            ds_scr[...] = jnp.zeros_like(ds_scr)

        cos, sin = cos_ref[...], sin_ref[...]
        qv = _rot_half(q_ref[...], cos, sin, False)
        kv = _rot_half(k_ref[...], cos, sin, False) * kscale
        qb, kb = qv.astype(BF16), kv.astype(BF16)
        vb = v_ref[...].astype(BF16)
        dob = do_ref[...]
        intra, qd, kd, cd = intra_ref[...], qd_ref[...], kd_ref[...], cd_ref[...]
        S, dS = st_ref[...], ds_scr[...]
        Sb, dSb = S.astype(BF16), dS.astype(BF16)
        P = _dot_nt(qb, kb) * intra
        dP_raw = _dot_nt(dob, vb)
        dPb = (dP_raw * intra).astype(BF16)
        dq_cross = _dot_nt(dob, Sb) * qd
        dq_rot = _dot(dPb, kb) + dq_cross
        dk_state = _dot_nt(vb, dSb) * kd
        dk_rot = _dot_tn(dPb, qb) + dk_state
        dv_ref[...] = _dot_tn(P.astype(BF16), dob) + _dot((kv * kd).astype(BF16), dSb)
        dq_ref[...] = _rot_half(dq_rot, cos, sin, True)
        dk_ref[...] = _rot_half(dk_rot, cos, sin, True) * kscale
        dlam = (jnp.sum(dm_ref[...] * P * dP_raw) + jnp.sum(wq_ref[...] * qv * dq_cross)
                + C * jnp.sum(cd * S * dS) + jnp.sum(wk_ref[...] * kv * dk_state))
        dl = jnp.full(dl_ref.shape, dlam, F32)

        @pl.when(first)
        def _():
            dl_ref[...] = dl

        @pl.when(jnp.logical_not(first))
        def _():
            dl_ref[...] += dl

        ds_scr[...] = cd * dS + _dot_tn((qv * qd).astype(BF16), dob)

    chunk = lambda d, h, s: _ret_chunk(d, ns - 1 - s, ncx, ncc)
    tab = lambda shape: pl.BlockSpec((None, None) + shape, lambda d, h, s: (d, h, 0, 0))
    dtab = lambda shape: pl.BlockSpec((None,) + shape, lambda d, h, s: (d, 0, 0))
    return pl.pallas_call(
        body, grid=(2, H, ns),
        in_specs=[pl.BlockSpec((C, dk), lambda d, h, s: (chunk(d, h, s), h)),
                  pl.BlockSpec((C, dk), lambda d, h, s: (chunk(d, h, s), h)),
                  pl.BlockSpec((C, dv), lambda d, h, s: (chunk(d, h, s), h)),
                  pl.BlockSpec((None, C, dv), lambda d, h, s: (d, chunk(d, h, s), h)),
                  pl.BlockSpec((None, None, None, dk, dv), lambda d, h, s: (d, h, ns - 1 - s, 0, 0)),
                  pl.BlockSpec((C, dk), lambda d, h, s: (chunk(d, h, s), 0)),
                  pl.BlockSpec((C, dk), lambda d, h, s: (chunk(d, h, s), 0)),
                  tab((C, C)), tab((C, dk)), tab((C, dk)), tab((1, dv)),
                  dtab((C, C)), dtab((C, dk)), dtab((C, dk))],
        out_specs=[pl.BlockSpec((None, C, dk), lambda d, h, s: (d, chunk(d, h, s), h)),
                   pl.BlockSpec((None, C, dk), lambda d, h, s: (d, chunk(d, h, s), h)),
                   pl.BlockSpec((None, C, dv), lambda d, h, s: (d, chunk(d, h, s), h)),
                   pl.BlockSpec((None, None, 1, LANES), lambda d, h, s: (d, h, 0, 0))],
        out_shape=[jax.ShapeDtypeStruct((2, T, H * dk), F32), jax.ShapeDtypeStruct((2, T, H * dk), F32),
                   jax.ShapeDtypeStruct((2, T, H * dv), F32), jax.ShapeDtypeStruct((2, H, 1, LANES), F32)],
        scratch_shapes=[pltpu.VMEM((dk, dv), F32)],
        compiler_params=_params("parallel", "parallel", "arbitrary"), name="ret_bwd",
    )(q, k, v, do, states, tabs["cos"], tabs["sin"], tabs["intra"], tabs["qd"], tabs["kd"], tabs["cd"],
      tabs["dmat"], tabs["wq"], tabs["wk"])


def _ret_tables(decay_logit, Nx, Nc, dk, dv):
    C, H = RET_CHUNK, RET_HEADS
    inv = RET_ROPE_BASE ** (-jnp.linspace(0.0, 1.0, dk // 2, dtype=F32))
    ang = jnp.arange(Nx, dtype=F32)[:, None] * inv[None, :]
    cos, sin = jnp.cos(ang), jnp.sin(ang)
    cosf = jnp.concatenate([jnp.concatenate([cos, cos], 1), jnp.ones((Nc, dk), F32)], 0)
    sinf = jnp.concatenate([jnp.concatenate([-sin, sin], 1), jnp.zeros((Nc, dk), F32)], 0)
    lg = jax.nn.log_sigmoid(decay_logit.astype(F32))
    idx = jnp.arange(C, dtype=F32)
    diff = idx[:, None] - idx[None, :]
    dmat = jnp.stack([jnp.maximum(diff, 0.0), jnp.maximum(-diff, 0.0)])
    mask = jnp.stack([diff >= 0, diff <= 0])
    intra = jnp.where(mask[:, None], jnp.exp(lg[:, :, None, None] * dmat[:, None]), 0.0)
    wq = jnp.stack([idx + 1.0, C - idx])
    wk = jnp.stack([C - 1.0 - idx, idx])
    qd = jnp.exp(lg[:, :, None] * wq[:, None, :])
    kd = jnp.exp(lg[:, :, None] * wk[:, None, :])
    cd = jnp.exp(lg * C)
    bc = lambda t, w: jnp.broadcast_to(t[..., None], t.shape + (w,))
    return dict(cos=cosf, sin=sinf, intra=intra, qd=bc(qd, dk), kd=bc(kd, dk),
                cd=jnp.broadcast_to(cd[:, :, None, None], (2, H, 1, dv)),
                dmat=dmat, wq=bc(wq, dk), wk=bc(wk, dk), lg=lg)


def _attn_fwd(q, kn, v, kr, Nx, tq):
    T, H = q.shape[0], MLA_HEADS
    n_xq = Nx // tq

    def body(q_ref, kn_ref, v_ref, kr_ref, o_ref, lse_ref):
        def attend(lo):
            kcat = jnp.concatenate([kn_ref[lo:, :], kr_ref[lo:, :]], axis=1)
            s = _dot_nt(q_ref[...], kcat) * MLA_SCALE
            m = jnp.max(s, axis=-1, keepdims=True)
            p = jnp.exp(s - m)
            l = jnp.sum(p, axis=-1, keepdims=True)
            o_ref[...] = (_dot(p.astype(BF16), v_ref[lo:, :]) / l).astype(o_ref.dtype)
            lse_ref[...] = m + jnp.log(l)

        @pl.when(pl.program_id(1) < n_xq)
        def _():
            attend(0)

        @pl.when(pl.program_id(1) >= n_xq)
        def _():
            attend(Nx)

    return pl.pallas_call(
        body, grid=(H, T // tq),
        in_specs=[pl.BlockSpec((tq, 2 * LANES), lambda h, i: (i, h)),
                  pl.BlockSpec((T, LANES), lambda h, i: (0, h)),
                  pl.BlockSpec((T, LANES), lambda h, i: (0, h)),
                  pl.BlockSpec((T, LANES), lambda h, i: (0, 0))],
        out_specs=[pl.BlockSpec((tq, LANES), lambda h, i: (i, h)),
                   pl.BlockSpec((None, tq, 1), lambda h, i: (h, i, 0))],
        out_shape=[jax.ShapeDtypeStruct((T, H * LANES), BF16), jax.ShapeDtypeStruct((H, T, 1), F32)],
        compiler_params=_params("parallel", "arbitrary"), name="attn_fwd")(q, kn, v, kr)


def _attn_bwd(q, kn, v, kr, do, lse, Nx, tq):
    T, H = q.shape[0], MLA_HEADS
    n_xq, nq = Nx // tq, T // tq

    def body(q_ref, kn_ref, v_ref, kr_ref, do_ref, lse_ref, dq_ref, dkn_ref, dv_ref, dkr_ref, dk_acc, dv_acc):
        h, i = pl.program_id(0), pl.program_id(1)

        @pl.when(i == 0)
        def _():
            dk_acc[...] = jnp.zeros_like(dk_acc)
            dv_acc[...] = jnp.zeros_like(dv_acc)

        def attend(lo):
            kcat = jnp.concatenate([kn_ref[lo:, :], kr_ref[lo:, :]], axis=1)
            qb, dob = q_ref[...], do_ref[...]
            p = jnp.exp(_dot_nt(qb, kcat) * MLA_SCALE - lse_ref[...])
            dp = _dot_nt(dob, v_ref[lo:, :])
            delta = jnp.sum(p * dp, axis=-1, keepdims=True)
            dsb = (p * (dp - delta) * MLA_SCALE).astype(BF16)
            dq_ref[...] = _dot(dsb, kcat)
            dk_acc[lo:, :] += _dot_tn(dsb, qb)
            dv_acc[lo:, :] += _dot_tn(p.astype(BF16), dob)

        @pl.when(i < n_xq)
        def _():
            attend(0)

        @pl.when(i >= n_xq)
        def _():
            attend(Nx)

        @pl.when(i == nq - 1)
        def _():
            dkn_ref[...] = dk_acc[:, :LANES].astype(dkn_ref.dtype)
            dv_ref[...] = dv_acc[...].astype(dv_ref.dtype)

        @pl.when(jnp.logical_and(i == nq - 1, h == 0))
        def _():
            dkr_ref[...] = dk_acc[:, LANES:]

        @pl.when(jnp.logical_and(i == nq - 1, h != 0))
        def _():
            dkr_ref[...] += dk_acc[:, LANES:]

    return pl.pallas_call(
        body, grid=(H, nq),
        in_specs=[pl.BlockSpec((tq, 2 * LANES), lambda h, i: (i, h)),
                  pl.BlockSpec((T, LANES), lambda h, i: (0, h)),
                  pl.BlockSpec((T, LANES), lambda h, i: (0, h)),
                  pl.BlockSpec((T, LANES), lambda h, i: (0, 0)),
                  pl.BlockSpec((tq, LANES), lambda h, i: (i, h)),
                  pl.BlockSpec((None, tq, 1), lambda h, i: (h, i, 0))],
        out_specs=[pl.BlockSpec((tq, 2 * LANES), lambda h, i: (i, h)),
                   pl.BlockSpec((T, LANES), lambda h, i: (0, h)),
                   pl.BlockSpec((T, LANES), lambda h, i: (0, h)),
                   pl.BlockSpec((T, LANES), lambda h, i: (0, 0))],
        out_shape=[jax.ShapeDtypeStruct((T, H * 2 * LANES), F32), jax.ShapeDtypeStruct((T, H * LANES), BF16),
                   jax.ShapeDtypeStruct((T, H * LANES), BF16), jax.ShapeDtypeStruct((T, LANES), F32)],
        scratch_shapes=[pltpu.VMEM((T, 2 * LANES), F32), pltpu.VMEM((T, LANES), F32)],
        compiler_params=_params("arbitrary", "arbitrary"), name="attn_bwd")(q, kn, v, kr, do, lse)


def _mla_tables(Nx, Nc):
    ad = MLA_D_ROPE // 2
    inv = AXIAL_ROPE_BASE ** (-jnp.arange(ad // 2, dtype=F32) * 2.0 / ad)
    t = jnp.arange(Nx)
    rang = (t // GRID_W).astype(F32)[:, None] * inv[None, :]
    cang = (t % GRID_W).astype(F32)[:, None] * inv[None, :]
    rc, rs, cc, cs = jnp.cos(rang), jnp.sin(rang), jnp.cos(cang), jnp.sin(cang)
    pad1, pad0 = jnp.ones((Nx, LANES - MLA_D_ROPE), F32), jnp.zeros((Nx, LANES - MLA_D_ROPE), F32)
    cos = jnp.concatenate([rc, rc, cc, cc, pad1], 1)
    sin = jnp.concatenate([-rs, rs, -cs, cs, pad0], 1)
    return (jnp.concatenate([cos, jnp.ones((Nc, LANES), F32)], 0),
            jnp.concatenate([sin, jnp.zeros((Nc, LANES), F32)], 0))


def _place():
    x, y, c = lax.axis_index("x"), lax.axis_index("y"), lax.axis_index("c")
    return x, y, c


def _all_gather8(v):
    R = v.shape[0]

    def body(v_ref, g_ref, s_ref, send_sems, recv_sems):
        x, y, c = _place()
        me = 4 * x + 2 * y + c
        g_ref[me] = v_ref[...]
        copies = []
        for k in range(1, N_DEV):
            peer = (x ^ (k >> 2), y ^ ((k >> 1) & 1), c ^ (k & 1))
            copies.append(pltpu.make_async_remote_copy(
                src_ref=v_ref, dst_ref=g_ref.at[me], send_sem=send_sems.at[k - 1], recv_sem=recv_sems.at[k - 1],
                device_id=peer, device_id_type=MESH))
        for cp in copies:
            cp.start()
        for cp in copies:
            cp.wait_recv()
        for cp in copies:
            cp.wait_send()
        acc = g_ref[0]
        for k in range(1, N_DEV):
            acc = acc + g_ref[k]
        s_ref[...] = acc

    vm = pl.BlockSpec(memory_space=pltpu.VMEM)
    return pl.pallas_call(
        body, in_specs=[vm], out_specs=[vm, vm],
        out_shape=[jax.ShapeDtypeStruct((N_DEV, R, LANES), F32), jax.ShapeDtypeStruct((R, LANES), F32)],
        scratch_shapes=[pltpu.SemaphoreType.DMA((N_DEV - 1,)), pltpu.SemaphoreType.DMA((N_DEV - 1,))],
        compiler_params=pltpu.CompilerParams(vmem_limit_bytes=V7X_VMEM_LIMIT_BYTES), name="all_gather8")(v)


def _other_chips(x, y):
    return [(1 - x, y), (x, 1 - y), (1 - x, 1 - y)]


def _place_own(w, chip_idx):
    R, C = w.shape
    tr = _pick(R, max(16, (2 << 20) // (C * 4)), 16)

    def body(s_ref, w_ref, o_ref):
        o_ref[...] = w_ref[...].astype(o_ref.dtype)

    return pl.pallas_call(
        body,
        grid_spec=pltpu.PrefetchScalarGridSpec(
            num_scalar_prefetch=1, grid=(R // tr,),
            in_specs=[pl.BlockSpec((tr, C), lambda i, s: (i, 0))],
            out_specs=pl.BlockSpec((None, tr, C), lambda i, s: (s[0], i, 0))),
        out_shape=jax.ShapeDtypeStruct((N_CHIPS, R, C), BF16),
        compiler_params=_params("parallel"), name="place_own")(chip_idx, w)


def _gather_chips(buf):
    J, R, C = buf.shape
    half = R // 2
    assert R % 2 == 0

    def body(in_ref, o_ref, send_sems, recv_sems):
        x, y, c = _place()
        chip = 2 * x + y
        mine, sibling = pl.ds(c * half, half), (x, y, 1 - c)
        chips = _other_chips(x, y)

        def copy(k, part, to):
            return pltpu.make_async_remote_copy(src_ref=part, dst_ref=part, send_sem=send_sems.at[k],
                                                recv_sem=recv_sems.at[k], device_id=to, device_id_type=MESH)

        first = [copy(j, o_ref.at[chip, mine], (*ch, c)) for j, ch in enumerate(chips)]
        for cp in first:
            cp.start()
        landed = [o_ref.at[2 * ch[0] + ch[1], mine] for ch in chips]
        passed = [copy(3 + j, landed[j], sibling) for j in range(3)]
        for j in range(3):
            copy(j, landed[j], sibling).wait_recv()
            passed[j].start()
        for j, ch in enumerate(chips):
            copy(3 + j, o_ref.at[2 * ch[0] + ch[1], pl.ds((1 - c) * half, half)], sibling).wait_recv()
        for cp in first + passed:
            cp.wait_send()

    hbm = pl.BlockSpec(memory_space=pl.ANY)
    return pl.pallas_call(
        body, in_specs=[hbm], out_specs=hbm, out_shape=jax.ShapeDtypeStruct(buf.shape, buf.dtype),
        input_output_aliases={0: 0},
        scratch_shapes=[pltpu.SemaphoreType.DMA((6,)), pltpu.SemaphoreType.DMA((6,))],
        name="gather_chips")(buf)


def _swap_halves(g):
    J, R, C = g.shape
    half = R // 2

    def body(g_ref, o_ref, send_sem, recv_sem):
        x, y, c = _place()
        cp = pltpu.make_async_remote_copy(src_ref=g_ref.at[:, pl.ds((1 - c) * half, half), :], dst_ref=o_ref,
                                          send_sem=send_sem, recv_sem=recv_sem, device_id=(x, y, 1 - c),
                                          device_id_type=MESH)
        cp.start()
        cp.wait()

    hbm = pl.BlockSpec(memory_space=pl.ANY)
    return pl.pallas_call(
        body, in_specs=[hbm], out_specs=hbm, out_shape=jax.ShapeDtypeStruct((J, half, C), g.dtype),
        scratch_shapes=[pltpu.SemaphoreType.DMA, pltpu.SemaphoreType.DMA], name="swap_halves")(g)


def _add_half(g, r, place_idx):
    J, R, C = g.shape
    half = R // 2
    tr = _pick(half, max(16, (2 << 20) // (C * 4)), 16)
    nb = half // tr

    def body(s_ref, g_ref, r_ref, o_ref, ob_ref):
        acc = g_ref[...] + r_ref[...]
        o_ref[...] = acc
        ob_ref[...] = acc.astype(ob_ref.dtype)

    spec = pl.BlockSpec((None, tr, C), lambda j, i, s: (j, i, 0))
    return pl.pallas_call(
        body,
        grid_spec=pltpu.PrefetchScalarGridSpec(
            num_scalar_prefetch=1, grid=(J, nb),
            in_specs=[pl.BlockSpec((None, tr, C), lambda j, i, s: (j, s[0] * nb + i, 0)), spec],
            out_specs=[spec, spec]),
        out_shape=[jax.ShapeDtypeStruct((J, half, C), F32), jax.ShapeDtypeStruct((J, half, C), BF16)],
        compiler_params=_params("parallel", "parallel"), name="add_half")(place_idx[4], g, r)


def _scatter_chips(h):
    J, R, C = h.shape

    def body(h_ref, o_ref, send_sems, recv_sems):
        x, y, c = _place()
        chip = 2 * x + y
        copies = [pltpu.make_async_remote_copy(
            src_ref=h_ref.at[2 * ch[0] + ch[1]], dst_ref=o_ref.at[chip], send_sem=send_sems.at[j],
            recv_sem=recv_sems.at[j], device_id=(*ch, c), device_id_type=MESH)
            for j, ch in enumerate(_other_chips(x, y))]
        for cp in copies:
            cp.start()
        for cp in copies:
            cp.wait_recv()
        for cp in copies:
            cp.wait_send()

    hbm = pl.BlockSpec(memory_space=pl.ANY)
    return pl.pallas_call(
        body, in_specs=[hbm], out_specs=hbm, out_shape=jax.ShapeDtypeStruct((J, R, C), h.dtype),
        scratch_shapes=[pltpu.SemaphoreType.DMA((3,)), pltpu.SemaphoreType.DMA((3,))],
        name="scatter_chips")(h)


def _add_chips(h, p, place_idx):
    J, R, C = h.shape
    tr = _pick(R, max(16, (2 << 20) // (C * 4)), 16)
    nb = R // tr

    def body(s0, s1, s2, s3, s4, h_ref, p0_ref, p1_ref, p2_ref, o_ref):
        o_ref[...] = ((h_ref[...] + p0_ref[...].astype(F32)) + p1_ref[...].astype(F32)) + p2_ref[...].astype(F32)

    slot = lambda k: pl.BlockSpec((None, tr, C), lambda i, *s: (s[k][0], i, 0))
    return pl.pallas_call(
        body,
        grid_spec=pltpu.PrefetchScalarGridSpec(
            num_scalar_prefetch=5, grid=(nb,),
            in_specs=[slot(0), slot(1), slot(2), slot(3)],
            out_specs=pl.BlockSpec((tr, C), lambda i, *s: (s[4][0] * nb + i, 0))),
        out_shape=jax.ShapeDtypeStruct((2 * R, C), F32),
        compiler_params=_params("parallel"), name="add_chips")(*place_idx, h, p, p, p)


def _join_halves(s):
    R, C = s.shape
    half = R // 2

    def body(in_ref, o_ref, send_sem, recv_sem):
        x, y, c = _place()
        mine = o_ref.at[pl.ds(c * half, half)]
        cp = pltpu.make_async_remote_copy(src_ref=mine, dst_ref=mine, send_sem=send_sem, recv_sem=recv_sem,
                                          device_id=(x, y, 1 - c), device_id_type=MESH)
        cp.start()
        cp.wait()

    hbm = pl.BlockSpec(memory_space=pl.ANY)
    return pl.pallas_call(
        body, in_specs=[hbm], out_specs=hbm, out_shape=jax.ShapeDtypeStruct((R, C), s.dtype),
        input_output_aliases={0: 0},
        scratch_shapes=[pltpu.SemaphoreType.DMA, pltpu.SemaphoreType.DMA], name="join_halves")(s)


def _reduce_scatter(g, place_idx):
    h, hb = _add_half(g, _swap_halves(g), place_idx)
    return _join_halves(_add_chips(h, _scatter_chips(hb), place_idx))


def _pack(parts):
    flat = jnp.concatenate([p.reshape(-1).astype(F32) for p in parts])
    n = flat.shape[0]
    rows = -(-n // (8 * LANES)) * 8
    return jnp.pad(flat, (0, rows * LANES - n)).reshape(rows, LANES)


def _unpack(buf, shapes):
    flat = buf.reshape(buf.shape[:-2] + (-1,))
    out, off = [], 0
    for s in shapes:
        n = math.prod(s)
        out.append(flat[..., off:off + n].reshape(buf.shape[:-2] + tuple(s)))
        off += n
    return out


def _mod_table(mod_x, mod_c):
    return jnp.stack([mod_x.reshape(6, 1, -1), mod_c.reshape(6, 1, -1)])


def kernel(x, c, ctx, c_ctx, ada_w, ada_b, ln_g, ln_b, ret_w_qkv, ret_w_g, ret_decay_logit, ret_w_o, mla_w_dq, mla_g_q, mla_w_uq, mla_w_dkv, mla_g_kv, mla_w_ukv, mla_w_o, ffn_w_in, ffn_w_out, loss_target, m_c_ctx, m_ada_w, m_ada_b, m_ln_g, m_ln_b, m_ret_w_qkv, m_ret_w_g, m_ret_decay_logit, m_ret_w_o, m_mla_w_dq, m_mla_g_q, m_mla_w_uq, m_mla_w_dkv, m_mla_g_kv, m_mla_w_ukv, m_mla_w_o, m_ffn_w_in, m_ffn_w_out, v_c_ctx, v_ada_w, v_ada_b, v_ln_g, v_ln_b, v_ret_w_qkv, v_ret_w_g, v_ret_decay_logit, v_ret_w_o, v_mla_w_dq, v_mla_g_q, v_mla_w_uq, v_mla_w_dkv, v_mla_g_kv, v_mla_w_ukv, v_mla_w_o, v_ffn_w_in, v_ffn_w_out):
    Nx, D = x.shape[1], x.shape[2]
    Nc = ctx.shape[1]
    T = Nx + Nc
    tr = 256 if (Nx % 256 == 0 and Nc % 256 == 0) else 128
    n_xt = Nx // tr
    C = RET_CHUNK
    ncx, ncc = Nx // C, Nc // C
    H = RET_HEADS
    dk, dv = D // H, 2 * D // H
    L = MLA_LORA
    HM = MLA_HEADS
    D6 = 6 * D
    n6 = D6 // N_CHIPS
    Dq = D // N_CHIPS
    xi, yi, ci = lax.axis_index("x"), lax.axis_index("y"), lax.axis_index("c")
    chip = 2 * xi + yi
    dev = 4 * xi + 2 * yi + ci
    as_index = lambda s: jnp.reshape(s, (1,)).astype(jnp.int32)
    place_idx = (as_index(chip), as_index(2 * (1 - xi) + yi), as_index(2 * xi + 1 - yi),
                 as_index(2 * (1 - xi) + 1 - yi), as_index(ci))
    chip_idx = place_idx[0]
    tile = dict(T=T, tr=tr, n_xt=n_xt)

    shapes1 = [(D,), (DEPTH, 2, Dq), (DEPTH, 2, Dq), (2, L // N_CHIPS), (2, L // N_CHIPS)]
    g1, _ = _all_gather8(_pack([c[0], ln_g, ln_b, mla_g_q, mla_g_kv]))
    c_all, lng_s, lnb_s, gq_s, gkv_s = _unpack(g1, shapes1)
    by_chip = lambda t: jnp.moveaxis(t[0::2], 0, -2).reshape(t.shape[1:-1] + (-1,))
    ln_g_full, ln_b_full = by_chip(lng_s), by_chip(lnb_s)
    gq_full, gkv_full = by_chip(gq_s), by_chip(gkv_s)

    cond = jnp.concatenate([c_all, c_ctx[None]], 0)
    silu_cond = cond * jax.nn.sigmoid(cond)
    s16 = jnp.pad(silu_cond, ((0, 16 - (N_DEV + 1)), (0, 0))).astype(BF16)
    mods = []
    for i in range(DEPTH):
        bias = lax.dynamic_slice_in_dim(ada_b[i], chip * n6, n6)
        mods.append(_mm_nn(s16, ada_w[i][None], name="ada_fwd")[:N_DEV + 1] + bias[None])
    g2, _ = _all_gather8(_pack([jnp.stack(mods)]))
    (mod_all,) = _unpack(g2, [(DEPTH, N_DEV + 1, n6)])
    mod_all = jnp.moveaxis(mod_all[0::2], 0, -2).reshape(DEPTH, N_DEV + 1, D6)
    mod_tabs = [_mod_table(lax.dynamic_index_in_dim(mod_all[i], dev, 0, False), mod_all[i, N_DEV])
                for i in range(DEPTH)]
    lnps = [[jnp.stack([ln_g_full[i, s], ln_b_full[i, s]])[:, None, :] for s in range(2)] for i in range(DEPTH)]

    def gathered(w_l):
        return _gather_chips(_place_own(w_l, chip_idx))

    def mla_pack(j):
        return jnp.concatenate([mla_w_dq[j], mla_w_uq[j], jnp.pad(mla_w_dkv[j], ((0, 0), (0, 64))),
                                mla_w_ukv[j], mla_w_o[j]], axis=1)

    mla_cols = [L, 3 * L // 2, L + LANES, 2 * L, D]

    def mla_unpack_weights(buf):
        offs = [0]
        for wdt in mla_cols:
            offs.append(offs[-1] + wdt)
        dq_, uq_, dkv_, ukv_, wo_ = [buf[:, :, offs[k]:offs[k + 1]] for k in range(5)]
        w_dq = dq_.reshape(D, L)
        w_dkv = dkv_.reshape(D, L + LANES)
        w_lora = jnp.concatenate([w_dq, w_dkv], axis=1)
        w_uq = jnp.moveaxis(uq_, 0, 1).reshape(L, HM, MLA_D_NOPE + MLA_D_ROPE)
        wq_cat = jnp.pad(w_uq, ((0, 0), (0, 0), (0, 2 * LANES - MLA_D_NOPE - MLA_D_ROPE))).reshape(L, HM * 2 * LANES)
        w_ukv = jnp.moveaxis(ukv_, 0, 1).reshape(L, HM, MLA_D_NOPE + MLA_D_V)
        w_kn = w_ukv[:, :, :MLA_D_NOPE].reshape(L, HM * MLA_D_NOPE)
        w_v = w_ukv[:, :, MLA_D_NOPE:].reshape(L, HM * MLA_D_V)
        w_o = wo_.reshape(HM * MLA_D_V, D)
        return dict(lora=w_lora[None], q=wq_cat[None], kn=w_kn[None], v=w_v[None], o=w_o[None])

    def mla_pack_grads(g):
        d_lora = g["lora"][0]
        d_dq = d_lora[:, :L].reshape(N_CHIPS, Dq, L)
        d_dkv = d_lora[:, L:].reshape(N_CHIPS, Dq, L + LANES)
        d_uq = g["q"][0].reshape(L, HM, 2 * LANES)[:, :, :MLA_D_NOPE + MLA_D_ROPE]
        d_uq = jnp.moveaxis(d_uq.reshape(L, N_CHIPS, -1), 1, 0)
        d_ukv = jnp.concatenate([g["kn"][0].reshape(L, HM, MLA_D_NOPE), g["v"][0].reshape(L, HM, MLA_D_V)], axis=2)
        d_ukv = jnp.moveaxis(d_ukv.reshape(L, N_CHIPS, -1), 1, 0)
        d_o = g["o"][0].reshape(N_CHIPS, L, D)
        return jnp.concatenate([d_dq, d_uq, d_dkv, d_ukv, d_o], axis=2)

    assert Dq == L, "the packed MLA buffer assumes D_MODEL / 4 == 512 rows per shard"

    W = []
    for i in range(DEPTH):
        j = i // 2
        lw = {}
        if i % 2 == 0:
            qkv = gathered(ret_w_qkv[j])
            gg = gathered(ret_w_g[j])
            lw.update(q=qkv[0:1], k=qkv[1:2], v=qkv[2:4], gf=gg[0:2], gb=gg[2:4],
                      o=gathered(ret_w_o[j]).reshape(1, 2 * D, D))
        else:
            lw.update(mla_unpack_weights(gathered(mla_pack(j))))
        w_in = gathered(ffn_w_in[i])
        lw.update(a=w_in[0:2], b=w_in[2:4], out=gathered(ffn_w_out[i]).reshape(1, -1, D))
        W.append(lw)

    ret_tabs = [_ret_tables(ret_decay_logit[j], Nx, Nc, dk, dv) for j in range(2)]
    mla_cos, mla_sin = _mla_tables(Nx, Nc)
    gqkv = [jnp.stack([gq_full[j], gkv_full[j]])[:, None, :] for j in range(2)]
    tq_f, tq_b = tr, 128

    h = jnp.concatenate([x[0], ctx[0]], axis=0)
    u = _modulate(h, mod_tabs[0], **tile)
    saved = []
    for i in range(DEPTH):
        j, lw, mod, sv = i // 2, W[i], mod_tabs[i], {}
        sv.update(h=h, u=u)
        if i % 2 == 0:
            q = _mm_nn(u, lw["q"], name="ret_q")
            k = _mm_nn(u, lw["k"], name="ret_k")
            v = _mm_nn(u, lw["v"], name="ret_v")
            gf = _mm_nn(u, lw["gf"], name="ret_gf")
            gb = _mm_nn(u, lw["gb"], name="ret_gb")
            o, states = _ret_fwd(q, k, v, ret_tabs[j], ncx, ncc)
            yg = _gate_fwd(gf, gb, o, dv=dv, **tile)
            y = _mm_nn(yg, lw["o"], name="ret_o")
            sv.update(q=q, k=k, v=v, gf=gf, gb=gb, o=o, states=states, yg=yg)
        else:
            lora = _mm_nn(u, lw["lora"], name="mla_lora")
            cqn, ckvn, kr = _lora_fwd(lora, gqkv[j], mla_cos, mla_sin, **tile)
            qcat = _mm_nn(cqn, lw["q"], name="mla_q")
            qrot = _qrope(qcat, mla_cos, mla_sin, False, **tile)
            kn = _mm_nn(ckvn, lw["kn"], out_dtype=BF16, name="mla_kn")
            vv = _mm_nn(ckvn, lw["v"], out_dtype=BF16, name="mla_v")
            att, lse = _attn_fwd(qrot, kn, vv, kr, Nx, tq_f)
            y = _mm_nn(att, lw["o"], name="mla_o")
            sv.update(lora=lora, cqn=cqn, ckvn=ckvn, kr=kr, qrot=qrot, kn=kn, vv=vv, att=att, lse=lse)
        h1, u2 = _ln_fwd(h, y, mod, lnps[i][0], 2, mod, (3, 4), **tile)
        a = _mm_nn(u2, lw["a"], name="ffn_a")
        b = _mm_nn(u2, lw["b"], name="ffn_b")
        act = _swiglu_fwd(a, b, **tile)
        f = _mm_nn(act, lw["out"], name="ffn_out")
        last = i == DEPTH - 1
        h2, u_next = _ln_fwd(h1, f, mod, lnps[i][1], 5, None if last else mod_tabs[i + 1], (0, 1), **tile)
        sv.update(y=y, h1=h1, u2=u2, a=a, b=b, act=act, f=f)
        saved.append(sv)
        h, u = h2, u_next

    dh, err_cols = _loss_grad(h, loss_target[0], **tile)
    loss = lax.psum(0.5 * jnp.sum(err_cols) / D, ("x", "y", "c"))

    g_big = {}
    d_mods, d_lng, d_lnb = [None] * DEPTH, [None] * DEPTH, [None] * DEPTH
    d_gq, d_gkv, d_lam = [None] * 2, [None] * 2, [None] * 2
    for i in reversed(range(DEPTH)):
        j, lw, mod, sv = i // 2, W[i], mod_tabs[i], saved[i]
        dh1_res, df, dg_f, dlg1, dlb1 = _ln_bwd(sv["h1"], sv["f"], dh, mod, lnps[i][1], 5, **tile)
        dact = _mm_nt(df, lw["out"], name="ffn_out_nt")
        g_out = _mm_tn(sv["act"], df, 1, name="ffn_out_tn").reshape(N_CHIPS, -1, D)
        da, db = _swiglu_bwd(sv["a"], sv["b"], dact, **tile)
        du2 = _mm_nt(db, lw["b"], add=_mm_nt(da, lw["a"], name="ffn_a_nt"), name="ffn_b_nt")
        g_in = jnp.concatenate([_mm_tn(sv["u2"], da, 2, name="ffn_a_tn"), _mm_tn(sv["u2"], db, 2, name="ffn_b_tn")], 0)
        dh1, dsc_f, dsh_f = _mod_bwd(dh1_res, du2, sv["h1"], mod, 4, **tile)
        dh_res, dy, dg_a, dlg0, dlb0 = _ln_bwd(sv["h"], sv["y"], dh1, mod, lnps[i][0], 2, **tile)
        uu = sv["u"]
        if i % 2 == 0:
            dyg = _mm_nt(dy, lw["o"], name="ret_o_nt")
            g_o = _mm_tn(sv["yg"], dy, 1, name="ret_o_tn").reshape(N_CHIPS, -1, D)
            dgf, dgb, do_f, do_b = _gate_bwd(sv["gf"], sv["gb"], sv["o"], dyg, dv=dv, **tile)
            dq2, dk2, dv2, dlam = _ret_bwd(sv["q"], sv["k"], sv["v"], jnp.stack([do_f, do_b]), sv["states"],
                                           ret_tabs[j], ncx, ncc)
            dq, dkk, dvv = _add_dirs(dq2, dk2, dv2, **tile)
            du = _mm_nt(dq, lw["q"], name="ret_q_nt")
            du = _mm_nt(dkk, lw["k"], add=du, name="ret_k_nt")
            du = _mm_nt(dvv, lw["v"], add=du, name="ret_v_nt")
            du = _mm_nt(dgf, lw["gf"], add=du, name="ret_gf_nt")
            du = _mm_nt(dgb, lw["gb"], add=du, name="ret_gb_nt")
            g_qkv = jnp.concatenate([_mm_tn(uu, dq, 1, name="ret_q_tn"), _mm_tn(uu, dkk, 1, name="ret_k_tn"),
                                     _mm_tn(uu, dvv, 2, name="ret_v_tn")], 0)
            g_g = jnp.concatenate([_mm_tn(uu, dgf, 2, name="ret_gf_tn"), _mm_tn(uu, dgb, 2, name="ret_gb_tn")], 0)
            g_big.setdefault("ret_w_qkv", {})[j] = g_qkv
            g_big.setdefault("ret_w_g", {})[j] = g_g
            g_big.setdefault("ret_w_o", {})[j] = g_o
            d_lam[j] = dlam[:, :, 0, 0]
        else:
            datt = _mm_nt(dy, lw["o"], name="mla_o_nt").astype(BF16)
            gm = dict(o=_mm_tn(sv["att"], dy, 1, name="mla_o_tn"))
            dqcat, dkn, dvv, dkr = _attn_bwd(sv["qrot"], sv["kn"], sv["vv"], sv["kr"], datt, sv["lse"], Nx, tq_b)
            dqraw = _qrope(dqcat, mla_cos, mla_sin, True, **tile)
            dcqn = _mm_nt(dqraw, lw["q"], name="mla_q_nt")
            gm["q"] = _mm_tn(sv["cqn"], dqraw, 1, name="mla_q_tn")
            dckvn = _mm_nt(dvv, lw["v"], add=_mm_nt(dkn, lw["kn"], name="mla_kn_nt"), name="mla_v_nt")
            gm["kn"] = _mm_tn(sv["ckvn"], dkn, 1, name="mla_kn_tn")
            gm["v"] = _mm_tn(sv["ckvn"], dvv, 1, name="mla_v_tn")
            dlora, dgq, dgkv = _lora_bwd(sv["lora"], dcqn, dckvn, dkr, gqkv[j], mla_cos, mla_sin, **tile)
            du = _mm_nt(dlora, lw["lora"], name="mla_lora_nt")
            gm["lora"] = _mm_tn(uu, dlora, 1, name="mla_lora_tn")
            g_big.setdefault("mla", {})[j] = mla_pack_grads(gm)
            d_gq[j], d_gkv[j] = dgq[0], dgkv[0]
        dh, dsc_a, dsh_a = _mod_bwd(dh_res, du, sv["h"], mod, 1, **tile)
        g_big.setdefault("ffn_w_in", {})[i] = g_in
        g_big.setdefault("ffn_w_out", {})[i] = g_out
        d_mods[i] = jnp.concatenate([dsh_a, dsc_a, dg_a, dsh_f, dsc_f, dg_f], axis=2)[:, 0, :]
        d_lng[i] = jnp.concatenate([dlg0, dlg1], 0)
        d_lnb[i] = jnp.concatenate([dlb0, dlb1], 0)

    grad_x = dh[:Nx][None]

    d_mods = jnp.stack(d_mods)
    dlogit = jnp.stack([d_lam[j] * jax.nn.sigmoid(-ret_decay_logit[j]) for j in range(2)])
    shapes3 = [(DEPTH, D6), (DEPTH, D6), (DEPTH, 2, D), (DEPTH, 2, D), (2, L), (2, L), (2, 2, H)]
    g3, s3 = _all_gather8(_pack([d_mods[:, 0], d_mods[:, 1], jnp.stack(d_lng), jnp.stack(d_lnb),
                                 jnp.stack(d_gq), jnp.stack(d_gkv), dlogit]))
    dmod_x_all = _unpack(g3, shapes3)[0]
    dmod_x_sum, dmod_c_sum, g_lng, g_lnb, g_gq, g_gkv, g_decay = _unpack(s3, shapes3)
    grad_ada_b = dmod_x_sum + dmod_c_sum
    dmod9 = jnp.concatenate([jnp.moveaxis(dmod_x_all, 0, 1), dmod_c_sum[:, None]], axis=1)
    dmod16 = jnp.pad(lax.dynamic_slice_in_dim(dmod9, chip * n6, n6, axis=2), ((0, 0), (0, 16 - (N_DEV + 1)), (0, 0)))
    dmod16 = dmod16.astype(BF16)
    grad_ada_w = jnp.stack([_mm_tn(s16, dmod16[i], 1, name="ada_tn")[0] for i in range(DEPTH)])
    dsilu = _mm_nt(jnp.moveaxis(dmod16, 0, 1).reshape(16, DEPTH * n6), ada_w, name="ada_nt")
    _, s4 = _all_gather8(_pack([dsilu[N_DEV]]))
    sg = jax.nn.sigmoid(c_ctx)
    grad_c_ctx = (0.5 * s4.reshape(-1)[:D]) * (sg * (1.0 + c_ctx * (1.0 - sg)))

    my_cols = lambda t, n: lax.dynamic_slice_in_dim(t, chip * n, n, axis=t.ndim - 1)
    grad_ln_g, grad_ln_b = my_cols(g_lng, Dq), my_cols(g_lnb, Dq)
    grad_gq, grad_gkv = my_cols(g_gq, L // N_CHIPS), my_cols(g_gkv, L // N_CHIPS)

    def rs(name, n_layers, shard_shape):
        return jnp.stack([_reduce_scatter(g_big[name][l].reshape(N_CHIPS, -1, shard_shape[-1]), place_idx)
                          .reshape(shard_shape) for l in range(n_layers)])

    grad_ret_w_qkv = rs("ret_w_qkv", 2, ret_w_qkv.shape[1:])
    grad_ret_w_g = rs("ret_w_g", 2, ret_w_g.shape[1:])
    grad_ret_w_o = rs("ret_w_o", 2, ret_w_o.shape[1:])
    grad_ffn_w_in = rs("ffn_w_in", DEPTH, ffn_w_in.shape[1:])
    grad_ffn_w_out = rs("ffn_w_out", DEPTH, ffn_w_out.shape[1:])
    mla_red = [_reduce_scatter(g_big["mla"][l], place_idx) for l in range(2)]
    offs = [0]
    for wdt in mla_cols:
        offs.append(offs[-1] + wdt)
    mla_parts = [jnp.stack([mla_red[l][:, offs[k]:offs[k + 1]] for l in range(2)]) for k in range(5)]
    grad_mla_w_dq, grad_mla_w_uq, grad_mla_w_ukv, grad_mla_w_o = mla_parts[0], mla_parts[1], mla_parts[3], mla_parts[4]
    grad_mla_w_dkv = mla_parts[2][:, :, :L + MLA_D_ROPE]

    grads = [grad_c_ctx, grad_ada_w, grad_ada_b, grad_ln_g, grad_ln_b, grad_ret_w_qkv, grad_ret_w_g, g_decay,
             grad_ret_w_o, grad_mla_w_dq, grad_gq, grad_mla_w_uq, grad_mla_w_dkv, grad_gkv, grad_mla_w_ukv,
             grad_mla_w_o, grad_ffn_w_in, grad_ffn_w_out]
    weights = [c_ctx, ada_w, ada_b, ln_g, ln_b, ret_w_qkv, ret_w_g, ret_decay_logit, ret_w_o, mla_w_dq, mla_g_q,
               mla_w_uq, mla_w_dkv, mla_g_kv, mla_w_ukv, mla_w_o, ffn_w_in, ffn_w_out]
    ms = [m_c_ctx, m_ada_w, m_ada_b, m_ln_g, m_ln_b, m_ret_w_qkv, m_ret_w_g, m_ret_decay_logit, m_ret_w_o,
          m_mla_w_dq, m_mla_g_q, m_mla_w_uq, m_mla_w_dkv, m_mla_g_kv, m_mla_w_ukv, m_mla_w_o, m_ffn_w_in, m_ffn_w_out]
    vs = [v_c_ctx, v_ada_w, v_ada_b, v_ln_g, v_ln_b, v_ret_w_qkv, v_ret_w_g, v_ret_decay_logit, v_ret_w_o,
          v_mla_w_dq, v_mla_g_q, v_mla_w_uq, v_mla_w_dkv, v_mla_g_kv, v_mla_w_ukv, v_mla_w_o, v_ffn_w_in, v_ffn_w_out]
    upd = [_adamw(w_, g_, m_, v_) for w_, g_, m_, v_ in zip(weights, grads, ms, vs)]
    return (loss, grad_x, *grads, *[u_[0] for u_ in upd], *[u_[1] for u_ in upd], *[u_[2] for u_ in upd])
```

```python
import functools
import math

import jax
import jax.numpy as jnp
from jax import lax
from jax.experimental import pallas as pl
from jax.experimental.pallas import tpu as pltpu

F32, BF16 = jnp.float32, jnp.bfloat16
MESH = pl.DeviceIdType.MESH
V7X_VMEM_LIMIT_BYTES = 56 * 1024 * 1024
LANES = 128

GRID_W = 64
RET_HEADS = 8
RET_CHUNK = 128
RET_ROPE_BASE = 10000.0
GN_EPS = 1e-6
MLA_HEADS = 16
MLA_LORA = 512
MLA_D_NOPE = 128
MLA_D_ROPE = 64
MLA_D_V = 128
MLA_SCALE = (MLA_D_NOPE + MLA_D_ROPE) ** -0.5
AXIAL_ROPE_BASE = 10000.0
RMS_EPS = 1e-6
DEPTH = 4
DEEPNORM_ALPHA = (2 * DEPTH) ** 0.25
LN_EPS = 1e-5
ADAM_LR, ADAM_B1, ADAM_B2, ADAM_EPS, ADAM_WD, ADAM_STEP = 0.001, 0.9, 0.999, 1e-08, 0.01, 10
N_CHIPS = 4
N_DEV = 8


def _pick(dim, target, mult):
    best = None
    for d in range(mult, min(dim, target) + 1, mult):
        if dim % d == 0:
            best = d
    return dim if best is None else best


def _params(*sem):
    return pltpu.CompilerParams(dimension_semantics=sem, vmem_limit_bytes=V7X_VMEM_LIMIT_BYTES)


def _sigmoid(x):
    return 1.0 / (1.0 + jnp.exp(-x))


def _accumulate(step, nsteps, acc, part, write):
    if nsteps == 1:
        write(part)
        return

    @pl.when(step == 0)
    def _():
        acc[...] = part

    @pl.when(jnp.logical_and(step > 0, step < nsteps - 1))
    def _():
        acc[...] += part

    @pl.when(step == nsteps - 1)
    def _():
        write(acc[...] + part)


def _mm_nn(a, w, out_dtype=F32, name="mm_nn"):
    M, K = a.shape
    J, K2, n = w.shape
    assert K == K2
    tn = _pick(n, 1408, LANES)
    tk = _pick(K, 2048, LANES)
    tm = _pick(M, max(16, (6 << 20) // (tn * 4)), 16)
    npj, nk = n // tn, K // tk

    def body(a_ref, w_ref, o_ref, acc):
        part = jnp.dot(a_ref[...].astype(BF16), w_ref[...].astype(BF16), preferred_element_type=F32)

        def write(total):
            o_ref[...] = total.astype(o_ref.dtype)

        _accumulate(pl.program_id(2), nk, acc, part, write)

    return pl.pallas_call(
        body, grid=(M // tm, J * npj, nk),
        in_specs=[pl.BlockSpec((tm, tk), lambda i, j, k: (i, k)),
                  pl.BlockSpec((None, tk, tn), lambda i, j, k: (j // npj, k, j % npj))],
        out_specs=pl.BlockSpec((tm, tn), lambda i, j, k: (i, j)),
        out_shape=jax.ShapeDtypeStruct((M, J * n), out_dtype),
        scratch_shapes=[pltpu.VMEM((tm, tn), F32)],
        compiler_params=_params("parallel", "parallel", "arbitrary"), name=name)(a, w)


def _mm_nt(dy, w, add=None, name="mm_nt"):
    M, N = dy.shape
    J, K, n = w.shape
    assert N == J * n
    tko = _pick(K, 512, LANES)
    tc = _pick(n, 2048, LANES)
    tm = _pick(M, max(16, (4 << 20) // (tko * 4)), 16)
    npj = n // tc
    nc = J * npj
    has_add = add is not None

    def body(*refs):
        if has_add:
            dy_ref, w_ref, add_ref, o_ref, acc = refs
        else:
            dy_ref, w_ref, o_ref, acc = refs
        part = lax.dot_general(dy_ref[...].astype(BF16), w_ref[...].astype(BF16),
                               (((1,), (1,)), ((), ())), preferred_element_type=F32)

        def write(total):
            o_ref[...] = total + add_ref[...] if has_add else total

        _accumulate(pl.program_id(2), nc, acc, part, write)

    in_specs = [pl.BlockSpec((tm, tc), lambda i, ko, c: (i, c)),
                pl.BlockSpec((None, tko, tc), lambda i, ko, c: (c // npj, ko, c % npj))]
    args = [dy, w]
    if has_add:
        in_specs.append(pl.BlockSpec((tm, tko), lambda i, ko, c: (i, ko)))
        args.append(add)
    return pl.pallas_call(
        body, grid=(M // tm, K // tko, nc), in_specs=in_specs,
        out_specs=pl.BlockSpec((tm, tko), lambda i, ko, c: (i, ko)),
        out_shape=jax.ShapeDtypeStruct((M, K), F32),
        scratch_shapes=[pltpu.VMEM((tm, tko), F32)],
        compiler_params=_params("parallel", "parallel", "arbitrary"), name=name)(*args)


def _mm_tn(a, dy, J, name="mm_tn"):
    M, K = a.shape
    M2, N = dy.shape
    assert M == M2 and N % J == 0
    n = N // J
    tko = _pick(K, 1024, LANES)
    tn = _pick(n, 1408, LANES)
    tmc = _pick(M, 2176, 16)
    npj, nm = n // tn, M // tmc

    def body(a_ref, dy_ref, o_ref, acc):
        part = lax.dot_general(a_ref[...].astype(BF16), dy_ref[...].astype(BF16),
                               (((0,), (0,)), ((), ())), preferred_element_type=F32)

        def write(total):
            o_ref[...] = total

        _accumulate(pl.program_id(2), nm, acc, part, write)

    return pl.pallas_call(
        body, grid=(K // tko, J * npj, nm),
        in_specs=[pl.BlockSpec((tmc, tko), lambda ko, j, m: (m, ko)),
                  pl.BlockSpec((tmc, tn), lambda ko, j, m: (m, j))],
        out_specs=pl.BlockSpec((None, tko, tn), lambda ko, j, m: (j // npj, ko, j % npj)),
        out_shape=jax.ShapeDtypeStruct((J, K, n), F32),
        scratch_shapes=[pltpu.VMEM((tko, tn), F32)],
        compiler_params=_params("parallel", "parallel", "arbitrary"), name=name)(a, dy)


def _rowwise(name, fn, T, tr, n_xt, row_in, grp_in=(), const_in=(), row_out=(), gsum_w=(), tsum_w=(), ncol=1):
    nt = T // tr
    n_in = len(row_in) + len(grp_in) + len(const_in)
    n_ro, n_gs = len(row_out), len(gsum_w)
    assert ncol == 1 or not (gsum_w or tsum_w)

    def body(*refs):
        t = pl.program_id(0)
        vals = [r[...] for r in refs[:n_in]]
        rv = vals[:len(row_in)]
        gv = vals[len(row_in):len(row_in) + len(grp_in)]
        cv = vals[len(row_in) + len(grp_in):]
        ro, gs, ts = fn(rv, gv, cv)
        outs = refs[n_in:]
        for ref, val in zip(outs[:n_ro], ro):
            ref[...] = val.astype(ref.dtype)
        first_g = jnp.logical_or(t == 0, t == n_xt)
        for ref, val, first in ([(r, v, first_g) for r, v in zip(outs[n_ro:n_ro + n_gs], gs)]
                                + [(r, v, t == 0) for r, v in zip(outs[n_ro + n_gs:], ts)]):
            s = jnp.sum(val, axis=0, keepdims=True)

            @pl.when(first)
            def _(ref=ref, s=s):
                ref[...] = s

            @pl.when(jnp.logical_not(first))
            def _(ref=ref, s=s):
                ref[...] += s

    in_specs, args = [], []
    for spec in row_in:
        arr, width = spec[:2]
        lead = spec[2] if len(spec) > 2 else None
        step = spec[3] if len(spec) > 3 else 1
        if lead is not None:
            in_specs.append(pl.BlockSpec((None, tr, width), lambda t, cb, lead=lead, step=step: (lead, t, cb * step)))
        else:
            in_specs.append(pl.BlockSpec((tr, width), lambda t, cb, step=step: (t, cb * step)))
        args.append(arr)
    for arr in grp_in:
        in_specs.append(pl.BlockSpec((None,) + arr.shape[1:], lambda t, cb: (jnp.where(t >= n_xt, 1, 0), 0, 0, 0)))
        args.append(arr)
    for arr in const_in:
        in_specs.append(pl.BlockSpec(arr.shape, lambda t, cb: (0, 0, 0)))
        args.append(arr)
    out_specs, out_shape = [], []
    for wtot, wblk, dt in row_out:
        out_specs.append(pl.BlockSpec((tr, wblk), lambda t, cb: (t, cb)))
        out_shape.append(jax.ShapeDtypeStruct((T, wtot), dt))
    for w in gsum_w:
        out_specs.append(pl.BlockSpec((None, 1, w), lambda t, cb: (jnp.where(t >= n_xt, 1, 0), 0, 0)))
        out_shape.append(jax.ShapeDtypeStruct((2, 1, w), F32))
    for w in tsum_w:
        out_specs.append(pl.BlockSpec((1, w), lambda t, cb: (0, 0)))
        out_shape.append(jax.ShapeDtypeStruct((1, w), F32))
    sem = ("arbitrary", "arbitrary") if (gsum_w or tsum_w) else ("parallel", "parallel")
    return pl.pallas_call(body, grid=(nt, ncol), in_specs=in_specs, out_specs=out_specs, out_shape=out_shape,
                          compiler_params=_params(*sem), name=name)(*args)


def _modulate(h, mod, T, tr, n_xt):
    D = h.shape[1]

    def fn(rv, gv, cv):
        m = gv[0]
        return [rv[0] * (1.0 + m[1]) + m[0]], [], []

    return _rowwise("modulate", fn, T, tr, n_xt, [(h, D)], [mod], row_out=[(D, D, BF16)])[0]


def _ln_stats(r):
    mu = jnp.mean(r, axis=-1, keepdims=True)
    xc = r - mu
    var = jnp.mean(xc * xc, axis=-1, keepdims=True)
    rstd = lax.rsqrt(var + LN_EPS)
    return xc * rstd, rstd


def _ln_fwd(h, y, mod, lnp, gate_row, mod_next, next_rows, T, tr, n_xt):
    D = h.shape[1]
    with_u = mod_next is not None

    def fn(rv, gv, cv):
        r = DEEPNORM_ALPHA * rv[0] + gv[0][gate_row] * rv[1]
        xhat, _ = _ln_stats(r)
        out = xhat * cv[0][0] + cv[0][1]
        if not with_u:
            return [out], [], []
        mn = gv[1]
        return [out, out * (1.0 + mn[next_rows[1]]) + mn[next_rows[0]]], [], []

    grp = [mod, mod_next] if with_u else [mod]
    outs = [(D, D, F32), (D, D, BF16)] if with_u else [(D, D, F32)]
    res = _rowwise("ln_fwd", fn, T, tr, n_xt, [(h, D), (y, D)], grp, [lnp], row_out=outs)
    return (res[0], res[1]) if with_u else (res[0], None)


def _ln_bwd(h, y, dout, mod, lnp, gate_row, T, tr, n_xt):
    D = h.shape[1]

    def fn(rv, gv, cv):
        g = gv[0][gate_row]
        r = DEEPNORM_ALPHA * rv[0] + g * rv[1]
        xhat, rstd = _ln_stats(r)
        dxh = rv[2] * cv[0][0]
        m1 = jnp.mean(dxh, axis=-1, keepdims=True)
        m2 = jnp.mean(dxh * xhat, axis=-1, keepdims=True)
        dr = rstd * (dxh - m1 - xhat * m2)
        return [DEEPNORM_ALPHA * dr, g * dr], [dr * rv[1]], [rv[2] * xhat, rv[2]]

    return _rowwise("ln_bwd", fn, T, tr, n_xt, [(h, D), (y, D), (dout, D)], [mod], [lnp],
                    row_out=[(D, D, F32), (D, D, BF16)], gsum_w=[D], tsum_w=[D, D])


def _mod_bwd(dh_res, du, h, mod, scale_row, T, tr, n_xt):
    D = h.shape[1]

    def fn(rv, gv, cv):
        return [rv[0] + rv[1] * (1.0 + gv[0][scale_row])], [rv[1] * rv[2], rv[1]], []

    return _rowwise("mod_bwd", fn, T, tr, n_xt, [(dh_res, D), (du, D), (h, D)], [mod],
                    row_out=[(D, D, F32)], gsum_w=[D, D])


def _swiglu_fwd(a, b, T, tr, n_xt):
    F = a.shape[1]
    wc = _pick(F, 1408, LANES)

    def fn(rv, gv, cv):
        return [rv[0] * _sigmoid(rv[0]) * rv[1]], [], []

    return _rowwise("swiglu_fwd", fn, T, tr, n_xt, [(a, wc), (b, wc)], row_out=[(F, wc, BF16)], ncol=F // wc)[0]


def _swiglu_bwd(a, b, dact, T, tr, n_xt):
    F = a.shape[1]
    wc = _pick(F, 1408, LANES)

    def fn(rv, gv, cv):
        av, bv, dv = rv
        sg = _sigmoid(av)
        return [dv * bv * (sg * (1.0 + av * (1.0 - sg))), dv * av * sg], [], []

    return _rowwise("swiglu_bwd", fn, T, tr, n_xt, [(a, wc), (b, wc), (dact, wc)],
                    row_out=[(F, wc, BF16), (F, wc, BF16)], ncol=F // wc)


def _gn(o):
    mu = jnp.mean(o, axis=-1, keepdims=True)
    xc = o - mu
    var = jnp.mean(xc * xc, axis=-1, keepdims=True)
    rstd = lax.rsqrt(var + GN_EPS)
    return xc * rstd, rstd


def _gate_fwd(gf, gb, o, T, tr, n_xt, dv):
    W = gf.shape[1]

    def fn(rv, gv, cv):
        xf, _ = _gn(rv[2])
        xb, _ = _gn(rv[3])
        return [rv[0] * _sigmoid(rv[0]) * xf + rv[1] * _sigmoid(rv[1]) * xb], [], []

    return _rowwise("gate_fwd", fn, T, tr, n_xt, [(gf, dv), (gb, dv), (o, dv, 0), (o, dv, 1)],
                    row_out=[(W, dv, BF16)], ncol=W // dv)[0]


def _gate_bwd(gf, gb, o, dy, T, tr, n_xt, dv):
    W = gf.shape[1]

    def fn(rv, gv, cv):
        outs_g, outs_o = [], []
        for g, ov in ((rv[0], rv[2]), (rv[1], rv[3])):
            xh, rstd = _gn(ov)
            sg = _sigmoid(g)
            outs_g.append(rv[4] * xh * (sg * (1.0 + g * (1.0 - sg))))
            dxh = rv[4] * g * sg
            m1 = jnp.mean(dxh, axis=-1, keepdims=True)
            m2 = jnp.mean(dxh * xh, axis=-1, keepdims=True)
            outs_o.append(rstd * (dxh - m1 - xh * m2))
        return outs_g + outs_o, [], []

    return _rowwise("gate_bwd", fn, T, tr, n_xt,
                    [(gf, dv), (gb, dv), (o, dv, 0), (o, dv, 1), (dy, dv)],
                    row_out=[(W, dv, BF16)] * 4, ncol=W // dv)


def _add_dirs(dq, dk, dv_, T, tr, n_xt):
    ws = [dq.shape[2], dk.shape[2], dv_.shape[2]]

    def fn(rv, gv, cv):
        return [rv[0] + rv[1], rv[2] + rv[3], rv[4] + rv[5]], [], []

    row_in = []
    for arr, w in zip((dq, dk, dv_), ws):
        row_in += [(arr, w, 0), (arr, w, 1)]
    return _rowwise("add_dirs", fn, T, tr, n_xt, row_in, row_out=[(w, w, BF16) for w in ws])


def _swap16(x):
    lane = lax.broadcasted_iota(jnp.int32, x.shape, x.ndim - 1)
    return jnp.where(lane % 32 < 16, pltpu.roll(x, LANES - 16, x.ndim - 1), pltpu.roll(x, 16, x.ndim - 1))


def _rope2d(x, cos, sin, transpose):
    if transpose:
        return x * cos + _swap16(x * sin)
    return x * cos + _swap16(x) * sin


def _rms(x, g):
    rstd = lax.rsqrt(jnp.mean(x * x, axis=-1, keepdims=True) + RMS_EPS)
    return x * rstd, rstd


def _lora_fwd(lora, gq_gkv, cos, sin, T, tr, n_xt):
    L = MLA_LORA

    def fn(rv, gv, cv):
        x = rv[0]
        xq, _ = _rms(x[:, :L], None)
        xkv, _ = _rms(x[:, L:2 * L], None)
        return [xq * cv[0][0], xkv * cv[0][1], _rope2d(x[:, 2 * L:], rv[1], rv[2], False)], [], []

    return _rowwise("lora_fwd", fn, T, tr, n_xt, [(lora, 2 * L + LANES), (cos, LANES), (sin, LANES)],
                    const_in=[gq_gkv], row_out=[(L, L, BF16), (L, L, BF16), (LANES, LANES, BF16)])


def _lora_bwd(lora, dq, dkv, dkr, gq_gkv, cos, sin, T, tr, n_xt):
    L = MLA_LORA

    def fn(rv, gv, cv):
        x = rv[0]
        outs, sums = [], []
        for xs, dy, g in ((x[:, :L], rv[1], cv[0][0]), (x[:, L:2 * L], rv[2], cv[0][1])):
            xh, rstd = _rms(xs, None)
            dxh = dy * g
            outs.append(rstd * (dxh - xh * jnp.mean(dxh * xh, axis=-1, keepdims=True)))
            sums.append(dy * xh)
        outs.append(_rope2d(rv[3], rv[4], rv[5], True))
        return [jnp.concatenate(outs, axis=1)], [], sums

    W = 2 * L + LANES
    return _rowwise("lora_bwd", fn, T, tr, n_xt,
                    [(lora, W), (dq, L), (dkv, L), (dkr, LANES), (cos, LANES), (sin, LANES)],
                    const_in=[gq_gkv], row_out=[(W, W, BF16)], tsum_w=[L, L])


def _qrope(q, cos, sin, transpose, T, tr, n_xt):
    W = q.shape[1]

    def fn(rv, gv, cv):
        x = rv[0]
        return [jnp.concatenate([x[:, :LANES], _rope2d(x[:, LANES:], rv[1], rv[2], transpose)], axis=1)], [], []

    return _rowwise("qrope_bwd" if transpose else "qrope_fwd", fn, T, tr, n_xt,
                    [(q, 2 * LANES), (cos, LANES, None, 0), (sin, LANES, None, 0)],
                    row_out=[(W, 2 * LANES, BF16)], ncol=W // (2 * LANES))[0]


def _loss_grad(h, target, T, tr, n_xt):
    D = h.shape[1]
    nt = T // tr

    def body(h_ref, t_ref, dh_ref, s_ref):
        t = pl.program_id(0)
        diff = jnp.where(t < n_xt, h_ref[...] - t_ref[...], 0.0)
        dh_ref[...] = diff * (1.0 / D)
        s = jnp.sum(diff * diff, axis=0, keepdims=True)

        @pl.when(t == 0)
        def _():
            s_ref[...] = s

        @pl.when(t != 0)
        def _():
            s_ref[...] += s

    return pl.pallas_call(
        body, grid=(nt,),
        in_specs=[pl.BlockSpec((tr, D), lambda t: (t, 0)),
                  pl.BlockSpec((tr, D), lambda t: (jnp.minimum(t, n_xt - 1), 0))],
        out_specs=[pl.BlockSpec((tr, D), lambda t: (t, 0)), pl.BlockSpec((1, D), lambda t: (0, 0))],
        out_shape=[jax.ShapeDtypeStruct((T, D), F32), jax.ShapeDtypeStruct((1, D), F32)],
        compiler_params=_params("arbitrary"), name="loss_grad")(h, target)


def _adamw(w, g, m, v):
    shape = w.shape
    C = shape[-1] if w.ndim > 1 else shape[0]
    R = w.size // C
    tr = _pick(R, max(8, (2 << 20) // (C * 4)), 8)
    c1 = 1.0 - ADAM_B1 ** ADAM_STEP
    c2 = 1.0 - ADAM_B2 ** ADAM_STEP

    def body(w_ref, g_ref, m_ref, v_ref, d_ref, nm_ref, nv_ref):
        gv = g_ref[...]
        nm = ADAM_B1 * m_ref[...] + (1.0 - ADAM_B1) * gv
        nv = ADAM_B2 * v_ref[...] + (1.0 - ADAM_B2) * (gv * gv)
        nm_ref[...] = nm
        nv_ref[...] = nv
        d_ref[...] = -ADAM_LR * ((nm / c1) / (jnp.sqrt(nv / c2) + ADAM_EPS) + ADAM_WD * w_ref[...])

    spec = pl.BlockSpec((tr, C), lambda i: (i, 0))
    outs = pl.pallas_call(
        body, grid=(R // tr,), in_specs=[spec] * 4, out_specs=[spec] * 3,
        out_shape=[jax.ShapeDtypeStruct((R, C), F32)] * 3,
        compiler_params=_params("parallel"), name="adamw")(*[t.reshape(R, C) for t in (w, g, m, v)])
    return tuple(o.reshape(shape) for o in outs)


def _ret_chunk(d, s, ncx, ncc):
    fwd = jnp.where(s < ncc, ncx + s, s - ncc)
    bwd = jnp.where(s < ncc, ncx + ncc - 1 - s, ncx - 1 - (s - ncc))
    return jnp.where(d == 0, fwd, bwd)


def _rot_half(x, cos, sin, transpose):
    half = x.shape[-1] // 2
    if transpose:
        return x * cos + pltpu.roll(x * sin, half, x.ndim - 1)
    return x * cos + pltpu.roll(x, half, x.ndim - 1) * sin


def _dot_nt(a, b):
    return lax.dot_general(a, b, (((1,), (1,)), ((), ())), preferred_element_type=F32)


def _dot_tn(a, b):
    return lax.dot_general(a, b, (((0,), (0,)), ((), ())), preferred_element_type=F32)


def _dot(a, b):
    return jnp.dot(a, b, preferred_element_type=F32)


def _ret_fwd(q, k, v, tabs, ncx, ncc):
    T, C, H = q.shape[0], RET_CHUNK, RET_HEADS
    dk, dv = q.shape[1] // H, v.shape[1] // H
    ns = ncx + ncc
    kscale = dk ** -0.5

    def body(q_ref, k_ref, v_ref, cos_ref, sin_ref, intra_ref, qd_ref, kd_ref, cd_ref, o_ref, st_ref, s_scr):
        @pl.when(pl.program_id(2) == 0)
        def _():
            s_scr[...] = jnp.zeros_like(s_scr)

        cos, sin = cos_ref[...], sin_ref[...]
        qv = _rot_half(q_ref[...], cos, sin, False)
        kv = _rot_half(k_ref[...], cos, sin, False) * kscale
        vb = v_ref[...].astype(BF16)
        S = s_scr[...]
        st_ref[...] = S
        scores = _dot_nt(qv.astype(BF16), kv.astype(BF16)) * intra_ref[...]
        o_ref[...] = _dot(scores.astype(BF16), vb) + _dot((qv * qd_ref[...]).astype(BF16), S.astype(BF16))
        s_scr[...] = S * cd_ref[...] + _dot_tn((kv * kd_ref[...]).astype(BF16), vb)

    chunk = lambda d, h, s: _ret_chunk(d, s, ncx, ncc)
    tab = lambda shape: pl.BlockSpec((None, None) + shape, lambda d, h, s: (d, h, 0, 0))
    return pl.pallas_call(
        body, grid=(2, H, ns),
        in_specs=[pl.BlockSpec((C, dk), lambda d, h, s: (chunk(d, h, s), h)),
                  pl.BlockSpec((C, dk), lambda d, h, s: (chunk(d, h, s), h)),
                  pl.BlockSpec((C, dv), lambda d, h, s: (chunk(d, h, s), h)),
                  pl.BlockSpec((C, dk), lambda d, h, s: (chunk(d, h, s), 0)),
                  pl.BlockSpec((C, dk), lambda d, h, s: (chunk(d, h, s), 0)),
                  tab((C, C)), tab((C, dk)), tab((C, dk)), tab((1, dv))],
        out_specs=[pl.BlockSpec((None, C, dv), lambda d, h, s: (d, chunk(d, h, s), h)),
                   pl.BlockSpec((None, None, None, dk, dv), lambda d, h, s: (d, h, s, 0, 0))],
        out_shape=[jax.ShapeDtypeStruct((2, T, H * dv), F32), jax.ShapeDtypeStruct((2, H, ns, dk, dv), F32)],
        scratch_shapes=[pltpu.VMEM((dk, dv), F32)],
        compiler_params=_params("parallel", "parallel", "arbitrary"), name="ret_fwd",
    )(q, k, v, tabs["cos"], tabs["sin"], tabs["intra"], tabs["qd"], tabs["kd"], tabs["cd"])


def _ret_bwd(q, k, v, do, states, tabs, ncx, ncc):
    T, C, H = q.shape[0], RET_CHUNK, RET_HEADS
    dk, dv = q.shape[1] // H, v.shape[1] // H
    ns = ncx + ncc
    kscale = dk ** -0.5

    def body(q_ref, k_ref, v_ref, do_ref, st_ref, cos_ref, sin_ref, intra_ref, qd_ref, kd_ref, cd_ref,
             dm_ref, wq_ref, wk_ref, dq_ref, dk_ref, dv_ref, dl_ref, ds_scr):
        first = pl.program_id(2) == 0

        @pl.when(first)
        def _():
            ds_scr[...] = jnp.zeros_like(ds_scr)

        cos, sin = cos_ref[...], sin_ref[...]
        qv = _rot_half(q_ref[...], cos, sin, False)
        kv = _rot_half(k_ref[...], cos, sin, False) * kscale
        qb, kb = qv.astype(BF16), kv.astype(BF16)
        vb = v_ref[...].astype(BF16)
        dob = do_ref[...]
        intra, qd, kd, cd = intra_ref[...], qd_ref[...], kd_ref[...], cd_ref[...]
        S, dS = st_ref[...], ds_scr[...]
        Sb, dSb = S.astype(BF16), dS.astype(BF16)
        P = _dot_nt(qb, kb) * intra
        dP_raw = _dot_nt(dob, vb)
        dPb = (dP_raw * intra).astype(BF16)
        dq_cross = _dot_nt(dob, Sb) * qd
        dq_rot = _dot(dPb, kb) + dq_cross
        dk_state = _dot_nt(vb, dSb) * kd
        dk_rot = _dot_tn(dPb, qb) + dk_state
        dv_ref[...] = _dot_tn(P.astype(BF16), dob) + _dot((kv * kd).astype(BF16), dSb)
        dq_ref[...] = _rot_half(dq_rot, cos, sin, True)
        dk_ref[...] = _rot_half(dk_rot, cos, sin, True) * kscale
        dlam = (jnp.sum(dm_ref[...] * P * dP_raw) + jnp.sum(wq_ref[...] * qv * dq_cross)
                + C * jnp.sum(cd * S * dS) + jnp.sum(wk_ref[...] * kv * dk_state))
        dl = jnp.full(dl_ref.shape, dlam, F32)

        @pl.when(first)
        def _():
            dl_ref[...] = dl

        @pl.when(jnp.logical_not(first))
        def _():
            dl_ref[...] += dl

        ds_scr[...] = cd * dS + _dot_tn((qv * qd).astype(BF16), dob)

    chunk = lambda d, h, s: _ret_chunk(d, ns - 1 - s, ncx, ncc)
    tab = lambda shape: pl.BlockSpec((None, None) + shape, lambda d, h, s: (d, h, 0, 0))
    dtab = lambda shape: pl.BlockSpec((None,) + shape, lambda d, h, s: (d, 0, 0))
    return pl.pallas_call(
        body, grid=(2, H, ns),
        in_specs=[pl.BlockSpec((C, dk), lambda d, h, s: (chunk(d, h, s), h)),
                  pl.BlockSpec((C, dk), lambda d, h, s: (chunk(d, h, s), h)),
                  pl.BlockSpec((C, dv), lambda d, h, s: (chunk(d, h, s), h)),
                  pl.BlockSpec((None, C, dv), lambda d, h, s: (d, chunk(d, h, s), h)),
                  pl.BlockSpec((None, None, None, dk, dv), lambda d, h, s: (d, h, ns - 1 - s, 0, 0)),
                  pl.BlockSpec((C, dk), lambda d, h, s: (chunk(d, h, s), 0)),
                  pl.BlockSpec((C, dk), lambda d, h, s: (chunk(d, h, s), 0)),
                  tab((C, C)), tab((C, dk)), tab((C, dk)), tab((1, dv)),
                  dtab((C, C)), dtab((C, dk)), dtab((C, dk))],
        out_specs=[pl.BlockSpec((None, C, dk), lambda d, h, s: (d, chunk(d, h, s), h)),
                   pl.BlockSpec((None, C, dk), lambda d, h, s: (d, chunk(d, h, s), h)),
                   pl.BlockSpec((None, C, dv), lambda d, h, s: (d, chunk(d, h, s), h)),
                   pl.BlockSpec((None, None, 1, LANES), lambda d, h, s: (d, h, 0, 0))],
        out_shape=[jax.ShapeDtypeStruct((2, T, H * dk), F32), jax.ShapeDtypeStruct((2, T, H * dk), F32),
                   jax.ShapeDtypeStruct((2, T, H * dv), F32), jax.ShapeDtypeStruct((2, H, 1, LANES), F32)],
        scratch_shapes=[pltpu.VMEM((dk, dv), F32)],
        compiler_params=_params("parallel", "parallel", "arbitrary"), name="ret_bwd",
    )(q, k, v, do, states, tabs["cos"], tabs["sin"], tabs["intra"], tabs["qd"], tabs["kd"], tabs["cd"],
      tabs["dmat"], tabs["wq"], tabs["wk"])


def _ret_tables(decay_logit, Nx, Nc, dk, dv):
    C, H = RET_CHUNK, RET_HEADS
    inv = RET_ROPE_BASE ** (-jnp.linspace(0.0, 1.0, dk // 2, dtype=F32))
    ang = jnp.arange(Nx, dtype=F32)[:, None] * inv[None, :]
    cos, sin = jnp.cos(ang), jnp.sin(ang)
    cosf = jnp.concatenate([jnp.concatenate([cos, cos], 1), jnp.ones((Nc, dk), F32)], 0)
    sinf = jnp.concatenate([jnp.concatenate([-sin, sin], 1), jnp.zeros((Nc, dk), F32)], 0)
    lg = jax.nn.log_sigmoid(decay_logit.astype(F32))
    idx = jnp.arange(C, dtype=F32)
    diff = idx[:, None] - idx[None, :]
    dmat = jnp.stack([jnp.maximum(diff, 0.0), jnp.maximum(-diff, 0.0)])
    mask = jnp.stack([diff >= 0, diff <= 0])
    intra = jnp.where(mask[:, None], jnp.exp(lg[:, :, None, None] * dmat[:, None]), 0.0)
    wq = jnp.stack([idx + 1.0, C - idx])
    wk = jnp.stack([C - 1.0 - idx, idx])
    qd = jnp.exp(lg[:, :, None] * wq[:, None, :])
    kd = jnp.exp(lg[:, :, None] * wk[:, None, :])
    cd = jnp.exp(lg * C)
    bc = lambda t, w: jnp.broadcast_to(t[..., None], t.shape + (w,))
    return dict(cos=cosf, sin=sinf, intra=intra, qd=bc(qd, dk), kd=bc(kd, dk),
                cd=jnp.broadcast_to(cd[:, :, None, None], (2, H, 1, dv)),
                dmat=dmat, wq=bc(wq, dk), wk=bc(wk, dk), lg=lg)


def _attn_fwd(q, kn, v, kr, Nx, tq):
    T, H = q.shape[0], MLA_HEADS
    n_xq = Nx // tq

    def body(q_ref, kn_ref, v_ref, kr_ref, o_ref, lse_ref):
        def attend(lo):
            kcat = jnp.concatenate([kn_ref[lo:, :], kr_ref[lo:, :]], axis=1)
            s = _dot_nt(q_ref[...], kcat) * MLA_SCALE
            m = jnp.max(s, axis=-1, keepdims=True)
            p = jnp.exp(s - m)
            l = jnp.sum(p, axis=-1, keepdims=True)
            o_ref[...] = (_dot(p.astype(BF16), v_ref[lo:, :]) / l).astype(o_ref.dtype)
            lse_ref[...] = m + jnp.log(l)

        @pl.when(pl.program_id(1) < n_xq)
        def _():
            attend(0)

        @pl.when(pl.program_id(1) >= n_xq)
        def _():
            attend(Nx)

    return pl.pallas_call(
        body, grid=(H, T // tq),
        in_specs=[pl.BlockSpec((tq, 2 * LANES), lambda h, i: (i, h)),
                  pl.BlockSpec((T, LANES), lambda h, i: (0, h)),
                  pl.BlockSpec((T, LANES), lambda h, i: (0, h)),
                  pl.BlockSpec((T, LANES), lambda h, i: (0, 0))],
        out_specs=[pl.BlockSpec((tq, LANES), lambda h, i: (i, h)),
                   pl.BlockSpec((None, tq, 1), lambda h, i: (h, i, 0))],
        out_shape=[jax.ShapeDtypeStruct((T, H * LANES), BF16), jax.ShapeDtypeStruct((H, T, 1), F32)],
        compiler_params=_params("parallel", "arbitrary"), name="attn_fwd")(q, kn, v, kr)


def _attn_bwd(q, kn, v, kr, do, lse, Nx, tq):
    T, H = q.shape[0], MLA_HEADS
    n_xq, nq = Nx // tq, T // tq

    def body(q_ref, kn_ref, v_ref, kr_ref, do_ref, lse_ref, dq_ref, dkn_ref, dv_ref, dkr_ref, dk_acc, dv_acc):
        h, i = pl.program_id(0), pl.program_id(1)

        @pl.when(i == 0)
        def _():
            dk_acc[...] = jnp.zeros_like(dk_acc)
            dv_acc[...] = jnp.zeros_like(dv_acc)

        def attend(lo):
            kcat = jnp.concatenate([kn_ref[lo:, :], kr_ref[lo:, :]], axis=1)
            qb, dob = q_ref[...], do_ref[...]
            p = jnp.exp(_dot_nt(qb, kcat) * MLA_SCALE - lse_ref[...])
            dp = _dot_nt(dob, v_ref[lo:, :])
            delta = jnp.sum(p * dp, axis=-1, keepdims=True)
            dsb = (p * (dp - delta) * MLA_SCALE).astype(BF16)
            dq_ref[...] = _dot(dsb, kcat)
            dk_acc[lo:, :] += _dot_tn(dsb, qb)
            dv_acc[lo:, :] += _dot_tn(p.astype(BF16), dob)

        @pl.when(i < n_xq)
        def _():
            attend(0)

        @pl.when(i >= n_xq)
        def _():
            attend(Nx)

        @pl.when(i == nq - 1)
        def _():
            dkn_ref[...] = dk_acc[:, :LANES].astype(dkn_ref.dtype)
            dv_ref[...] = dv_acc[...].astype(dv_ref.dtype)

        @pl.when(jnp.logical_and(i == nq - 1, h == 0))
        def _():
            dkr_ref[...] = dk_acc[:, LANES:]

        @pl.when(jnp.logical_and(i == nq - 1, h != 0))
        def _():
            dkr_ref[...] += dk_acc[:, LANES:]

    return pl.pallas_call(
        body, grid=(H, nq),
        in_specs=[pl.BlockSpec((tq, 2 * LANES), lambda h, i: (i, h)),
                  pl.BlockSpec((T, LANES), lambda h, i: (0, h)),
                  pl.BlockSpec((T, LANES), lambda h, i: (0, h)),
                  pl.BlockSpec((T, LANES), lambda h, i: (0, 0)),
                  pl.BlockSpec((tq, LANES), lambda h, i: (i, h)),
                  pl.BlockSpec((None, tq, 1), lambda h, i: (h, i, 0))],
        out_specs=[pl.BlockSpec((tq, 2 * LANES), lambda h, i: (i, h)),
                   pl.BlockSpec((T, LANES), lambda h, i: (0, h)),
                   pl.BlockSpec((T, LANES), lambda h, i: (0, h)),
                   pl.BlockSpec((T, LANES), lambda h, i: (0, 0))],
        out_shape=[jax.ShapeDtypeStruct((T, H * 2 * LANES), F32), jax.ShapeDtypeStruct((T, H * LANES), BF16),
                   jax.ShapeDtypeStruct((T, H * LANES), BF16), jax.ShapeDtypeStruct((T, LANES), F32)],
        scratch_shapes=[pltpu.VMEM((T, 2 * LANES), F32), pltpu.VMEM((T, LANES), F32)],
        compiler_params=_params("arbitrary", "arbitrary"), name="attn_bwd")(q, kn, v, kr, do, lse)


def _mla_tables(Nx, Nc):
    ad = MLA_D_ROPE // 2
    inv = AXIAL_ROPE_BASE ** (-jnp.arange(ad // 2, dtype=F32) * 2.0 / ad)
    t = jnp.arange(Nx)
    rang = (t // GRID_W).astype(F32)[:, None] * inv[None, :]
    cang = (t % GRID_W).astype(F32)[:, None] * inv[None, :]
    rc, rs, cc, cs = jnp.cos(rang), jnp.sin(rang), jnp.cos(cang), jnp.sin(cang)
    pad1, pad0 = jnp.ones((Nx, LANES - MLA_D_ROPE), F32), jnp.zeros((Nx, LANES - MLA_D_ROPE), F32)
    cos = jnp.concatenate([rc, rc, cc, cc, pad1], 1)
    sin = jnp.concatenate([-rs, rs, -cs, cs, pad0], 1)
    return (jnp.concatenate([cos, jnp.ones((Nc, LANES), F32)], 0),
            jnp.concatenate([sin, jnp.zeros((Nc, LANES), F32)], 0))


def _place():
    x, y, c = lax.axis_index("x"), lax.axis_index("y"), lax.axis_index("c")
    return x, y, c


def _all_gather8(v):
    R = v.shape[0]

    def body(v_ref, g_ref, s_ref, send_sems, recv_sems):
        x, y, c = _place()
        me = 4 * x + 2 * y + c
        g_ref[me] = v_ref[...]
        copies = []
        for k in range(1, N_DEV):
            peer = (x ^ (k >> 2), y ^ ((k >> 1) & 1), c ^ (k & 1))
            copies.append(pltpu.make_async_remote_copy(
                src_ref=v_ref, dst_ref=g_ref.at[me], send_sem=send_sems.at[k - 1], recv_sem=recv_sems.at[k - 1],
                device_id=peer, device_id_type=MESH))
        for cp in copies:
            cp.start()
        for cp in copies:
            cp.wait_recv()
        for cp in copies:
            cp.wait_send()
        acc = g_ref[0]
        for k in range(1, N_DEV):
            acc = acc + g_ref[k]
        s_ref[...] = acc

    vm = pl.BlockSpec(memory_space=pltpu.VMEM)
    return pl.pallas_call(
        body, in_specs=[vm], out_specs=[vm, vm],
        out_shape=[jax.ShapeDtypeStruct((N_DEV, R, LANES), F32), jax.ShapeDtypeStruct((R, LANES), F32)],
        scratch_shapes=[pltpu.SemaphoreType.DMA((N_DEV - 1,)), pltpu.SemaphoreType.DMA((N_DEV - 1,))],
        compiler_params=pltpu.CompilerParams(vmem_limit_bytes=V7X_VMEM_LIMIT_BYTES), name="all_gather8")(v)


def _other_chips(x, y):
    return [(1 - x, y), (x, 1 - y), (1 - x, 1 - y)]


def _place_own(w, chip_idx, after):
    R, C = w.shape
    tr = _pick(R, max(16, (2 << 20) // (C * 4)), 16)

    def body(s_ref, w_ref, after_ref, o_ref):
        o_ref[...] = w_ref[...].astype(o_ref.dtype)

    return pl.pallas_call(
        body,
        grid_spec=pltpu.PrefetchScalarGridSpec(
            num_scalar_prefetch=1, grid=(R // tr,),
            in_specs=[pl.BlockSpec((tr, C), lambda i, s: (i, 0)), pl.BlockSpec(memory_space=pl.ANY)],
            out_specs=pl.BlockSpec((None, tr, C), lambda i, s: (s[0], i, 0))),
        out_shape=jax.ShapeDtypeStruct((N_CHIPS, R, C), BF16),
        compiler_params=_params("parallel"), name="place_own")(chip_idx, w, after)


def _gather_chips(buf):
    J, R, C = buf.shape
    half = R // 2
    assert R % 2 == 0

    def body(in_ref, o_ref, token, send_sems, recv_sems):
        x, y, c = _place()
        chip = 2 * x + y
        mine, sibling = pl.ds(c * half, half), (x, y, 1 - c)
        chips = _other_chips(x, y)
        token[...] = jnp.zeros_like(token)

        def copy(k, part, to):
            return pltpu.make_async_remote_copy(src_ref=part, dst_ref=part, send_sem=send_sems.at[k],
                                                recv_sem=recv_sems.at[k], device_id=to, device_id_type=MESH)

        first = [copy(j, o_ref.at[chip, mine], (*ch, c)) for j, ch in enumerate(chips)]
        for cp in first:
            cp.start()
        landed = [o_ref.at[2 * ch[0] + ch[1], mine] for ch in chips]
        passed = [copy(3 + j, landed[j], sibling) for j in range(3)]
        for j in range(3):
            copy(j, landed[j], sibling).wait_recv()
            passed[j].start()
        for j, ch in enumerate(chips):
            copy(3 + j, o_ref.at[2 * ch[0] + ch[1], pl.ds((1 - c) * half, half)], sibling).wait_recv()
        for cp in first + passed:
            cp.wait_send()

    hbm = pl.BlockSpec(memory_space=pl.ANY)
    return pl.pallas_call(
        body, in_specs=[hbm], out_specs=[hbm, pl.BlockSpec(memory_space=pltpu.VMEM)],
        out_shape=[jax.ShapeDtypeStruct(buf.shape, buf.dtype), jax.ShapeDtypeStruct((8, LANES), F32)],
        input_output_aliases={0: 0},
        scratch_shapes=[pltpu.SemaphoreType.DMA((6,)), pltpu.SemaphoreType.DMA((6,))],
        name="gather_chips")(buf)


HBM_SPEC = pl.BlockSpec(memory_space=pltpu.HBM)
SEM_SPEC = pl.BlockSpec(memory_space=pltpu.SEMAPHORE)
DATAFLOW_EFFECT = pltpu.SideEffectType.DATAFLOW_SIDE_EFFECTING


def _gather_start(buf, name):
    def body(b_ref, send_sem, recv_sem, b_thru, token):
        x, y, c = _place()
        chip = 2 * x + y
        for ch in _other_chips(x, y):
            pltpu.make_async_remote_copy(src_ref=b_ref.at[chip], dst_ref=b_ref.at[chip], send_sem=send_sem,
                                         recv_sem=recv_sem, device_id=(*ch, c), device_id_type=MESH).start()
        token[...] = jnp.zeros_like(token)

    return pl.pallas_call(
        body, name=name,
        out_shape=(pltpu.SemaphoreType.DMA(()), pltpu.SemaphoreType.DMA(()), pltpu.HBM(buf.shape, buf.dtype),
                   jax.ShapeDtypeStruct((8, LANES), F32)),
        in_specs=(HBM_SPEC,), out_specs=(SEM_SPEC, SEM_SPEC, HBM_SPEC, pl.BlockSpec(memory_space=pltpu.VMEM)),
        input_output_aliases={0: 2},
        compiler_params=pltpu.CompilerParams(has_side_effects=DATAFLOW_EFFECT),
    )(pltpu.with_memory_space_constraint(buf, pltpu.HBM))


def _gather_wait(send_sem, recv_sem, buf_thru, after, name):
    def body(b_ref, send_sem, recv_sem, after_ref, b_out):
        x, y, c = _place()
        three = b_ref.at[pl.ds(0, 3)]
        cp = pltpu.make_async_remote_copy(src_ref=three, dst_ref=three, send_sem=send_sem, recv_sem=recv_sem,
                                          device_id=(x, y, 1 - c), device_id_type=MESH)
        cp.wait_send()
        cp.wait_recv()

    return pl.pallas_call(
        body, name=name, out_shape=pltpu.HBM(buf_thru.shape, buf_thru.dtype),
        in_specs=(HBM_SPEC, SEM_SPEC, SEM_SPEC, pl.BlockSpec(memory_space=pl.ANY)), out_specs=HBM_SPEC,
        input_output_aliases={0: 0},
        compiler_params=pltpu.CompilerParams(has_side_effects=DATAFLOW_EFFECT),
    )(buf_thru, send_sem, recv_sem, after)


def _scatter_start(hb, name):
    def body(h_ref, land_ref, send_sem, recv_sem, h_thru, land_thru, token):
        x, y, c = _place()
        chip = 2 * x + y
        for ch in _other_chips(x, y):
            pltpu.make_async_remote_copy(src_ref=h_ref.at[2 * ch[0] + ch[1]], dst_ref=land_ref.at[chip],
                                         send_sem=send_sem, recv_sem=recv_sem, device_id=(*ch, c),
                                         device_id_type=MESH).start()
        token[...] = jnp.zeros_like(token)

    return pl.pallas_call(
        body, name=name,
        out_shape=(pltpu.SemaphoreType.DMA(()), pltpu.SemaphoreType.DMA(()), pltpu.HBM(hb.shape, hb.dtype),
                   pltpu.HBM(hb.shape, hb.dtype), jax.ShapeDtypeStruct((8, LANES), F32)),
        in_specs=(HBM_SPEC, HBM_SPEC),
        out_specs=(SEM_SPEC, SEM_SPEC, HBM_SPEC, HBM_SPEC, pl.BlockSpec(memory_space=pltpu.VMEM)),
        input_output_aliases={0: 2, 1: 3},
        compiler_params=pltpu.CompilerParams(has_side_effects=DATAFLOW_EFFECT),
    )(pltpu.with_memory_space_constraint(hb, pltpu.HBM),
      pltpu.with_memory_space_constraint(lax.empty(hb.shape, hb.dtype), pltpu.HBM))


def _scatter_wait(send_sem, recv_sem, h_thru, land_thru, after, name):
    def body(h_ref, land_ref, send_sem, recv_sem, after_ref, h_dead, got_ref):
        x, y, c = _place()
        cp = pltpu.make_async_remote_copy(src_ref=h_ref.at[pl.ds(0, 3)], dst_ref=land_ref.at[pl.ds(0, 3)],
                                          send_sem=send_sem, recv_sem=recv_sem, device_id=(x, y, 1 - c),
                                          device_id_type=MESH)
        cp.wait_send()
        cp.wait_recv()

    return pl.pallas_call(
        body, name=name,
        out_shape=(pltpu.HBM(h_thru.shape, h_thru.dtype), pltpu.HBM(land_thru.shape, land_thru.dtype)),
        in_specs=(HBM_SPEC, HBM_SPEC, SEM_SPEC, SEM_SPEC, pl.BlockSpec(memory_space=pl.ANY)),
        out_specs=(HBM_SPEC, HBM_SPEC), input_output_aliases={0: 0, 1: 1},
        compiler_params=pltpu.CompilerParams(has_side_effects=DATAFLOW_EFFECT),
    )(h_thru, land_thru, send_sem, recv_sem, after)[1]


def _swap_halves(g):
    J, R, C = g.shape
    half = R // 2

    def body(g_ref, o_ref, send_sem, recv_sem):
        x, y, c = _place()
        cp = pltpu.make_async_remote_copy(src_ref=g_ref.at[:, pl.ds((1 - c) * half, half), :], dst_ref=o_ref,
                                          send_sem=send_sem, recv_sem=recv_sem, device_id=(x, y, 1 - c),
                                          device_id_type=MESH)
        cp.start()
        cp.wait()

    hbm = pl.BlockSpec(memory_space=pl.ANY)
    return pl.pallas_call(
        body, in_specs=[hbm], out_specs=hbm, out_shape=jax.ShapeDtypeStruct((J, half, C), g.dtype),
        scratch_shapes=[pltpu.SemaphoreType.DMA, pltpu.SemaphoreType.DMA], name="swap_halves")(g)


def _add_half(g, r, place_idx):
    J, R, C = g.shape
    half = R // 2
    tr = _pick(half, max(16, (2 << 20) // (C * 4)), 16)
    nb = half // tr

    def body(s_ref, g_ref, r_ref, o_ref, ob_ref):
        acc = g_ref[...] + r_ref[...]
        o_ref[...] = acc
        ob_ref[...] = acc.astype(ob_ref.dtype)

    spec = pl.BlockSpec((None, tr, C), lambda j, i, s: (j, i, 0))
    return pl.pallas_call(
        body,
        grid_spec=pltpu.PrefetchScalarGridSpec(
            num_scalar_prefetch=1, grid=(J, nb),
            in_specs=[pl.BlockSpec((None, tr, C), lambda j, i, s: (j, s[0] * nb + i, 0)), spec],
            out_specs=[spec, spec]),
        out_shape=[jax.ShapeDtypeStruct((J, half, C), F32), jax.ShapeDtypeStruct((J, half, C), BF16)],
        compiler_params=_params("parallel", "parallel"), name="add_half")(place_idx[4], g, r)


def _add_chips(h, p, place_idx):
    J, R, C = h.shape
    tr = _pick(R, max(16, (2 << 20) // (C * 4)), 16)
    nb = R // tr

    def body(s0, s1, s2, s3, s4, h_ref, p0_ref, p1_ref, p2_ref, o_ref):
        o_ref[...] = ((h_ref[...] + p0_ref[...].astype(F32)) + p1_ref[...].astype(F32)) + p2_ref[...].astype(F32)

    slot = lambda k: pl.BlockSpec((None, tr, C), lambda i, *s: (s[k][0], i, 0))
    return pl.pallas_call(
        body,
        grid_spec=pltpu.PrefetchScalarGridSpec(
            num_scalar_prefetch=5, grid=(nb,),
            in_specs=[slot(0), slot(1), slot(2), slot(3)],
            out_specs=pl.BlockSpec((tr, C), lambda i, *s: (s[4][0] * nb + i, 0))),
        out_shape=jax.ShapeDtypeStruct((2 * R, C), F32),
        compiler_params=_params("parallel"), name="add_chips")(*place_idx, h, p, p, p)


def _join_halves(s):
    R, C = s.shape
    half = R // 2

    def body(in_ref, o_ref, send_sem, recv_sem):
        x, y, c = _place()
        mine = o_ref.at[pl.ds(c * half, half)]
        cp = pltpu.make_async_remote_copy(src_ref=mine, dst_ref=mine, send_sem=send_sem, recv_sem=recv_sem,
                                          device_id=(x, y, 1 - c), device_id_type=MESH)
        cp.start()
        cp.wait()

    hbm = pl.BlockSpec(memory_space=pl.ANY)
    return pl.pallas_call(
        body, in_specs=[hbm], out_specs=hbm, out_shape=jax.ShapeDtypeStruct((R, C), s.dtype),
        input_output_aliases={0: 0},
        scratch_shapes=[pltpu.SemaphoreType.DMA, pltpu.SemaphoreType.DMA], name="join_halves")(s)


def _reduce_scatter_start(g, place_idx, name):
    h, hb = _add_half(g, _swap_halves(g), place_idx)
    send_sem, recv_sem, hb_thru, land_thru, token = _scatter_start(hb, name + "_start")
    return (h, send_sem, recv_sem, hb_thru, land_thru), token


def _reduce_scatter_finish(state, place_idx, after, name):
    h, send_sem, recv_sem, hb_thru, land_thru = state
    p = _scatter_wait(send_sem, recv_sem, hb_thru, land_thru, after, name + "_wait")
    return _join_halves(_add_chips(h, p, place_idx))


def _pack(parts):
    flat = jnp.concatenate([p.reshape(-1).astype(F32) for p in parts])
    n = flat.shape[0]
    rows = -(-n // (8 * LANES)) * 8
    return jnp.pad(flat, (0, rows * LANES - n)).reshape(rows, LANES)


def _unpack(buf, shapes):
    flat = buf.reshape(buf.shape[:-2] + (-1,))
    out, off = [], 0
    for s in shapes:
        n = math.prod(s)
        out.append(flat[..., off:off + n].reshape(buf.shape[:-2] + tuple(s)))
        off += n
    return out


def _mod_table(mod_x, mod_c):
    return jnp.stack([mod_x.reshape(6, 1, -1), mod_c.reshape(6, 1, -1)])


def kernel(x, c, ctx, c_ctx, ada_w, ada_b, ln_g, ln_b, ret_w_qkv, ret_w_g, ret_decay_logit, ret_w_o, mla_w_dq, mla_g_q, mla_w_uq, mla_w_dkv, mla_g_kv, mla_w_ukv, mla_w_o, ffn_w_in, ffn_w_out, loss_target, m_c_ctx, m_ada_w, m_ada_b, m_ln_g, m_ln_b, m_ret_w_qkv, m_ret_w_g, m_ret_decay_logit, m_ret_w_o, m_mla_w_dq, m_mla_g_q, m_mla_w_uq, m_mla_w_dkv, m_mla_g_kv, m_mla_w_ukv, m_mla_w_o, m_ffn_w_in, m_ffn_w_out, v_c_ctx, v_ada_w, v_ada_b, v_ln_g, v_ln_b, v_ret_w_qkv, v_ret_w_g, v_ret_decay_logit, v_ret_w_o, v_mla_w_dq, v_mla_g_q, v_mla_w_uq, v_mla_w_dkv, v_mla_g_kv, v_mla_w_ukv, v_mla_w_o, v_ffn_w_in, v_ffn_w_out):
    Nx, D = x.shape[1], x.shape[2]
    Nc = ctx.shape[1]
    T = Nx + Nc
    tr = 256 if (Nx % 256 == 0 and Nc % 256 == 0) else 128
    n_xt = Nx // tr
    C = RET_CHUNK
    ncx, ncc = Nx // C, Nc // C
    H = RET_HEADS
    dk, dv = D // H, 2 * D // H
    L = MLA_LORA
    HM = MLA_HEADS
    D6 = 6 * D
    n6 = D6 // N_CHIPS
    Dq = D // N_CHIPS
    xi, yi, ci = lax.axis_index("x"), lax.axis_index("y"), lax.axis_index("c")
    chip = 2 * xi + yi
    dev = 4 * xi + 2 * yi + ci
    as_index = lambda s: jnp.reshape(s, (1,)).astype(jnp.int32)
    place_idx = (as_index(chip), as_index(2 * (1 - xi) + yi), as_index(2 * xi + 1 - yi),
                 as_index(2 * (1 - xi) + 1 - yi), as_index(ci))
    chip_idx = place_idx[0]
    tile = dict(T=T, tr=tr, n_xt=n_xt)

    shapes1 = [(D,), (DEPTH, 2, Dq), (DEPTH, 2, Dq), (2, L // N_CHIPS), (2, L // N_CHIPS)]
    g1, _ = _all_gather8(_pack([c[0], ln_g, ln_b, mla_g_q, mla_g_kv]))
    c_all, lng_s, lnb_s, gq_s, gkv_s = _unpack(g1, shapes1)
    by_chip = lambda t: jnp.moveaxis(t[0::2], 0, -2).reshape(t.shape[1:-1] + (-1,))
    ln_g_full, ln_b_full = by_chip(lng_s), by_chip(lnb_s)
    gq_full, gkv_full = by_chip(gq_s), by_chip(gkv_s)

    cond = jnp.concatenate([c_all, c_ctx[None]], 0)
    silu_cond = cond * jax.nn.sigmoid(cond)
    s16 = jnp.pad(silu_cond, ((0, 16 - (N_DEV + 1)), (0, 0))).astype(BF16)
    mods = []
    for i in range(DEPTH):
        bias = lax.dynamic_slice_in_dim(ada_b[i], chip * n6, n6)
        mods.append(_mm_nn(s16, ada_w[i][None], name="ada_fwd")[:N_DEV + 1] + bias[None])
    g2, _ = _all_gather8(_pack([jnp.stack(mods)]))
    (mod_all,) = _unpack(g2, [(DEPTH, N_DEV + 1, n6)])
    mod_all = jnp.moveaxis(mod_all[0::2], 0, -2).reshape(DEPTH, N_DEV + 1, D6)
    mod_tabs = [_mod_table(lax.dynamic_index_in_dim(mod_all[i], dev, 0, False), mod_all[i, N_DEV])
                for i in range(DEPTH)]
    lnps = [[jnp.stack([ln_g_full[i, s], ln_b_full[i, s]])[:, None, :] for s in range(2)] for i in range(DEPTH)]

    def mla_pack(j):
        return jnp.concatenate([mla_w_dq[j], mla_w_uq[j], jnp.pad(mla_w_dkv[j], ((0, 0), (0, 64))),
                                mla_w_ukv[j], mla_w_o[j]], axis=1)

    mla_cols = [L, 3 * L // 2, L + LANES, 2 * L, D]

    def mla_unpack_weights(buf):
        offs = [0]
        for wdt in mla_cols:
            offs.append(offs[-1] + wdt)
        dq_, uq_, dkv_, ukv_, wo_ = [buf[:, :, offs[k]:offs[k + 1]] for k in range(5)]
        w_dq = dq_.reshape(D, L)
        w_dkv = dkv_.reshape(D, L + LANES)
        w_lora = jnp.concatenate([w_dq, w_dkv], axis=1)
        w_uq = jnp.moveaxis(uq_, 0, 1).reshape(L, HM, MLA_D_NOPE + MLA_D_ROPE)
        wq_cat = jnp.pad(w_uq, ((0, 0), (0, 0), (0, 2 * LANES - MLA_D_NOPE - MLA_D_ROPE))).reshape(L, HM * 2 * LANES)
        w_ukv = jnp.moveaxis(ukv_, 0, 1).reshape(L, HM, MLA_D_NOPE + MLA_D_V)
        w_kn = w_ukv[:, :, :MLA_D_NOPE].reshape(L, HM * MLA_D_NOPE)
        w_v = w_ukv[:, :, MLA_D_NOPE:].reshape(L, HM * MLA_D_V)
        w_o = wo_.reshape(HM * MLA_D_V, D)
        return dict(lora=w_lora[None], q=wq_cat[None], kn=w_kn[None], v=w_v[None], o=w_o[None])

    def mla_pack_grads(g):
        d_lora = g["lora"][0]
        d_dq = d_lora[:, :L].reshape(N_CHIPS, Dq, L)
        d_dkv = d_lora[:, L:].reshape(N_CHIPS, Dq, L + LANES)
        d_uq = g["q"][0].reshape(L, HM, 2 * LANES)[:, :, :MLA_D_NOPE + MLA_D_ROPE]
        d_uq = jnp.moveaxis(d_uq.reshape(L, N_CHIPS, -1), 1, 0)
        d_ukv = jnp.concatenate([g["kn"][0].reshape(L, HM, MLA_D_NOPE), g["v"][0].reshape(L, HM, MLA_D_V)], axis=2)
        d_ukv = jnp.moveaxis(d_ukv.reshape(L, N_CHIPS, -1), 1, 0)
        d_o = g["o"][0].reshape(N_CHIPS, L, D)
        return jnp.concatenate([d_dq, d_uq, d_dkv, d_ukv, d_o], axis=2)

    assert Dq == L, "the packed MLA buffer assumes D_MODEL / 4 == 512 rows per shard"

    def layer_shards(i):
        j = i // 2
        own = dict(qkv=ret_w_qkv[j], g=ret_w_g[j], o=ret_w_o[j]) if i % 2 == 0 else dict(mla=mla_pack(j))
        own.update(w_in=ffn_w_in[i], w_out=ffn_w_out[i])
        return own

    def layer_weights(full):
        lw = {}
        if "qkv" in full:
            qkv, gg = full["qkv"], full["g"]
            lw.update(q=qkv[0:1], k=qkv[1:2], v=qkv[2:4], gf=gg[0:2], gb=gg[2:4], o=full["o"].reshape(1, 2 * D, D))
        else:
            lw.update(mla_unpack_weights(full["mla"]))
        w_in = full["w_in"]
        lw.update(a=w_in[0:2], b=w_in[2:4], out=full["w_out"].reshape(1, -1, D))
        return lw

    first = {}
    tok0 = jnp.zeros((8, LANES), F32)
    for n, w_l in layer_shards(0).items():
        first[n], tok = _gather_chips(_place_own(w_l, chip_idx, tok0))
        tok0 = tok0 + tok
    W = [layer_weights(first)] + [None] * (DEPTH - 1)
    in_flight, start_tok = [None] * DEPTH, jnp.zeros((), F32)
    for i in range(1, DEPTH):
        in_flight[i] = {}
        for n, w_l in layer_shards(i).items():
            send_sem, recv_sem, thru, tok = _gather_start(_place_own(w_l, chip_idx, tok0), name=f"gather_start_{i}_{n}")
            in_flight[i][n] = (send_sem, recv_sem, thru)
            start_tok = start_tok + tok[0, 0]

    ret_tabs = [_ret_tables(ret_decay_logit[j], Nx, Nc, dk, dv) for j in range(2)]
    mla_cos, mla_sin = _mla_tables(Nx, Nc)
    gqkv = [jnp.stack([gq_full[j], gkv_full[j]])[:, None, :] for j in range(2)]
    tq_f, tq_b = tr, 128

    h = jnp.concatenate([x[0], ctx[0]], axis=0) + start_tok
    u = _modulate(h, mod_tabs[0], **tile)
    saved = []
    for i in range(DEPTH):
        if i > 0:
            W[i] = layer_weights({n: _gather_wait(*st, after=h, name=f"gather_wait_{i}_{n}")
                                  for n, st in in_flight[i].items()})
        j, lw, mod, sv = i // 2, W[i], mod_tabs[i], {}
        sv.update(h=h, u=u)
        if i % 2 == 0:
            q = _mm_nn(u, lw["q"], name="ret_q")
            k = _mm_nn(u, lw["k"], name="ret_k")
            v = _mm_nn(u, lw["v"], name="ret_v")
            gf = _mm_nn(u, lw["gf"], name="ret_gf")
            gb = _mm_nn(u, lw["gb"], name="ret_gb")
            o, states = _ret_fwd(q, k, v, ret_tabs[j], ncx, ncc)
            yg = _gate_fwd(gf, gb, o, dv=dv, **tile)
            y = _mm_nn(yg, lw["o"], name="ret_o")
            sv.update(q=q, k=k, v=v, gf=gf, gb=gb, o=o, states=states, yg=yg)
        else:
            lora = _mm_nn(u, lw["lora"], name="mla_lora")
            cqn, ckvn, kr = _lora_fwd(lora, gqkv[j], mla_cos, mla_sin, **tile)
            qcat = _mm_nn(cqn, lw["q"], name="mla_q")
            qrot = _qrope(qcat, mla_cos, mla_sin, False, **tile)
            kn = _mm_nn(ckvn, lw["kn"], out_dtype=BF16, name="mla_kn")
            vv = _mm_nn(ckvn, lw["v"], out_dtype=BF16, name="mla_v")
            att, lse = _attn_fwd(qrot, kn, vv, kr, Nx, tq_f)
            y = _mm_nn(att, lw["o"], name="mla_o")
            sv.update(lora=lora, cqn=cqn, ckvn=ckvn, kr=kr, qrot=qrot, kn=kn, vv=vv, att=att, lse=lse)
        h1, u2 = _ln_fwd(h, y, mod, lnps[i][0], 2, mod, (3, 4), **tile)
        a = _mm_nn(u2, lw["a"], name="ffn_a")
        b = _mm_nn(u2, lw["b"], name="ffn_b")
        act = _swiglu_fwd(a, b, **tile)
        f = _mm_nn(act, lw["out"], name="ffn_out")
        last = i == DEPTH - 1
        h2, u_next = _ln_fwd(h1, f, mod, lnps[i][1], 5, None if last else mod_tabs[i + 1], (0, 1), **tile)
        sv.update(y=y, h1=h1, u2=u2, a=a, b=b, act=act, f=f)
        saved.append(sv)
        h, u = h2, u_next

    dh, err_cols = _loss_grad(h, loss_target[0], **tile)
    loss = lax.psum(0.5 * jnp.sum(err_cols) / D, ("x", "y", "c"))

    pending, order_tok = {}, [jnp.zeros((), F32)]

    def rs_start(name, l, g):
        state, tok = _reduce_scatter_start(g.reshape(N_CHIPS, -1, g.shape[-1]), place_idx, f"rs_{name}_{l}")
        pending[(name, l)] = state
        order_tok[0] = order_tok[0] + tok[0, 0]

    d_mods, d_lng, d_lnb = [None] * DEPTH, [None] * DEPTH, [None] * DEPTH
    d_gq, d_gkv, d_lam = [None] * 2, [None] * 2, [None] * 2
    for i in reversed(range(DEPTH)):
        j, lw, sv = i // 2, W[i], saved[i]
        mod = mod_tabs[i] + order_tok[0]
        dh1_res, df, dg_f, dlg1, dlb1 = _ln_bwd(sv["h1"], sv["f"], dh, mod, lnps[i][1], 5, **tile)
        dact = _mm_nt(df, lw["out"], name="ffn_out_nt")
        rs_start("ffn_w_out", i, _mm_tn(sv["act"], df, 1, name="ffn_out_tn").reshape(N_CHIPS, -1, D))
        da, db = _swiglu_bwd(sv["a"], sv["b"], dact, **tile)
        du2 = _mm_nt(db, lw["b"], add=_mm_nt(da, lw["a"], name="ffn_a_nt"), name="ffn_b_nt")
        rs_start("ffn_w_in", i, jnp.concatenate([_mm_tn(sv["u2"], da, 2, name="ffn_a_tn"),
                                                 _mm_tn(sv["u2"], db, 2, name="ffn_b_tn")], 0))
        mod = mod_tabs[i] + order_tok[0]
        dh1, dsc_f, dsh_f = _mod_bwd(dh1_res, du2, sv["h1"], mod, 4, **tile)
        dh_res, dy, dg_a, dlg0, dlb0 = _ln_bwd(sv["h"], sv["y"], dh1, mod, lnps[i][0], 2, **tile)
        uu = sv["u"]
        if i % 2 == 0:
            dyg = _mm_nt(dy, lw["o"], name="ret_o_nt")
            rs_start("ret_w_o", j, _mm_tn(sv["yg"], dy, 1, name="ret_o_tn").reshape(N_CHIPS, -1, D))
            dgf, dgb, do_f, do_b = _gate_bwd(sv["gf"], sv["gb"], sv["o"], dyg, dv=dv, **tile)
            dq2, dk2, dv2, dlam = _ret_bwd(sv["q"], sv["k"], sv["v"], jnp.stack([do_f, do_b]), sv["states"],
                                           ret_tabs[j], ncx, ncc)
            dq, dkk, dvv = _add_dirs(dq2, dk2, dv2, **tile)
            du = _mm_nt(dq, lw["q"], name="ret_q_nt")
            du = _mm_nt(dkk, lw["k"], add=du, name="ret_k_nt")
            du = _mm_nt(dvv, lw["v"], add=du, name="ret_v_nt")
            du = _mm_nt(dgf, lw["gf"], add=du, name="ret_gf_nt")
            du = _mm_nt(dgb, lw["gb"], add=du, name="ret_gb_nt")
            rs_start("ret_w_qkv", j, jnp.concatenate([_mm_tn(uu, dq, 1, name="ret_q_tn"),
                                                      _mm_tn(uu, dkk, 1, name="ret_k_tn"),
                                                      _mm_tn(uu, dvv, 2, name="ret_v_tn")], 0))
            rs_start("ret_w_g", j, jnp.concatenate([_mm_tn(uu, dgf, 2, name="ret_gf_tn"),
                                                    _mm_tn(uu, dgb, 2, name="ret_gb_tn")], 0))
            d_lam[j] = dlam[:, :, 0, 0]
        else:
            datt = _mm_nt(dy, lw["o"], name="mla_o_nt").astype(BF16)
            gm = dict(o=_mm_tn(sv["att"], dy, 1, name="mla_o_tn"))
            dqcat, dkn, dvv, dkr = _attn_bwd(sv["qrot"], sv["kn"], sv["vv"], sv["kr"], datt, sv["lse"], Nx, tq_b)
            dqraw = _qrope(dqcat, mla_cos, mla_sin, True, **tile)
            dcqn = _mm_nt(dqraw, lw["q"], name="mla_q_nt")
            gm["q"] = _mm_tn(sv["cqn"], dqraw, 1, name="mla_q_tn")
            dckvn = _mm_nt(dvv, lw["v"], add=_mm_nt(dkn, lw["kn"], name="mla_kn_nt"), name="mla_v_nt")
            gm["kn"] = _mm_tn(sv["ckvn"], dkn, 1, name="mla_kn_tn")
            gm["v"] = _mm_tn(sv["ckvn"], dvv, 1, name="mla_v_tn")
            dlora, dgq, dgkv = _lora_bwd(sv["lora"], dcqn, dckvn, dkr, gqkv[j], mla_cos, mla_sin, **tile)
            du = _mm_nt(dlora, lw["lora"], name="mla_lora_nt")
            gm["lora"] = _mm_tn(uu, dlora, 1, name="mla_lora_tn")
            rs_start("mla", j, mla_pack_grads(gm))
            d_gq[j], d_gkv[j] = dgq[0], dgkv[0]
        mod = mod_tabs[i] + order_tok[0]
        dh, dsc_a, dsh_a = _mod_bwd(dh_res, du, sv["h"], mod, 1, **tile)
        d_mods[i] = jnp.concatenate([dsh_a, dsc_a, dg_a, dsh_f, dsc_f, dg_f], axis=2)[:, 0, :]
        d_lng[i] = jnp.concatenate([dlg0, dlg1], 0)
        d_lnb[i] = jnp.concatenate([dlb0, dlb1], 0)

    grad_x = dh[:Nx][None]

    d_mods = jnp.stack(d_mods)
    dlogit = jnp.stack([d_lam[j] * jax.nn.sigmoid(-ret_decay_logit[j]) for j in range(2)])
    shapes3 = [(DEPTH, D6), (DEPTH, D6), (DEPTH, 2, D), (DEPTH, 2, D), (2, L), (2, L), (2, 2, H)]
    g3, s3 = _all_gather8(_pack([d_mods[:, 0], d_mods[:, 1], jnp.stack(d_lng), jnp.stack(d_lnb),
                                 jnp.stack(d_gq), jnp.stack(d_gkv), dlogit]))
    dmod_x_all = _unpack(g3, shapes3)[0]
    dmod_x_sum, dmod_c_sum, g_lng, g_lnb, g_gq, g_gkv, g_decay = _unpack(s3, shapes3)
    grad_ada_b = dmod_x_sum + dmod_c_sum
    dmod9 = jnp.concatenate([jnp.moveaxis(dmod_x_all, 0, 1), dmod_c_sum[:, None]], axis=1)
    dmod16 = jnp.pad(lax.dynamic_slice_in_dim(dmod9, chip * n6, n6, axis=2), ((0, 0), (0, 16 - (N_DEV + 1)), (0, 0)))
    dmod16 = dmod16.astype(BF16)
    grad_ada_w = jnp.stack([_mm_tn(s16, dmod16[i], 1, name="ada_tn")[0] for i in range(DEPTH)])
    dsilu = _mm_nt(jnp.moveaxis(dmod16, 0, 1).reshape(16, DEPTH * n6), ada_w, name="ada_nt")
    _, s4 = _all_gather8(_pack([dsilu[N_DEV]]))
    sg = jax.nn.sigmoid(c_ctx)
    grad_c_ctx = (0.5 * s4.reshape(-1)[:D]) * (sg * (1.0 + c_ctx * (1.0 - sg)))

    my_cols = lambda t, n: lax.dynamic_slice_in_dim(t, chip * n, n, axis=t.ndim - 1)
    grad_ln_g, grad_ln_b = my_cols(g_lng, Dq), my_cols(g_lnb, Dq)
    grad_gq, grad_gkv = my_cols(g_gq, L // N_CHIPS), my_cols(g_gkv, L // N_CHIPS)

    def rs_done(name, l):
        return _reduce_scatter_finish(pending[(name, l)], place_idx, dh, f"rs_{name}_{l}")

    def rs(name, n_layers, shard_shape):
        return jnp.stack([rs_done(name, l).reshape(shard_shape) for l in range(n_layers)])

    grad_ret_w_qkv = rs("ret_w_qkv", 2, ret_w_qkv.shape[1:])
    grad_ret_w_g = rs("ret_w_g", 2, ret_w_g.shape[1:])
    grad_ret_w_o = rs("ret_w_o", 2, ret_w_o.shape[1:])
    grad_ffn_w_in = rs("ffn_w_in", DEPTH, ffn_w_in.shape[1:])
    grad_ffn_w_out = rs("ffn_w_out", DEPTH, ffn_w_out.shape[1:])
    mla_red = [rs_done("mla", l) for l in range(2)]
    offs = [0]
    for wdt in mla_cols:
        offs.append(offs[-1] + wdt)
    mla_parts = [jnp.stack([mla_red[l][:, offs[k]:offs[k + 1]] for l in range(2)]) for k in range(5)]
    grad_mla_w_dq, grad_mla_w_uq, grad_mla_w_ukv, grad_mla_w_o = mla_parts[0], mla_parts[1], mla_parts[3], mla_parts[4]
    grad_mla_w_dkv = mla_parts[2][:, :, :L + MLA_D_ROPE]

    grads = [grad_c_ctx, grad_ada_w, grad_ada_b, grad_ln_g, grad_ln_b, grad_ret_w_qkv, grad_ret_w_g, g_decay,
             grad_ret_w_o, grad_mla_w_dq, grad_gq, grad_mla_w_uq, grad_mla_w_dkv, grad_gkv, grad_mla_w_ukv,
             grad_mla_w_o, grad_ffn_w_in, grad_ffn_w_out]
    weights = [c_ctx, ada_w, ada_b, ln_g, ln_b, ret_w_qkv, ret_w_g, ret_decay_logit, ret_w_o, mla_w_dq, mla_g_q,
               mla_w_uq, mla_w_dkv, mla_g_kv, mla_w_ukv, mla_w_o, ffn_w_in, ffn_w_out]
    ms = [m_c_ctx, m_ada_w, m_ada_b, m_ln_g, m_ln_b, m_ret_w_qkv, m_ret_w_g, m_ret_decay_logit, m_ret_w_o,
          m_mla_w_dq, m_mla_g_q, m_mla_w_uq, m_mla_w_dkv, m_mla_g_kv, m_mla_w_ukv, m_mla_w_o, m_ffn_w_in, m_ffn_w_out]
    vs = [v_c_ctx, v_ada_w, v_ada_b, v_ln_g, v_ln_b, v_ret_w_qkv, v_ret_w_g, v_ret_decay_logit, v_ret_w_o,
          v_mla_w_dq, v_mla_g_q, v_mla_w_uq, v_mla_w_dkv, v_mla_g_kv, v_mla_w_ukv, v_mla_w_o, v_ffn_w_in, v_ffn_w_out]
    upd = [_adamw(w_, g_, m_, v_) for w_, g_, m_, v_ in zip(weights, grads, ms, vs)]
    return (loss, grad_x, *grads, *[u_[0] for u_ in upd], *[u_[1] for u_ in upd], *[u_[2] for u_ in upd])
```

```python
import functools
import math

import jax
import jax.numpy as jnp
from jax import lax
from jax.experimental import pallas as pl
from jax.experimental.pallas import tpu as pltpu

F32, BF16 = jnp.float32, jnp.bfloat16
MESH = pl.DeviceIdType.MESH
V7X_VMEM_LIMIT_BYTES = 56 * 1024 * 1024
LANES = 128

GRID_W = 64
RET_HEADS = 8
RET_CHUNK = 128
RET_ROPE_BASE = 10000.0
GN_EPS = 1e-6
MLA_HEADS = 16
MLA_LORA = 512
MLA_D_NOPE = 128
MLA_D_ROPE = 64
MLA_D_V = 128
MLA_SCALE = (MLA_D_NOPE + MLA_D_ROPE) ** -0.5
AXIAL_ROPE_BASE = 10000.0
RMS_EPS = 1e-6
DEPTH = 4
DEEPNORM_ALPHA = (2 * DEPTH) ** 0.25
LN_EPS = 1e-5
ADAM_LR, ADAM_B1, ADAM_B2, ADAM_EPS, ADAM_WD, ADAM_STEP = 0.001, 0.9, 0.999, 1e-08, 0.01, 10
N_CHIPS = 4
N_DEV = 8


def _pick(dim, target, mult):
    best = None
    for d in range(mult, min(dim, target) + 1, mult):
        if dim % d == 0:
            best = d
    return dim if best is None else best


def _params(*sem):
    return pltpu.CompilerParams(dimension_semantics=sem, vmem_limit_bytes=V7X_VMEM_LIMIT_BYTES)


def _sigmoid(x):
    return 1.0 / (1.0 + jnp.exp(-x))


def _accumulate(step, nsteps, acc, part, write):
    if nsteps == 1:
        write(part)
        return

    @pl.when(step == 0)
    def _():
        acc[...] = part

    @pl.when(jnp.logical_and(step > 0, step < nsteps - 1))
    def _():
        acc[...] += part

    @pl.when(step == nsteps - 1)
    def _():
        write(acc[...] + part)


def _mm_nn(a, w, out_dtype=F32, name="mm_nn"):
    M, K = a.shape
    J, K2, n = w.shape
    assert K == K2
    tn = _pick(n, 1408, LANES)
    tk = _pick(K, 2048, LANES)
    tm = _pick(M, max(16, (6 << 20) // (tn * 4)), 16)
    npj, nk = n // tn, K // tk

    def body(a_ref, w_ref, o_ref, acc):
        part = jnp.dot(a_ref[...].astype(BF16), w_ref[...].astype(BF16), preferred_element_type=F32)

        def write(total):
            o_ref[...] = total.astype(o_ref.dtype)

        _accumulate(pl.program_id(2), nk, acc, part, write)

    return pl.pallas_call(
        body, grid=(M // tm, J * npj, nk),
        in_specs=[pl.BlockSpec((tm, tk), lambda i, j, k: (i, k)),
                  pl.BlockSpec((None, tk, tn), lambda i, j, k: (j // npj, k, j % npj))],
        out_specs=pl.BlockSpec((tm, tn), lambda i, j, k: (i, j)),
        out_shape=jax.ShapeDtypeStruct((M, J * n), out_dtype),
        scratch_shapes=[pltpu.VMEM((tm, tn), F32)],
        compiler_params=_params("parallel", "parallel", "arbitrary"), name=name)(a, w)


def _mm_nt(dy, w, add=None, name="mm_nt"):
    M, N = dy.shape
    J, K, n = w.shape
    assert N == J * n
    tko = _pick(K, 512, LANES)
    tc = _pick(n, 2048, LANES)
    tm = _pick(M, max(16, (4 << 20) // (tko * 4)), 16)
    npj = n // tc
    nc = J * npj
    has_add = add is not None

    def body(*refs):
        if has_add:
            dy_ref, w_ref, add_ref, o_ref, acc = refs
        else:
            dy_ref, w_ref, o_ref, acc = refs
        part = lax.dot_general(dy_ref[...].astype(BF16), w_ref[...].astype(BF16),
                               (((1,), (1,)), ((), ())), preferred_element_type=F32)

        def write(total):
            o_ref[...] = total + add_ref[...] if has_add else total

        _accumulate(pl.program_id(2), nc, acc, part, write)

    in_specs = [pl.BlockSpec((tm, tc), lambda i, ko, c: (i, c)),
                pl.BlockSpec((None, tko, tc), lambda i, ko, c: (c // npj, ko, c % npj))]
    args = [dy, w]
    if has_add:
        in_specs.append(pl.BlockSpec((tm, tko), lambda i, ko, c: (i, ko)))
        args.append(add)
    return pl.pallas_call(
        body, grid=(M // tm, K // tko, nc), in_specs=in_specs,
        out_specs=pl.BlockSpec((tm, tko), lambda i, ko, c: (i, ko)),
        out_shape=jax.ShapeDtypeStruct((M, K), F32),
        scratch_shapes=[pltpu.VMEM((tm, tko), F32)],
        compiler_params=_params("parallel", "parallel", "arbitrary"), name=name)(*args)


def _mm_tn(a, dy, J, name="mm_tn", slots=None, slot0=0, into=None):
    M, K = a.shape
    M2, N = dy.shape
    assert M == M2 and N % J == 0
    n = N // J
    tko = _pick(K, 1024, LANES)
    tn = _pick(n, 1408, LANES)
    tmc = _pick(M, 2176, 16)
    npj, nm = n // tn, M // tmc

    def body(*refs):
        a_ref, dy_ref, o_ref, acc = refs[0], refs[1], refs[-2], refs[-1]
        part = lax.dot_general(a_ref[...].astype(BF16), dy_ref[...].astype(BF16),
                               (((0,), (0,)), ((), ())), preferred_element_type=F32)

        def write(total):
            o_ref[...] = total

        _accumulate(pl.program_id(2), nm, acc, part, write)

    in_specs = [pl.BlockSpec((tmc, tko), lambda ko, j, m: (m, ko)),
                pl.BlockSpec((tmc, tn), lambda ko, j, m: (m, j))]
    args, aliases = [a, dy], {}
    if into is not None:
        in_specs.append(pl.BlockSpec(memory_space=pl.ANY))
        args.append(into)
        aliases = {2: 0}
    return pl.pallas_call(
        body, grid=(K // tko, J * npj, nm), in_specs=in_specs,
        out_specs=pl.BlockSpec((None, tko, tn), lambda ko, j, m: (slot0 + j // npj, ko, j % npj)),
        out_shape=jax.ShapeDtypeStruct((slots or J, K, n), F32), input_output_aliases=aliases,
        scratch_shapes=[pltpu.VMEM((tko, tn), F32)],
        compiler_params=_params("parallel", "parallel", "arbitrary"), name=name)(*args)


def _rowwise(name, fn, T, tr, n_xt, row_in, grp_in=(), const_in=(), row_out=(), gsum_w=(), tsum_w=(), ncol=1):
    nt = T // tr
    n_in = len(row_in) + len(grp_in) + len(const_in)
    n_ro, n_gs = len(row_out), len(gsum_w)
    assert ncol == 1 or not (gsum_w or tsum_w)

    def body(*refs):
        t = pl.program_id(0)
        vals = [r[...] for r in refs[:n_in]]
        rv = vals[:len(row_in)]
        gv = vals[len(row_in):len(row_in) + len(grp_in)]
        cv = vals[len(row_in) + len(grp_in):]
        ro, gs, ts = fn(rv, gv, cv)
        outs = refs[n_in:]
        for ref, val in zip(outs[:n_ro], ro):
            ref[...] = val.astype(ref.dtype)
        first_g = jnp.logical_or(t == 0, t == n_xt)
        for ref, val, first in ([(r, v, first_g) for r, v in zip(outs[n_ro:n_ro + n_gs], gs)]
                                + [(r, v, t == 0) for r, v in zip(outs[n_ro + n_gs:], ts)]):
            s = jnp.sum(val, axis=0, keepdims=True)

            @pl.when(first)
            def _(ref=ref, s=s):
                ref[...] = s

            @pl.when(jnp.logical_not(first))
            def _(ref=ref, s=s):
                ref[...] += s

    in_specs, args = [], []
    for spec in row_in:
        arr, width = spec[:2]
        lead = spec[2] if len(spec) > 2 else None
        step = spec[3] if len(spec) > 3 else 1
        if lead is not None:
            in_specs.append(pl.BlockSpec((None, tr, width), lambda t, cb, lead=lead, step=step: (lead, t, cb * step)))
        else:
            in_specs.append(pl.BlockSpec((tr, width), lambda t, cb, step=step: (t, cb * step)))
        args.append(arr)
    for arr in grp_in:
        in_specs.append(pl.BlockSpec((None,) + arr.shape[1:], lambda t, cb: (jnp.where(t >= n_xt, 1, 0), 0, 0, 0)))
        args.append(arr)
    for arr in const_in:
        in_specs.append(pl.BlockSpec(arr.shape, lambda t, cb: (0, 0, 0)))
        args.append(arr)
    out_specs, out_shape = [], []
    for wtot, wblk, dt in row_out:
        out_specs.append(pl.BlockSpec((tr, wblk), lambda t, cb: (t, cb)))
        out_shape.append(jax.ShapeDtypeStruct((T, wtot), dt))
    for w in gsum_w:
        out_specs.append(pl.BlockSpec((None, 1, w), lambda t, cb: (jnp.where(t >= n_xt, 1, 0), 0, 0)))
        out_shape.append(jax.ShapeDtypeStruct((2, 1, w), F32))
    for w in tsum_w:
        out_specs.append(pl.BlockSpec((1, w), lambda t, cb: (0, 0)))
        out_shape.append(jax.ShapeDtypeStruct((1, w), F32))
    sem = ("arbitrary", "arbitrary") if (gsum_w or tsum_w) else ("parallel", "parallel")
    return pl.pallas_call(body, grid=(nt, ncol), in_specs=in_specs, out_specs=out_specs, out_shape=out_shape,
                          compiler_params=_params(*sem), name=name)(*args)


def _modulate(h, mod, T, tr, n_xt):
    D = h.shape[1]

    def fn(rv, gv, cv):
        m = gv[0]
        return [rv[0] * (1.0 + m[1]) + m[0]], [], []

    return _rowwise("modulate", fn, T, tr, n_xt, [(h, D)], [mod], row_out=[(D, D, BF16)])[0]


def _ln_stats(r):
    mu = jnp.mean(r, axis=-1, keepdims=True)
    xc = r - mu
    var = jnp.mean(xc * xc, axis=-1, keepdims=True)
    rstd = lax.rsqrt(var + LN_EPS)
    return xc * rstd, rstd


def _ln_fwd(h, y, mod, lnp, gate_row, mod_next, next_rows, T, tr, n_xt):
    D = h.shape[1]
    with_u = mod_next is not None

    def fn(rv, gv, cv):
        r = DEEPNORM_ALPHA * rv[0] + gv[0][gate_row] * rv[1]
        xhat, _ = _ln_stats(r)
        out = xhat * cv[0][0] + cv[0][1]
        if not with_u:
            return [out], [], []
        mn = gv[1]
        return [out, out * (1.0 + mn[next_rows[1]]) + mn[next_rows[0]]], [], []

    grp = [mod, mod_next] if with_u else [mod]
    outs = [(D, D, F32), (D, D, BF16)] if with_u else [(D, D, F32)]
    res = _rowwise("ln_fwd", fn, T, tr, n_xt, [(h, D), (y, D)], grp, [lnp], row_out=outs)
    return (res[0], res[1]) if with_u else (res[0], None)


def _ln_bwd(h, y, dout, mod, lnp, gate_row, T, tr, n_xt):
    D = h.shape[1]

    def fn(rv, gv, cv):
        g = gv[0][gate_row]
        r = DEEPNORM_ALPHA * rv[0] + g * rv[1]
        xhat, rstd = _ln_stats(r)
        dxh = rv[2] * cv[0][0]
        m1 = jnp.mean(dxh, axis=-1, keepdims=True)
        m2 = jnp.mean(dxh * xhat, axis=-1, keepdims=True)
        dr = rstd * (dxh - m1 - xhat * m2)
        return [DEEPNORM_ALPHA * dr, g * dr], [dr * rv[1]], [rv[2] * xhat, rv[2]]

    return _rowwise("ln_bwd", fn, T, tr, n_xt, [(h, D), (y, D), (dout, D)], [mod], [lnp],
                    row_out=[(D, D, F32), (D, D, BF16)], gsum_w=[D], tsum_w=[D, D])


def _mod_bwd(dh_res, du, h, mod, scale_row, T, tr, n_xt):
    D = h.shape[1]

    def fn(rv, gv, cv):
        return [rv[0] + rv[1] * (1.0 + gv[0][scale_row])], [rv[1] * rv[2], rv[1]], []

    return _rowwise("mod_bwd", fn, T, tr, n_xt, [(dh_res, D), (du, D), (h, D)], [mod],
                    row_out=[(D, D, F32)], gsum_w=[D, D])


def _swiglu_fwd(a, b, T, tr, n_xt):
    F = a.shape[1]
    wc = _pick(F, 1408, LANES)

    def fn(rv, gv, cv):
        return [rv[0] * _sigmoid(rv[0]) * rv[1]], [], []

    return _rowwise("swiglu_fwd", fn, T, tr, n_xt, [(a, wc), (b, wc)], row_out=[(F, wc, BF16)], ncol=F // wc)[0]


def _swiglu_bwd(a, b, dact, T, tr, n_xt):
    F = a.shape[1]
    wc = _pick(F, 1408, LANES)

    def fn(rv, gv, cv):
        av, bv, dv = rv
        sg = _sigmoid(av)
        return [dv * bv * (sg * (1.0 + av * (1.0 - sg))), dv * av * sg], [], []

    return _rowwise("swiglu_bwd", fn, T, tr, n_xt, [(a, wc), (b, wc), (dact, wc)],
                    row_out=[(F, wc, BF16), (F, wc, BF16)], ncol=F // wc)


def _gn(o):
    mu = jnp.mean(o, axis=-1, keepdims=True)
    xc = o - mu
    var = jnp.mean(xc * xc, axis=-1, keepdims=True)
    rstd = lax.rsqrt(var + GN_EPS)
    return xc * rstd, rstd


def _gate_fwd(gf, gb, o, T, tr, n_xt, dv):
    W = gf.shape[1]

    def fn(rv, gv, cv):
        xf, _ = _gn(rv[2])
        xb, _ = _gn(rv[3])
        return [rv[0] * _sigmoid(rv[0]) * xf + rv[1] * _sigmoid(rv[1]) * xb], [], []

    return _rowwise("gate_fwd", fn, T, tr, n_xt, [(gf, dv), (gb, dv), (o, dv, 0), (o, dv, 1)],
                    row_out=[(W, dv, BF16)], ncol=W // dv)[0]


def _gate_bwd(gf, gb, o, dy, T, tr, n_xt, dv):
    W = gf.shape[1]

    def fn(rv, gv, cv):
        outs_g, outs_o = [], []
        for g, ov in ((rv[0], rv[2]), (rv[1], rv[3])):
            xh, rstd = _gn(ov)
            sg = _sigmoid(g)
            outs_g.append(rv[4] * xh * (sg * (1.0 + g * (1.0 - sg))))
            dxh = rv[4] * g * sg
            m1 = jnp.mean(dxh, axis=-1, keepdims=True)
            m2 = jnp.mean(dxh * xh, axis=-1, keepdims=True)
            outs_o.append(rstd * (dxh - m1 - xh * m2))
        return outs_g + outs_o, [], []

    return _rowwise("gate_bwd", fn, T, tr, n_xt,
                    [(gf, dv), (gb, dv), (o, dv, 0), (o, dv, 1), (dy, dv)],
                    row_out=[(W, dv, BF16)] * 4, ncol=W // dv)


def _add_dirs(dq, dk, dv_, T, tr, n_xt):
    ws = [dq.shape[2], dk.shape[2], dv_.shape[2]]

    def fn(rv, gv, cv):
        return [rv[0] + rv[1], rv[2] + rv[3], rv[4] + rv[5]], [], []

    row_in = []
    for arr, w in zip((dq, dk, dv_), ws):
        row_in += [(arr, w, 0), (arr, w, 1)]
    return _rowwise("add_dirs", fn, T, tr, n_xt, row_in, row_out=[(w, w, BF16) for w in ws])


def _swap16(x):
    lane = lax.broadcasted_iota(jnp.int32, x.shape, x.ndim - 1)
    return jnp.where(lane % 32 < 16, pltpu.roll(x, LANES - 16, x.ndim - 1), pltpu.roll(x, 16, x.ndim - 1))


def _rope2d(x, cos, sin, transpose):
    if transpose:
        return x * cos + _swap16(x * sin)
    return x * cos + _swap16(x) * sin


def _rms(x, g):
    rstd = lax.rsqrt(jnp.mean(x * x, axis=-1, keepdims=True) + RMS_EPS)
    return x * rstd, rstd


def _lora_fwd(lora, gq_gkv, cos, sin, T, tr, n_xt):
    L = MLA_LORA

    def fn(rv, gv, cv):
        x = rv[0]
        xq, _ = _rms(x[:, :L], None)
        xkv, _ = _rms(x[:, L:2 * L], None)
        return [xq * cv[0][0], xkv * cv[0][1], _rope2d(x[:, 2 * L:], rv[1], rv[2], False)], [], []

    return _rowwise("lora_fwd", fn, T, tr, n_xt, [(lora, 2 * L + LANES), (cos, LANES), (sin, LANES)],
                    const_in=[gq_gkv], row_out=[(L, L, BF16), (L, L, BF16), (LANES, LANES, BF16)])


def _lora_bwd(lora, dq, dkv, dkr, gq_gkv, cos, sin, T, tr, n_xt):
    L = MLA_LORA

    def fn(rv, gv, cv):
        x = rv[0]
        outs, sums = [], []
        for xs, dy, g in ((x[:, :L], rv[1], cv[0][0]), (x[:, L:2 * L], rv[2], cv[0][1])):
            xh, rstd = _rms(xs, None)
            dxh = dy * g
            outs.append(rstd * (dxh - xh * jnp.mean(dxh * xh, axis=-1, keepdims=True)))
            sums.append(dy * xh)
        outs.append(_rope2d(rv[3], rv[4], rv[5], True))
        return [jnp.concatenate(outs, axis=1)], [], sums

    W = 2 * L + LANES
    return _rowwise("lora_bwd", fn, T, tr, n_xt,
                    [(lora, W), (dq, L), (dkv, L), (dkr, LANES), (cos, LANES), (sin, LANES)],
                    const_in=[gq_gkv], row_out=[(W, W, BF16)], tsum_w=[L, L])


def _qrope(q, cos, sin, transpose, T, tr, n_xt):
    W = q.shape[1]

    def fn(rv, gv, cv):
        x = rv[0] * MLA_SCALE
        parts = []
        for hd in range(W // (2 * LANES)):
            lo = hd * 2 * LANES
            parts += [x[:, lo:lo + LANES], _rope2d(x[:, lo + LANES:lo + 2 * LANES], rv[1], rv[2], transpose)]
        return [jnp.concatenate(parts, axis=1)], [], []

    return _rowwise("qrope_bwd" if transpose else "qrope_fwd", fn, T, tr, n_xt,
                    [(q, W), (cos, LANES), (sin, LANES)], row_out=[(W, W, BF16)])[0]


def _loss_grad(h, target, T, tr, n_xt):
    D = h.shape[1]
    nt = T // tr

    def body(h_ref, t_ref, dh_ref, s_ref):
        t = pl.program_id(0)
        diff = jnp.where(t < n_xt, h_ref[...] - t_ref[...], 0.0)
        dh_ref[...] = diff * (1.0 / D)
        s = jnp.sum(diff * diff, axis=0, keepdims=True)

        @pl.when(t == 0)
        def _():
            s_ref[...] = s

        @pl.when(t != 0)
        def _():
            s_ref[...] += s

    return pl.pallas_call(
        body, grid=(nt,),
        in_specs=[pl.BlockSpec((tr, D), lambda t: (t, 0)),
                  pl.BlockSpec((tr, D), lambda t: (jnp.minimum(t, n_xt - 1), 0))],
        out_specs=[pl.BlockSpec((tr, D), lambda t: (t, 0)), pl.BlockSpec((1, D), lambda t: (0, 0))],
        out_shape=[jax.ShapeDtypeStruct((T, D), F32), jax.ShapeDtypeStruct((1, D), F32)],
        compiler_params=_params("arbitrary"), name="loss_grad")(h, target)


def _adamw(w, g, m, v):
    shape = w.shape
    C = shape[-1] if w.ndim > 1 else shape[0]
    R = w.size // C
    tr = _pick(R, max(8, (2 << 20) // (C * 4)), 8)
    c1 = 1.0 - ADAM_B1 ** ADAM_STEP
    c2 = 1.0 - ADAM_B2 ** ADAM_STEP

    def body(w_ref, g_ref, m_ref, v_ref, d_ref, nm_ref, nv_ref):
        gv = g_ref[...]
        nm = ADAM_B1 * m_ref[...] + (1.0 - ADAM_B1) * gv
        nv = ADAM_B2 * v_ref[...] + (1.0 - ADAM_B2) * (gv * gv)
        nm_ref[...] = nm
        nv_ref[...] = nv
        d_ref[...] = -ADAM_LR * ((nm / c1) / (jnp.sqrt(nv / c2) + ADAM_EPS) + ADAM_WD * w_ref[...])

    spec = pl.BlockSpec((tr, C), lambda i: (i, 0))
    outs = pl.pallas_call(
        body, grid=(R // tr,), in_specs=[spec] * 4, out_specs=[spec] * 3,
        out_shape=[jax.ShapeDtypeStruct((R, C), F32)] * 3,
        compiler_params=_params("parallel"), name="adamw")(*[t.reshape(R, C) for t in (w, g, m, v)])
    return tuple(o.reshape(shape) for o in outs)


def _ret_chunk(d, s, ncx, ncc):
    fwd = jnp.where(s < ncc, ncx + s, s - ncc)
    bwd = jnp.where(s < ncc, ncx + ncc - 1 - s, ncx - 1 - (s - ncc))
    return jnp.where(d == 0, fwd, bwd)


def _rot_half(x, cos, sin, transpose):
    half = x.shape[-1] // 2
    if transpose:
        return x * cos + pltpu.roll(x * sin, half, x.ndim - 1)
    return x * cos + pltpu.roll(x, half, x.ndim - 1) * sin


def _dot_nt(a, b):
    return lax.dot_general(a, b, (((1,), (1,)), ((), ())), preferred_element_type=F32)


def _dot_tn(a, b):
    return lax.dot_general(a, b, (((0,), (0,)), ((), ())), preferred_element_type=F32)


def _dot(a, b):
    return jnp.dot(a, b, preferred_element_type=F32)


def _ret_fwd(q, k, v, tabs, ncx, ncc):
    T, C, H = q.shape[0], RET_CHUNK, RET_HEADS
    dk, dv = q.shape[1] // H, v.shape[1] // H
    ns = ncx + ncc
    kscale = dk ** -0.5

    def body(q_ref, k_ref, v_ref, cos_ref, sin_ref, intra_ref, qd_ref, kd_ref, cd_ref, o_ref, st_ref, s_scr):
        @pl.when(pl.program_id(2) == 0)
        def _():
            s_scr[...] = jnp.zeros_like(s_scr)

        cos, sin = cos_ref[...], sin_ref[...]
        qv = _rot_half(q_ref[...], cos, sin, False)
        kv = _rot_half(k_ref[...], cos, sin, False) * kscale
        vb = v_ref[...].astype(BF16)
        S = s_scr[...]
        st_ref[...] = S
        scores = _dot_nt(qv.astype(BF16), kv.astype(BF16)) * intra_ref[...]
        o_ref[...] = _dot(scores.astype(BF16), vb) + _dot((qv * qd_ref[...]).astype(BF16), S.astype(BF16))
        s_scr[...] = S * cd_ref[...] + _dot_tn((kv * kd_ref[...]).astype(BF16), vb)

    chunk = lambda d, h, s: _ret_chunk(d, s, ncx, ncc)
    tab = lambda shape: pl.BlockSpec((None, None) + shape, lambda d, h, s: (d, h, 0, 0))
    return pl.pallas_call(
        body, grid=(2, H, ns),
        in_specs=[pl.BlockSpec((C, dk), lambda d, h, s: (chunk(d, h, s), h)),
                  pl.BlockSpec((C, dk), lambda d, h, s: (chunk(d, h, s), h)),
                  pl.BlockSpec((C, dv), lambda d, h, s: (chunk(d, h, s), h)),
                  pl.BlockSpec((C, dk), lambda d, h, s: (chunk(d, h, s), 0)),
                  pl.BlockSpec((C, dk), lambda d, h, s: (chunk(d, h, s), 0)),
                  tab((C, C)), tab((C, dk)), tab((C, dk)), tab((1, dv))],
        out_specs=[pl.BlockSpec((None, C, dv), lambda d, h, s: (d, chunk(d, h, s), h)),
                   pl.BlockSpec((None, None, None, dk, dv), lambda d, h, s: (d, h, s, 0, 0))],
        out_shape=[jax.ShapeDtypeStruct((2, T, H * dv), F32), jax.ShapeDtypeStruct((2, H, ns, dk, dv), F32)],
        scratch_shapes=[pltpu.VMEM((dk, dv), F32)],
        compiler_params=_params("parallel", "parallel", "arbitrary"), name="ret_fwd",
    )(q, k, v, tabs["cos"], tabs["sin"], tabs["intra"], tabs["qd"], tabs["kd"], tabs["cd"])


def _ret_bwd(q, k, v, do, states, tabs, ncx, ncc):
    T, C, H = q.shape[0], RET_CHUNK, RET_HEADS
    dk, dv = q.shape[1] // H, v.shape[1] // H
    ns = ncx + ncc
    kscale = dk ** -0.5

    def body(q_ref, k_ref, v_ref, do_ref, st_ref, cos_ref, sin_ref, intra_ref, qd_ref, kd_ref, cd_ref,
             dm_ref, wq_ref, wk_ref, dq_ref, dk_ref, dv_ref, dl_ref, ds_scr):
        first = pl.program_id(2) == 0

        @pl.when(first)
        def _():
            ds_scr[...] = jnp.zeros_like(ds_scr)

        cos, sin = cos_ref[...], sin_ref[...]
        qv = _rot_half(q_ref[...], cos, sin, False)
        kv = _rot_half(k_ref[...], cos, sin, False) * kscale
        qb, kb = qv.astype(BF16), kv.astype(BF16)
        vb = v_ref[...].astype(BF16)
        dob = do_ref[...]
        intra, qd, kd, cd = intra_ref[...], qd_ref[...], kd_ref[...], cd_ref[...]
        S, dS = st_ref[...], ds_scr[...]
        Sb, dSb = S.astype(BF16), dS.astype(BF16)
        P = _dot_nt(qb, kb) * intra
        dP_raw = _dot_nt(dob, vb)
        dPb = (dP_raw * intra).astype(BF16)
        dq_cross = _dot_nt(dob, Sb) * qd
        dq_rot = _dot(dPb, kb) + dq_cross
        dk_state = _dot_nt(vb, dSb) * kd
        dk_rot = _dot_tn(dPb, qb) + dk_state
        dv_ref[...] = _dot_tn(P.astype(BF16), dob) + _dot((kv * kd).astype(BF16), dSb)
        dq_ref[...] = _rot_half(dq_rot, cos, sin, True)
        dk_ref[...] = _rot_half(dk_rot, cos, sin, True) * kscale
        dlam = (jnp.sum(dm_ref[...] * P * dP_raw) + jnp.sum(wq_ref[...] * qv * dq_cross)
                + C * jnp.sum(cd * S * dS) + jnp.sum(wk_ref[...] * kv * dk_state))
        dl = jnp.full(dl_ref.shape, dlam, F32)

        @pl.when(first)
        def _():
            dl_ref[...] = dl

        @pl.when(jnp.logical_not(first))
        def _():
            dl_ref[...] += dl

        ds_scr[...] = cd * dS + _dot_tn((qv * qd).astype(BF16), dob)

    chunk = lambda d, h, s: _ret_chunk(d, ns - 1 - s, ncx, ncc)
    tab = lambda shape: pl.BlockSpec((None, None) + shape, lambda d, h, s: (d, h, 0, 0))
    dtab = lambda shape: pl.BlockSpec((None,) + shape, lambda d, h, s: (d, 0, 0))
    return pl.pallas_call(
        body, grid=(2, H, ns),
        in_specs=[pl.BlockSpec((C, dk), lambda d, h, s: (chunk(d, h, s), h)),
                  pl.BlockSpec((C, dk), lambda d, h, s: (chunk(d, h, s), h)),
                  pl.BlockSpec((C, dv), lambda d, h, s: (chunk(d, h, s), h)),
                  pl.BlockSpec((None, C, dv), lambda d, h, s: (d, chunk(d, h, s), h)),
                  pl.BlockSpec((None, None, None, dk, dv), lambda d, h, s: (d, h, ns - 1 - s, 0, 0)),
                  pl.BlockSpec((C, dk), lambda d, h, s: (chunk(d, h, s), 0)),
                  pl.BlockSpec((C, dk), lambda d, h, s: (chunk(d, h, s), 0)),
                  tab((C, C)), tab((C, dk)), tab((C, dk)), tab((1, dv)),
                  dtab((C, C)), dtab((C, dk)), dtab((C, dk))],
        out_specs=[pl.BlockSpec((None, C, dk), lambda d, h, s: (d, chunk(d, h, s), h)),
                   pl.BlockSpec((None, C, dk), lambda d, h, s: (d, chunk(d, h, s), h)),
                   pl.BlockSpec((None, C, dv), lambda d, h, s: (d, chunk(d, h, s), h)),
                   pl.BlockSpec((None, None, 1, LANES), lambda d, h, s: (d, h, 0, 0))],
        out_shape=[jax.ShapeDtypeStruct((2, T, H * dk), F32), jax.ShapeDtypeStruct((2, T, H * dk), F32),
                   jax.ShapeDtypeStruct((2, T, H * dv), F32), jax.ShapeDtypeStruct((2, H, 1, LANES), F32)],
        scratch_shapes=[pltpu.VMEM((dk, dv), F32)],
        compiler_params=_params("parallel", "parallel", "arbitrary"), name="ret_bwd",
    )(q, k, v, do, states, tabs["cos"], tabs["sin"], tabs["intra"], tabs["qd"], tabs["kd"], tabs["cd"],
      tabs["dmat"], tabs["wq"], tabs["wk"])


def _ret_tables(decay_logit, Nx, Nc, dk, dv):
    C, H = RET_CHUNK, RET_HEADS
    inv = RET_ROPE_BASE ** (-jnp.linspace(0.0, 1.0, dk // 2, dtype=F32))
    ang = jnp.arange(Nx, dtype=F32)[:, None] * inv[None, :]
    cos, sin = jnp.cos(ang), jnp.sin(ang)
    cosf = jnp.concatenate([jnp.concatenate([cos, cos], 1), jnp.ones((Nc, dk), F32)], 0)
    sinf = jnp.concatenate([jnp.concatenate([-sin, sin], 1), jnp.zeros((Nc, dk), F32)], 0)
    lg = jax.nn.log_sigmoid(decay_logit.astype(F32))
    idx = jnp.arange(C, dtype=F32)
    diff = idx[:, None] - idx[None, :]
    dmat = jnp.stack([jnp.maximum(diff, 0.0), jnp.maximum(-diff, 0.0)])
    mask = jnp.stack([diff >= 0, diff <= 0])
    intra = jnp.where(mask[:, None], jnp.exp(lg[:, :, None, None] * dmat[:, None]), 0.0)
    wq = jnp.stack([idx + 1.0, C - idx])
    wk = jnp.stack([C - 1.0 - idx, idx])
    qd = jnp.exp(lg[:, :, None] * wq[:, None, :])
    kd = jnp.exp(lg[:, :, None] * wk[:, None, :])
    cd = jnp.exp(lg * C)
    bc = lambda t, w: jnp.broadcast_to(t[..., None], t.shape + (w,))
    return dict(cos=cosf, sin=sinf, intra=intra, qd=bc(qd, dk), kd=bc(kd, dk),
                cd=jnp.broadcast_to(cd[:, :, None, None], (2, H, 1, dv)),
                dmat=dmat, wq=bc(wq, dk), wk=bc(wk, dk), lg=lg)


def _attn_fwd(q, kn, v, kr, Nx, tq):
    T, H = q.shape[0], MLA_HEADS
    n_xq = Nx // tq

    def body(q_ref, kn_ref, v_ref, kr_ref, o_ref, lse_ref):
        def attend(lo):
            kcat = jnp.concatenate([kn_ref[lo:, :], kr_ref[lo:, :]], axis=1)
            s = _dot_nt(q_ref[...], kcat)
            m = jnp.max(s, axis=-1, keepdims=True)
            p = jnp.exp(s - m)
            l = jnp.sum(p, axis=-1, keepdims=True)
            o_ref[...] = (_dot(p.astype(BF16), v_ref[lo:, :]) / l).astype(o_ref.dtype)
            lse_ref[...] = m + jnp.log(l)

        @pl.when(pl.program_id(1) < n_xq)
        def _():
            attend(0)

        @pl.when(pl.program_id(1) >= n_xq)
        def _():
            attend(Nx)

    return pl.pallas_call(
        body, grid=(H, T // tq),
        in_specs=[pl.BlockSpec((tq, 2 * LANES), lambda h, i: (i, h)),
                  pl.BlockSpec((T, LANES), lambda h, i: (0, h)),
                  pl.BlockSpec((T, LANES), lambda h, i: (0, h)),
                  pl.BlockSpec((T, LANES), lambda h, i: (0, 0))],
        out_specs=[pl.BlockSpec((tq, LANES), lambda h, i: (i, h)),
                   pl.BlockSpec((None, tq, 1), lambda h, i: (h, i, 0))],
        out_shape=[jax.ShapeDtypeStruct((T, H * LANES), BF16), jax.ShapeDtypeStruct((H, T, 1), F32)],
        compiler_params=_params("parallel", "arbitrary"), name="attn_fwd")(q, kn, v, kr)


def _attn_bwd(q, kn, v, kr, do, lse, Nx, tq):
    T, H = q.shape[0], MLA_HEADS
    n_xq, nq = Nx // tq, T // tq

    def body(q_ref, kn_ref, v_ref, kr_ref, do_ref, lse_ref, dq_ref, dkn_ref, dv_ref, dkr_ref, dk_acc, dv_acc):
        h, i = pl.program_id(0), pl.program_id(1)

        @pl.when(i == 0)
        def _():
            dk_acc[...] = jnp.zeros_like(dk_acc)
            dv_acc[...] = jnp.zeros_like(dv_acc)

        def attend(lo):
            kcat = jnp.concatenate([kn_ref[lo:, :], kr_ref[lo:, :]], axis=1)
            qb, dob = q_ref[...], do_ref[...]
            p = jnp.exp(_dot_nt(qb, kcat) - lse_ref[...])
            dp = _dot_nt(dob, v_ref[lo:, :])
            delta = jnp.sum(p * dp, axis=-1, keepdims=True)
            dsb = (p * (dp - delta)).astype(BF16)
            dq_ref[...] = _dot(dsb, kcat)
            dk_acc[lo:, :] += _dot_tn(dsb, qb)
            dv_acc[lo:, :] += _dot_tn(p.astype(BF16), dob)

        @pl.when(i < n_xq)
        def _():
            attend(0)

        @pl.when(i >= n_xq)
        def _():
            attend(Nx)

        @pl.when(i == nq - 1)
        def _():
            dkn_ref[...] = dk_acc[:, :LANES].astype(dkn_ref.dtype)
            dv_ref[...] = dv_acc[...].astype(dv_ref.dtype)

        @pl.when(jnp.logical_and(i == nq - 1, h == 0))
        def _():
            dkr_ref[...] = dk_acc[:, LANES:]

        @pl.when(jnp.logical_and(i == nq - 1, h != 0))
        def _():
            dkr_ref[...] += dk_acc[:, LANES:]

    return pl.pallas_call(
        body, grid=(H, nq),
        in_specs=[pl.BlockSpec((tq, 2 * LANES), lambda h, i: (i, h)),
                  pl.BlockSpec((T, LANES), lambda h, i: (0, h)),
                  pl.BlockSpec((T, LANES), lambda h, i: (0, h)),
                  pl.BlockSpec((T, LANES), lambda h, i: (0, 0)),
                  pl.BlockSpec((tq, LANES), lambda h, i: (i, h)),
                  pl.BlockSpec((None, tq, 1), lambda h, i: (h, i, 0))],
        out_specs=[pl.BlockSpec((tq, 2 * LANES), lambda h, i: (i, h)),
                   pl.BlockSpec((T, LANES), lambda h, i: (0, h)),
                   pl.BlockSpec((T, LANES), lambda h, i: (0, h)),
                   pl.BlockSpec((T, LANES), lambda h, i: (0, 0))],
        out_shape=[jax.ShapeDtypeStruct((T, H * 2 * LANES), F32), jax.ShapeDtypeStruct((T, H * LANES), BF16),
                   jax.ShapeDtypeStruct((T, H * LANES), BF16), jax.ShapeDtypeStruct((T, LANES), F32)],
        scratch_shapes=[pltpu.VMEM((T, 2 * LANES), F32), pltpu.VMEM((T, LANES), F32)],
        compiler_params=_params("arbitrary", "arbitrary"), name="attn_bwd")(q, kn, v, kr, do, lse)


def _mla_tables(Nx, Nc):
    ad = MLA_D_ROPE // 2
    inv = AXIAL_ROPE_BASE ** (-jnp.arange(ad // 2, dtype=F32) * 2.0 / ad)
    t = jnp.arange(Nx)
    rang = (t // GRID_W).astype(F32)[:, None] * inv[None, :]
    cang = (t % GRID_W).astype(F32)[:, None] * inv[None, :]
    rc, rs, cc, cs = jnp.cos(rang), jnp.sin(rang), jnp.cos(cang), jnp.sin(cang)
    pad1, pad0 = jnp.ones((Nx, LANES - MLA_D_ROPE), F32), jnp.zeros((Nx, LANES - MLA_D_ROPE), F32)
    cos = jnp.concatenate([rc, rc, cc, cc, pad1], 1)
    sin = jnp.concatenate([-rs, rs, -cs, cs, pad0], 1)
    return (jnp.concatenate([cos, jnp.ones((Nc, LANES), F32)], 0),
            jnp.concatenate([sin, jnp.zeros((Nc, LANES), F32)], 0))


def _place():
    x, y, c = lax.axis_index("x"), lax.axis_index("y"), lax.axis_index("c")
    return x, y, c


def _all_gather8(v):
    R = v.shape[0]

    def body(v_ref, g_ref, s_ref, send_sems, recv_sems):
        x, y, c = _place()
        me = 4 * x + 2 * y + c
        g_ref[me] = v_ref[...]
        copies = []
        for k in range(1, N_DEV):
            peer = (x ^ (k >> 2), y ^ ((k >> 1) & 1), c ^ (k & 1))
            copies.append(pltpu.make_async_remote_copy(
                src_ref=v_ref, dst_ref=g_ref.at[me], send_sem=send_sems.at[k - 1], recv_sem=recv_sems.at[k - 1],
                device_id=peer, device_id_type=MESH))
        for cp in copies:
            cp.start()
        for cp in copies:
            cp.wait_recv()
        for cp in copies:
            cp.wait_send()
        acc = g_ref[0]
        for k in range(1, N_DEV):
            acc = acc + g_ref[k]
        s_ref[...] = acc

    vm = pl.BlockSpec(memory_space=pltpu.VMEM)
    return pl.pallas_call(
        body, in_specs=[vm], out_specs=[vm, vm],
        out_shape=[jax.ShapeDtypeStruct((N_DEV, R, LANES), F32), jax.ShapeDtypeStruct((R, LANES), F32)],
        scratch_shapes=[pltpu.SemaphoreType.DMA((N_DEV - 1,)), pltpu.SemaphoreType.DMA((N_DEV - 1,))],
        compiler_params=pltpu.CompilerParams(vmem_limit_bytes=V7X_VMEM_LIMIT_BYTES), name="all_gather8")(v)


def _other_chips(x, y):
    return [(1 - x, y), (x, 1 - y), (1 - x, 1 - y)]


def _place_own(w, chip_idx, after):
    R, C = w.shape
    tr = _pick(R, max(16, (2 << 20) // (C * 4)), 16)

    def body(s_ref, w_ref, after_ref, o_ref):
        o_ref[...] = w_ref[...].astype(o_ref.dtype)

    return pl.pallas_call(
        body,
        grid_spec=pltpu.PrefetchScalarGridSpec(
            num_scalar_prefetch=1, grid=(R // tr,),
            in_specs=[pl.BlockSpec((tr, C), lambda i, s: (i, 0)), pl.BlockSpec(memory_space=pl.ANY)],
            out_specs=pl.BlockSpec((None, tr, C), lambda i, s: (s[0], i, 0))),
        out_shape=jax.ShapeDtypeStruct((N_CHIPS, R, C), BF16),
        compiler_params=_params("parallel"), name="place_own")(chip_idx, w, after)


HBM_SPEC = pl.BlockSpec(memory_space=pltpu.HBM)
SEM_SPEC = pl.BlockSpec(memory_space=pltpu.SEMAPHORE)
DATAFLOW_EFFECT = pltpu.SideEffectType.DATAFLOW_SIDE_EFFECTING


def _gather_start(buf, name):
    def body(b_ref, send_sem, recv_sem, b_thru, token):
        x, y, c = _place()
        chip = 2 * x + y
        for ch in _other_chips(x, y):
            pltpu.make_async_remote_copy(src_ref=b_ref.at[chip], dst_ref=b_ref.at[chip], send_sem=send_sem,
                                         recv_sem=recv_sem, device_id=(*ch, c), device_id_type=MESH).start()
        token[...] = jnp.zeros_like(token)

    return pl.pallas_call(
        body, name=name,
        out_shape=(pltpu.SemaphoreType.DMA(()), pltpu.SemaphoreType.DMA(()), pltpu.HBM(buf.shape, buf.dtype),
                   jax.ShapeDtypeStruct((8, LANES), F32)),
        in_specs=(HBM_SPEC,), out_specs=(SEM_SPEC, SEM_SPEC, HBM_SPEC, pl.BlockSpec(memory_space=pltpu.VMEM)),
        input_output_aliases={0: 2},
        compiler_params=pltpu.CompilerParams(has_side_effects=DATAFLOW_EFFECT),
    )(pltpu.with_memory_space_constraint(buf, pltpu.HBM))


def _gather_wait(send_sem, recv_sem, buf_thru, after, name):
    def body(b_ref, send_sem, recv_sem, after_ref, b_out):
        x, y, c = _place()
        three = b_ref.at[pl.ds(0, 3)]
        cp = pltpu.make_async_remote_copy(src_ref=three, dst_ref=three, send_sem=send_sem, recv_sem=recv_sem,
                                          device_id=(x, y, 1 - c), device_id_type=MESH)
        cp.wait_send()
        cp.wait_recv()

    return pl.pallas_call(
        body, name=name, out_shape=pltpu.HBM(buf_thru.shape, buf_thru.dtype),
        in_specs=(HBM_SPEC, SEM_SPEC, SEM_SPEC, pl.BlockSpec(memory_space=pl.ANY)), out_specs=HBM_SPEC,
        input_output_aliases={0: 0},
        compiler_params=pltpu.CompilerParams(has_side_effects=DATAFLOW_EFFECT),
    )(buf_thru, send_sem, recv_sem, after)


def _scatter_start(hb, name):
    def body(h_ref, land_ref, send_sem, recv_sem, h_thru, land_thru, token):
        x, y, c = _place()
        chip = 2 * x + y
        for ch in _other_chips(x, y):
            pltpu.make_async_remote_copy(src_ref=h_ref.at[2 * ch[0] + ch[1]], dst_ref=land_ref.at[chip],
                                         send_sem=send_sem, recv_sem=recv_sem, device_id=(*ch, c),
                                         device_id_type=MESH).start()
        token[...] = jnp.zeros_like(token)

    return pl.pallas_call(
        body, name=name,
        out_shape=(pltpu.SemaphoreType.DMA(()), pltpu.SemaphoreType.DMA(()), pltpu.HBM(hb.shape, hb.dtype),
                   pltpu.HBM(hb.shape, hb.dtype), jax.ShapeDtypeStruct((8, LANES), F32)),
        in_specs=(HBM_SPEC, HBM_SPEC),
        out_specs=(SEM_SPEC, SEM_SPEC, HBM_SPEC, HBM_SPEC, pl.BlockSpec(memory_space=pltpu.VMEM)),
        input_output_aliases={0: 2, 1: 3},
        compiler_params=pltpu.CompilerParams(has_side_effects=DATAFLOW_EFFECT),
    )(pltpu.with_memory_space_constraint(hb, pltpu.HBM),
      pltpu.with_memory_space_constraint(lax.empty(hb.shape, hb.dtype), pltpu.HBM))


def _scatter_wait(send_sem, recv_sem, h_thru, land_thru, after, name):
    def body(h_ref, land_ref, send_sem, recv_sem, after_ref, h_dead, got_ref):
        x, y, c = _place()
        cp = pltpu.make_async_remote_copy(src_ref=h_ref.at[pl.ds(0, 3)], dst_ref=land_ref.at[pl.ds(0, 3)],
                                          send_sem=send_sem, recv_sem=recv_sem, device_id=(x, y, 1 - c),
                                          device_id_type=MESH)
        cp.wait_send()
        cp.wait_recv()

    return pl.pallas_call(
        body, name=name,
        out_shape=(pltpu.HBM(h_thru.shape, h_thru.dtype), pltpu.HBM(land_thru.shape, land_thru.dtype)),
        in_specs=(HBM_SPEC, HBM_SPEC, SEM_SPEC, SEM_SPEC, pl.BlockSpec(memory_space=pl.ANY)),
        out_specs=(HBM_SPEC, HBM_SPEC), input_output_aliases={0: 0, 1: 1},
        compiler_params=pltpu.CompilerParams(has_side_effects=DATAFLOW_EFFECT),
    )(h_thru, land_thru, send_sem, recv_sem, after)[1]


def _swap_start(g, name):
    J, R, C = g.shape
    half = R // 2

    def body(g_ref, land_ref, send_sem, recv_sem, g_thru, land_thru, token):
        x, y, c = _place()
        pltpu.make_async_remote_copy(src_ref=g_ref.at[:, pl.ds((1 - c) * half, half), :], dst_ref=land_ref,
                                     send_sem=send_sem, recv_sem=recv_sem, device_id=(x, y, 1 - c),
                                     device_id_type=MESH).start()
        token[...] = jnp.zeros_like(token)

    return pl.pallas_call(
        body, name=name,
        out_shape=(pltpu.SemaphoreType.DMA(()), pltpu.SemaphoreType.DMA(()), pltpu.HBM(g.shape, g.dtype),
                   pltpu.HBM((J, half, C), g.dtype), jax.ShapeDtypeStruct((8, LANES), F32)),
        in_specs=(HBM_SPEC, HBM_SPEC),
        out_specs=(SEM_SPEC, SEM_SPEC, HBM_SPEC, HBM_SPEC, pl.BlockSpec(memory_space=pltpu.VMEM)),
        input_output_aliases={0: 2, 1: 3},
        compiler_params=pltpu.CompilerParams(has_side_effects=DATAFLOW_EFFECT),
    )(pltpu.with_memory_space_constraint(g, pltpu.HBM),
      pltpu.with_memory_space_constraint(lax.empty((J, half, C), g.dtype), pltpu.HBM))


def _swap_wait(send_sem, recv_sem, g_thru, land_thru, after, name):
    half = land_thru.shape[1]

    def body(g_ref, land_ref, send_sem, recv_sem, after_ref, g_out, got_ref):
        x, y, c = _place()
        cp = pltpu.make_async_remote_copy(src_ref=g_ref.at[:, pl.ds((1 - c) * half, half), :], dst_ref=land_ref,
                                          send_sem=send_sem, recv_sem=recv_sem, device_id=(x, y, 1 - c),
                                          device_id_type=MESH)
        cp.wait_send()
        cp.wait_recv()

    return pl.pallas_call(
        body, name=name,
        out_shape=(pltpu.HBM(g_thru.shape, g_thru.dtype), pltpu.HBM(land_thru.shape, land_thru.dtype)),
        in_specs=(HBM_SPEC, HBM_SPEC, SEM_SPEC, SEM_SPEC, pl.BlockSpec(memory_space=pl.ANY)),
        out_specs=(HBM_SPEC, HBM_SPEC), input_output_aliases={0: 0, 1: 1},
        compiler_params=pltpu.CompilerParams(has_side_effects=DATAFLOW_EFFECT),
    )(g_thru, land_thru, send_sem, recv_sem, after)


def _add_half(g, r, place_idx):
    J, R, C = g.shape
    half = R // 2
    tr = _pick(half, max(16, (2 << 20) // (C * 4)), 16)
    nb = half // tr

    def body(s_ref, g_ref, r_ref, o_ref, ob_ref):
        acc = g_ref[...] + r_ref[...]
        o_ref[...] = acc
        ob_ref[...] = acc.astype(ob_ref.dtype)

    spec = pl.BlockSpec((None, tr, C), lambda j, i, s: (j, i, 0))
    return pl.pallas_call(
        body,
        grid_spec=pltpu.PrefetchScalarGridSpec(
            num_scalar_prefetch=1, grid=(J, nb),
            in_specs=[pl.BlockSpec((None, tr, C), lambda j, i, s: (j, s[0] * nb + i, 0)), spec],
            out_specs=[spec, spec]),
        out_shape=[jax.ShapeDtypeStruct((J, half, C), F32), jax.ShapeDtypeStruct((J, half, C), BF16)],
        compiler_params=_params("parallel", "parallel"), name="add_half")(place_idx[4], g, r)


def _add_chips(h, p, place_idx):
    J, R, C = h.shape
    tr = _pick(R, max(16, (2 << 20) // (C * 4)), 16)
    nb = R // tr

    def body(s0, s1, s2, s3, s4, h_ref, p0_ref, p1_ref, p2_ref, o_ref):
        o_ref[...] = ((h_ref[...] + p0_ref[...].astype(F32)) + p1_ref[...].astype(F32)) + p2_ref[...].astype(F32)

    slot = lambda k: pl.BlockSpec((None, tr, C), lambda i, *s: (s[k][0], i, 0))
    return pl.pallas_call(
        body,
        grid_spec=pltpu.PrefetchScalarGridSpec(
            num_scalar_prefetch=5, grid=(nb,),
            in_specs=[slot(0), slot(1), slot(2), slot(3)],
            out_specs=pl.BlockSpec((tr, C), lambda i, *s: (s[4][0] * nb + i, 0))),
        out_shape=jax.ShapeDtypeStruct((2 * R, C), F32),
        compiler_params=_params("parallel"), name="add_chips")(*place_idx, h, p, p, p)


def _join_halves(s):
    R, C = s.shape
    half = R // 2

    def body(in_ref, o_ref, send_sem, recv_sem):
        x, y, c = _place()
        mine = o_ref.at[pl.ds(c * half, half)]
        cp = pltpu.make_async_remote_copy(src_ref=mine, dst_ref=mine, send_sem=send_sem, recv_sem=recv_sem,
                                          device_id=(x, y, 1 - c), device_id_type=MESH)
        cp.start()
        cp.wait()

    hbm = pl.BlockSpec(memory_space=pl.ANY)
    return pl.pallas_call(
        body, in_specs=[hbm], out_specs=hbm, out_shape=jax.ShapeDtypeStruct((R, C), s.dtype),
        input_output_aliases={0: 0},
        scratch_shapes=[pltpu.SemaphoreType.DMA, pltpu.SemaphoreType.DMA], name="join_halves")(s)


def _reduce_scatter_start(swap_state, place_idx, after, name):
    g, r = _swap_wait(*swap_state, after, name + "_swap_wait")
    h, hb = _add_half(g, r, place_idx)
    send_sem, recv_sem, hb_thru, land_thru, token = _scatter_start(hb, name + "_start")
    return (h, send_sem, recv_sem, hb_thru, land_thru), token


def _reduce_scatter_finish(state, place_idx, after, name):
    h, send_sem, recv_sem, hb_thru, land_thru = state
    p = _scatter_wait(send_sem, recv_sem, hb_thru, land_thru, after, name + "_wait")
    return _join_halves(_add_chips(h, p, place_idx))


def _pack(parts):
    flat = jnp.concatenate([p.reshape(-1).astype(F32) for p in parts])
    n = flat.shape[0]
    rows = -(-n // (8 * LANES)) * 8
    return jnp.pad(flat, (0, rows * LANES - n)).reshape(rows, LANES)


def _unpack(buf, shapes):
    flat = buf.reshape(buf.shape[:-2] + (-1,))
    out, off = [], 0
    for s in shapes:
        n = math.prod(s)
        out.append(flat[..., off:off + n].reshape(buf.shape[:-2] + tuple(s)))
        off += n
    return out


def _mod_table(mod_x, mod_c):
    return jnp.stack([mod_x.reshape(6, 1, -1), mod_c.reshape(6, 1, -1)])


def kernel(x, c, ctx, c_ctx, ada_w, ada_b, ln_g, ln_b, ret_w_qkv, ret_w_g, ret_decay_logit, ret_w_o, mla_w_dq, mla_g_q, mla_w_uq, mla_w_dkv, mla_g_kv, mla_w_ukv, mla_w_o, ffn_w_in, ffn_w_out, loss_target, m_c_ctx, m_ada_w, m_ada_b, m_ln_g, m_ln_b, m_ret_w_qkv, m_ret_w_g, m_ret_decay_logit, m_ret_w_o, m_mla_w_dq, m_mla_g_q, m_mla_w_uq, m_mla_w_dkv, m_mla_g_kv, m_mla_w_ukv, m_mla_w_o, m_ffn_w_in, m_ffn_w_out, v_c_ctx, v_ada_w, v_ada_b, v_ln_g, v_ln_b, v_ret_w_qkv, v_ret_w_g, v_ret_decay_logit, v_ret_w_o, v_mla_w_dq, v_mla_g_q, v_mla_w_uq, v_mla_w_dkv, v_mla_g_kv, v_mla_w_ukv, v_mla_w_o, v_ffn_w_in, v_ffn_w_out):
    Nx, D = x.shape[1], x.shape[2]
    Nc = ctx.shape[1]
    T = Nx + Nc
    tr = 256 if (Nx % 256 == 0 and Nc % 256 == 0) else 128
    n_xt = Nx // tr
    C = RET_CHUNK
    ncx, ncc = Nx // C, Nc // C
    H = RET_HEADS
    dk, dv = D // H, 2 * D // H
    L = MLA_LORA
    HM = MLA_HEADS
    D6 = 6 * D
    n6 = D6 // N_CHIPS
    Dq = D // N_CHIPS
    xi, yi, ci = lax.axis_index("x"), lax.axis_index("y"), lax.axis_index("c")
    chip = 2 * xi + yi
    dev = 4 * xi + 2 * yi + ci
    as_index = lambda s: jnp.reshape(s, (1,)).astype(jnp.int32)
    place_idx = (as_index(chip), as_index(2 * (1 - xi) + yi), as_index(2 * xi + 1 - yi),
                 as_index(2 * (1 - xi) + 1 - yi), as_index(ci))
    chip_idx = place_idx[0]
    tile = dict(T=T, tr=tr, n_xt=n_xt)

    shapes1 = [(D,), (DEPTH, 2, Dq), (DEPTH, 2, Dq), (2, L // N_CHIPS), (2, L // N_CHIPS)]
    g1, _ = _all_gather8(_pack([c[0], ln_g, ln_b, mla_g_q, mla_g_kv]))
    c_all, lng_s, lnb_s, gq_s, gkv_s = _unpack(g1, shapes1)
    by_chip = lambda t: jnp.moveaxis(t[0::2], 0, -2).reshape(t.shape[1:-1] + (-1,))
    ln_g_full, ln_b_full = by_chip(lng_s), by_chip(lnb_s)
    gq_full, gkv_full = by_chip(gq_s), by_chip(gkv_s)

    cond = jnp.concatenate([c_all, c_ctx[None]], 0)
    silu_cond = cond * jax.nn.sigmoid(cond)
    s16 = jnp.pad(silu_cond, ((0, 16 - (N_DEV + 1)), (0, 0))).astype(BF16)
    mods = []
    for i in range(DEPTH):
        bias = lax.dynamic_slice_in_dim(ada_b[i], chip * n6, n6)
        mods.append(_mm_nn(s16, ada_w[i][None], name="ada_fwd")[:N_DEV + 1] + bias[None])
    g2, _ = _all_gather8(_pack([jnp.stack(mods)]))
    (mod_all,) = _unpack(g2, [(DEPTH, N_DEV + 1, n6)])
    mod_all = jnp.moveaxis(mod_all[0::2], 0, -2).reshape(DEPTH, N_DEV + 1, D6)
    mod_tabs = [_mod_table(lax.dynamic_index_in_dim(mod_all[i], dev, 0, False), mod_all[i, N_DEV])
                for i in range(DEPTH)]
    lnps = [[jnp.stack([ln_g_full[i, s], ln_b_full[i, s]])[:, None, :] for s in range(2)] for i in range(DEPTH)]

    def mla_pack(j):
        return jnp.concatenate([mla_w_dq[j], mla_w_uq[j], jnp.pad(mla_w_dkv[j], ((0, 0), (0, 64))),
                                mla_w_ukv[j], mla_w_o[j]], axis=1)

    mla_cols = [L, 3 * L // 2, L + LANES, 2 * L, D]

    def mla_unpack_weights(buf):
        offs = [0]
        for wdt in mla_cols:
            offs.append(offs[-1] + wdt)
        dq_, uq_, dkv_, ukv_, wo_ = [buf[:, :, offs[k]:offs[k + 1]] for k in range(5)]
        w_dq = dq_.reshape(D, L)
        w_dkv = dkv_.reshape(D, L + LANES)
        w_lora = jnp.concatenate([w_dq, w_dkv], axis=1)
        w_uq = jnp.moveaxis(uq_, 0, 1).reshape(L, HM, MLA_D_NOPE + MLA_D_ROPE)
        wq_cat = jnp.pad(w_uq, ((0, 0), (0, 0), (0, 2 * LANES - MLA_D_NOPE - MLA_D_ROPE))).reshape(L, HM * 2 * LANES)
        w_ukv = jnp.moveaxis(ukv_, 0, 1).reshape(L, HM, MLA_D_NOPE + MLA_D_V)
        w_kn = w_ukv[:, :, :MLA_D_NOPE].reshape(L, HM * MLA_D_NOPE)
        w_v = w_ukv[:, :, MLA_D_NOPE:].reshape(L, HM * MLA_D_V)
        w_o = wo_.reshape(HM * MLA_D_V, D)
        return dict(lora=w_lora[None], q=wq_cat[None], kn=w_kn[None], v=w_v[None], o=w_o[None])

    def mla_pack_grads(g):
        d_lora = g["lora"][0]
        d_dq = d_lora[:, :L].reshape(N_CHIPS, Dq, L)
        d_dkv = d_lora[:, L:].reshape(N_CHIPS, Dq, L + LANES)
        d_uq = g["q"][0].reshape(L, HM, 2 * LANES)[:, :, :MLA_D_NOPE + MLA_D_ROPE]
        d_uq = jnp.moveaxis(d_uq.reshape(L, N_CHIPS, -1), 1, 0)
        d_ukv = jnp.concatenate([g["kn"][0].reshape(L, HM, MLA_D_NOPE), g["v"][0].reshape(L, HM, MLA_D_V)], axis=2)
        d_ukv = jnp.moveaxis(d_ukv.reshape(L, N_CHIPS, -1), 1, 0)
        d_o = g["o"][0].reshape(N_CHIPS, L, D)
        return jnp.concatenate([d_dq, d_uq, d_dkv, d_ukv, d_o], axis=2)

    assert Dq == L, "the packed MLA buffer assumes D_MODEL / 4 == 512 rows per shard"

    def layer_shards(i):
        j = i // 2
        own = dict(qkv=ret_w_qkv[j], g=ret_w_g[j], o=ret_w_o[j]) if i % 2 == 0 else dict(mla=mla_pack(j))
        own.update(w_in=ffn_w_in[i], w_out=ffn_w_out[i])
        return own

    def arrived(i, n, after, lw):
        full = _gather_wait(*in_flight[i][n], after=after, name=f"gather_wait_{i}_{n}")
        if n == "qkv":
            lw.update(q=full[0:1], k=full[1:2], v=full[2:4])
        elif n == "g":
            lw.update(gf=full[0:2], gb=full[2:4])
        elif n == "o":
            lw.update(o=full.reshape(1, 2 * D, D))
        elif n == "mla":
            lw.update(mla_unpack_weights(full))
        elif n == "w_in":
            lw.update(a=full[0:2], b=full[2:4])
        else:
            lw.update(out=full.reshape(1, -1, D))

    W = [None] * DEPTH
    in_flight, start_tok = [None] * DEPTH, jnp.zeros((8, LANES), F32)
    for i in range(DEPTH):
        in_flight[i] = {}
        for n, w_l in layer_shards(i).items():
            send_sem, recv_sem, thru, tok = _gather_start(_place_own(w_l, chip_idx, start_tok), name=f"gather_start_{i}_{n}")
            in_flight[i][n] = (send_sem, recv_sem, thru)
            start_tok = start_tok + tok
    start_tok = start_tok[0, 0]

    ret_tabs = [_ret_tables(ret_decay_logit[j], Nx, Nc, dk, dv) for j in range(2)]
    mla_cos, mla_sin = _mla_tables(Nx, Nc)
    gqkv = [jnp.stack([gq_full[j], gkv_full[j]])[:, None, :] for j in range(2)]
    tq_f, tq_b = tr, 128

    h = jnp.concatenate([x[0], ctx[0]], axis=0) + start_tok
    u = _modulate(h, mod_tabs[0], **tile)
    saved = []
    for i in range(DEPTH):
        W[i] = lw = {}
        j, mod, sv = i // 2, mod_tabs[i], {}
        sv.update(h=h, u=u)
        if i % 2 == 0:
            arrived(i, "qkv", u, lw)
            q = _mm_nn(u, lw["q"], name="ret_q")
            k = _mm_nn(u, lw["k"], name="ret_k")
            v = _mm_nn(u, lw["v"], name="ret_v")
            arrived(i, "g", v, lw)
            gf = _mm_nn(u, lw["gf"], name="ret_gf")
            gb = _mm_nn(u, lw["gb"], name="ret_gb")
            o, states = _ret_fwd(q, k, v, ret_tabs[j], ncx, ncc)
            yg = _gate_fwd(gf, gb, o, dv=dv, **tile)
            arrived(i, "o", yg, lw)
            y = _mm_nn(yg, lw["o"], name="ret_o")
            sv.update(q=q, k=k, v=v, gf=gf, gb=gb, o=o, states=states, yg=yg)
        else:
            arrived(i, "mla", u, lw)
            lora = _mm_nn(u, lw["lora"], name="mla_lora")
            cqn, ckvn, kr = _lora_fwd(lora, gqkv[j], mla_cos, mla_sin, **tile)
            qcat = _mm_nn(cqn, lw["q"], name="mla_q")
            qrot = _qrope(qcat, mla_cos, mla_sin, False, **tile)
            kn = _mm_nn(ckvn, lw["kn"], out_dtype=BF16, name="mla_kn")
            vv = _mm_nn(ckvn, lw["v"], out_dtype=BF16, name="mla_v")
            att, lse = _attn_fwd(qrot, kn, vv, kr, Nx, tq_f)
            y = _mm_nn(att, lw["o"], name="mla_o")
            sv.update(lora=lora, cqn=cqn, ckvn=ckvn, kr=kr, qrot=qrot, kn=kn, vv=vv, att=att, lse=lse)
        h1, u2 = _ln_fwd(h, y, mod, lnps[i][0], 2, mod, (3, 4), **tile)
        arrived(i, "w_in", u2, lw)
        a = _mm_nn(u2, lw["a"], name="ffn_a")
        b = _mm_nn(u2, lw["b"], name="ffn_b")
        act = _swiglu_fwd(a, b, **tile)
        arrived(i, "w_out", act, lw)
        f = _mm_nn(act, lw["out"], name="ffn_out")
        last = i == DEPTH - 1
        h2, u_next = _ln_fwd(h1, f, mod, lnps[i][1], 5, None if last else mod_tabs[i + 1], (0, 1), **tile)
        sv.update(y=y, h1=h1, u2=u2, a=a, b=b, act=act, f=f)
        saved.append(sv)
        h, u = h2, u_next

    dh, err_cols = _loss_grad(h, loss_target[0], **tile)
    loss = lax.psum(0.5 * jnp.sum(err_cols) / D, ("x", "y", "c"))

    pending, swapping, order_tok = {}, [], [jnp.zeros((), F32)]

    def rs_advance(after):
        name, l, swap_state = swapping.pop(0)
        state, tok = _reduce_scatter_start(swap_state, place_idx, after, f"rs_{name}_{l}")
        pending[(name, l)] = state
        order_tok[0] = order_tok[0] + tok[0, 0]

    def rs_start(name, l, g):
        *swap_state, tok = _swap_start(g.reshape(N_CHIPS, -1, g.shape[-1]), f"rs_{name}_{l}_swap_start")
        if swapping:
            rs_advance(tok)
        swapping.append((name, l, swap_state))
        order_tok[0] = order_tok[0] + tok[0, 0]

    d_mods, d_lng, d_lnb = [None] * DEPTH, [None] * DEPTH, [None] * DEPTH
    d_gq, d_gkv, d_lam = [None] * 2, [None] * 2, [None] * 2
    for i in reversed(range(DEPTH)):
        j, lw, sv = i // 2, W[i], saved[i]
        mod = mod_tabs[i] + order_tok[0]
        dh1_res, df, dg_f, dlg1, dlb1 = _ln_bwd(sv["h1"], sv["f"], dh, mod, lnps[i][1], 5, **tile)
        dact = _mm_nt(df, lw["out"], name="ffn_out_nt")
        rs_start("ffn_w_out", i, _mm_tn(sv["act"], df, 1, name="ffn_out_tn").reshape(N_CHIPS, -1, D))
        da, db = _swiglu_bwd(sv["a"], sv["b"], dact, **tile)
        du2 = _mm_nt(db, lw["b"], add=_mm_nt(da, lw["a"], name="ffn_a_nt"), name="ffn_b_nt")
        g_in = _mm_tn(sv["u2"], da, 2, name="ffn_a_tn", slots=N_CHIPS)
        rs_start("ffn_w_in", i, _mm_tn(sv["u2"], db, 2, name="ffn_b_tn", slots=N_CHIPS, slot0=2, into=g_in))
        mod = mod_tabs[i] + order_tok[0]
        dh1, dsc_f, dsh_f = _mod_bwd(dh1_res, du2, sv["h1"], mod, 4, **tile)
        dh_res, dy, dg_a, dlg0, dlb0 = _ln_bwd(sv["h"], sv["y"], dh1, mod, lnps[i][0], 2, **tile)
        uu = sv["u"]
        if i % 2 == 0:
            dyg = _mm_nt(dy, lw["o"], name="ret_o_nt")
            rs_start("ret_w_o", j, _mm_tn(sv["yg"], dy, 1, name="ret_o_tn").reshape(N_CHIPS, -1, D))
            dgf, dgb, do_f, do_b = _gate_bwd(sv["gf"], sv["gb"], sv["o"], dyg, dv=dv, **tile)
            dq2, dk2, dv2, dlam = _ret_bwd(sv["q"], sv["k"], sv["v"], jnp.stack([do_f, do_b]), sv["states"],
                                           ret_tabs[j], ncx, ncc)
            dq, dkk, dvv = _add_dirs(dq2, dk2, dv2, **tile)
            du = _mm_nt(dq, lw["q"], name="ret_q_nt")
            du = _mm_nt(dkk, lw["k"], add=du, name="ret_k_nt")
            du = _mm_nt(dvv, lw["v"], add=du, name="ret_v_nt")
            du = _mm_nt(dgf, lw["gf"], add=du, name="ret_gf_nt")
            du = _mm_nt(dgb, lw["gb"], add=du, name="ret_gb_nt")
            g_qkv = _mm_tn(uu, dq, 1, name="ret_q_tn", slots=N_CHIPS)
            g_qkv = _mm_tn(uu, dkk, 1, name="ret_k_tn", slots=N_CHIPS, slot0=1, into=g_qkv)
            rs_start("ret_w_qkv", j, _mm_tn(uu, dvv, 2, name="ret_v_tn", slots=N_CHIPS, slot0=2, into=g_qkv))
            g_g = _mm_tn(uu, dgf, 2, name="ret_gf_tn", slots=N_CHIPS)
            rs_start("ret_w_g", j, _mm_tn(uu, dgb, 2, name="ret_gb_tn", slots=N_CHIPS, slot0=2, into=g_g))
            d_lam[j] = dlam[:, :, 0, 0]
        else:
            datt = _mm_nt(dy, lw["o"], name="mla_o_nt").astype(BF16)
            gm = dict(o=_mm_tn(sv["att"], dy, 1, name="mla_o_tn"))
            dqcat, dkn, dvv, dkr = _attn_bwd(sv["qrot"], sv["kn"], sv["vv"], sv["kr"], datt, sv["lse"], Nx, tq_b)
            dqraw = _qrope(dqcat, mla_cos, mla_sin, True, **tile)
            dcqn = _mm_nt(dqraw, lw["q"], name="mla_q_nt")
            gm["q"] = _mm_tn(sv["cqn"], dqraw, 1, name="mla_q_tn")
            dckvn = _mm_nt(dvv, lw["v"], add=_mm_nt(dkn, lw["kn"], name="mla_kn_nt"), name="mla_v_nt")
            gm["kn"] = _mm_tn(sv["ckvn"], dkn, 1, name="mla_kn_tn")
            gm["v"] = _mm_tn(sv["ckvn"], dvv, 1, name="mla_v_tn")
            dlora, dgq, dgkv = _lora_bwd(sv["lora"], dcqn, dckvn, dkr, gqkv[j], mla_cos, mla_sin, **tile)
            du = _mm_nt(dlora, lw["lora"], name="mla_lora_nt")
            gm["lora"] = _mm_tn(uu, dlora, 1, name="mla_lora_tn")
            rs_start("mla", j, mla_pack_grads(gm))
            d_gq[j], d_gkv[j] = dgq[0], dgkv[0]
        mod = mod_tabs[i] + order_tok[0]
        dh, dsc_a, dsh_a = _mod_bwd(dh_res, du, sv["h"], mod, 1, **tile)
        d_mods[i] = jnp.concatenate([dsh_a, dsc_a, dg_a, dsh_f, dsc_f, dg_f], axis=2)[:, 0, :]
        d_lng[i] = jnp.concatenate([dlg0, dlg1], 0)
        d_lnb[i] = jnp.concatenate([dlb0, dlb1], 0)

    rs_advance(dh)
    grad_x = dh[:Nx][None]

    d_mods = jnp.stack(d_mods)
    dlogit = jnp.stack([d_lam[j] * jax.nn.sigmoid(-ret_decay_logit[j]) for j in range(2)])
    shapes3 = [(DEPTH, D6), (DEPTH, D6), (DEPTH, 2, D), (DEPTH, 2, D), (2, L), (2, L), (2, 2, H)]
    g3, s3 = _all_gather8(_pack([d_mods[:, 0], d_mods[:, 1], jnp.stack(d_lng), jnp.stack(d_lnb),
                                 jnp.stack(d_gq), jnp.stack(d_gkv), dlogit]))
    dmod_x_all = _unpack(g3, shapes3)[0]
    dmod_x_sum, dmod_c_sum, g_lng, g_lnb, g_gq, g_gkv, g_decay = _unpack(s3, shapes3)
    grad_ada_b = dmod_x_sum + dmod_c_sum
    dmod9 = jnp.concatenate([jnp.moveaxis(dmod_x_all, 0, 1), dmod_c_sum[:, None]], axis=1)
    dmod16 = jnp.pad(lax.dynamic_slice_in_dim(dmod9, chip * n6, n6, axis=2), ((0, 0), (0, 16 - (N_DEV + 1)), (0, 0)))
    dmod16 = dmod16.astype(BF16)
    grad_ada_w = jnp.stack([_mm_tn(s16, dmod16[i], 1, name="ada_tn")[0] for i in range(DEPTH)])
    dsilu = _mm_nt(jnp.moveaxis(dmod16, 0, 1).reshape(16, DEPTH * n6), ada_w, name="ada_nt")
    _, s4 = _all_gather8(_pack([dsilu[N_DEV]]))
    sg = jax.nn.sigmoid(c_ctx)
    grad_c_ctx = (0.5 * s4.reshape(-1)[:D]) * (sg * (1.0 + c_ctx * (1.0 - sg)))

    my_cols = lambda t, n: lax.dynamic_slice_in_dim(t, chip * n, n, axis=t.ndim - 1)
    grad_ln_g, grad_ln_b = my_cols(g_lng, Dq), my_cols(g_lnb, Dq)
    grad_gq, grad_gkv = my_cols(g_gq, L // N_CHIPS), my_cols(g_gkv, L // N_CHIPS)

    def rs_done(name, l):
        return _reduce_scatter_finish(pending[(name, l)], place_idx, dh, f"rs_{name}_{l}")

    def rs(name, n_layers, shard_shape):
        return jnp.stack([rs_done(name, l).reshape(shard_shape) for l in range(n_layers)])

    grad_ret_w_qkv = rs("ret_w_qkv", 2, ret_w_qkv.shape[1:])
    grad_ret_w_g = rs("ret_w_g", 2, ret_w_g.shape[1:])
    grad_ret_w_o = rs("ret_w_o", 2, ret_w_o.shape[1:])
    grad_ffn_w_in = rs("ffn_w_in", DEPTH, ffn_w_in.shape[1:])
    grad_ffn_w_out = rs("ffn_w_out", DEPTH, ffn_w_out.shape[1:])
    mla_red = [rs_done("mla", l) for l in range(2)]
    offs = [0]
    for wdt in mla_cols:
        offs.append(offs[-1] + wdt)
    mla_parts = [jnp.stack([mla_red[l][:, offs[k]:offs[k + 1]] for l in range(2)]) for k in range(5)]
    grad_mla_w_dq, grad_mla_w_uq, grad_mla_w_ukv, grad_mla_w_o = mla_parts[0], mla_parts[1], mla_parts[3], mla_parts[4]
    grad_mla_w_dkv = mla_parts[2][:, :, :L + MLA_D_ROPE]

    grads = [grad_c_ctx, grad_ada_w, grad_ada_b, grad_ln_g, grad_ln_b, grad_ret_w_qkv, grad_ret_w_g, g_decay,
             grad_ret_w_o, grad_mla_w_dq, grad_gq, grad_mla_w_uq, grad_mla_w_dkv, grad_gkv, grad_mla_w_ukv,
             grad_mla_w_o, grad_ffn_w_in, grad_ffn_w_out]
    weights = [c_ctx, ada_w, ada_b, ln_g, ln_b, ret_w_qkv, ret_w_g, ret_decay_logit, ret_w_o, mla_w_dq, mla_g_q,
               mla_w_uq, mla_w_dkv, mla_g_kv, mla_w_ukv, mla_w_o, ffn_w_in, ffn_w_out]
    ms = [m_c_ctx, m_ada_w, m_ada_b, m_ln_g, m_ln_b, m_ret_w_qkv, m_ret_w_g, m_ret_decay_logit, m_ret_w_o,
          m_mla_w_dq, m_mla_g_q, m_mla_w_uq, m_mla_w_dkv, m_mla_g_kv, m_mla_w_ukv, m_mla_w_o, m_ffn_w_in, m_ffn_w_out]
    vs = [v_c_ctx, v_ada_w, v_ada_b, v_ln_g, v_ln_b, v_ret_w_qkv, v_ret_w_g, v_ret_decay_logit, v_ret_w_o,
          v_mla_w_dq, v_mla_g_q, v_mla_w_uq, v_mla_w_dkv, v_mla_g_kv, v_mla_w_ukv, v_mla_w_o, v_ffn_w_in, v_ffn_w_out]
    upd = [_adamw(w_, g_, m_, v_) for w_, g_, m_, v_ in zip(weights, grads, ms, vs)]
    return (loss, grad_x, *grads, *[u_[0] for u_ in upd], *[u_[1] for u_ in upd], *[u_[2] for u_ in upd])
```

```python
import functools
import math

import jax
import jax.numpy as jnp
from jax import lax
from jax.experimental import pallas as pl
from jax.experimental.pallas import tpu as pltpu

F32, BF16 = jnp.float32, jnp.bfloat16
MESH = pl.DeviceIdType.MESH
V7X_VMEM_LIMIT_BYTES = 56 * 1024 * 1024
LANES = 128

GRID_W = 64
RET_HEADS = 8
RET_CHUNK = 128
RET_HEAD_GROUP = 4
RET_ROPE_BASE = 10000.0
GN_EPS = 1e-6
MLA_HEADS = 16
MLA_LORA = 512
MLA_D_NOPE = 128
MLA_D_ROPE = 64
MLA_D_V = 128
MLA_SCALE = (MLA_D_NOPE + MLA_D_ROPE) ** -0.5
AXIAL_ROPE_BASE = 10000.0
RMS_EPS = 1e-6
DEPTH = 4
DEEPNORM_ALPHA = (2 * DEPTH) ** 0.25
LN_EPS = 1e-5
ADAM_LR, ADAM_B1, ADAM_B2, ADAM_EPS, ADAM_WD, ADAM_STEP = 0.001, 0.9, 0.999, 1e-08, 0.01, 10
N_CHIPS = 4
N_DEV = 8


def _pick(dim, target, mult):
    best = None
    for d in range(mult, min(dim, target) + 1, mult):
        if dim % d == 0:
            best = d
    return dim if best is None else best


def _params(*sem):
    return pltpu.CompilerParams(dimension_semantics=sem, vmem_limit_bytes=V7X_VMEM_LIMIT_BYTES)


def _sigmoid(x):
    return 1.0 / (1.0 + jnp.exp(-x))


def _accumulate(step, nsteps, acc, part, write):
    if nsteps == 1:
        write(part)
        return

    @pl.when(step == 0)
    def _():
        acc[...] = part

    @pl.when(jnp.logical_and(step > 0, step < nsteps - 1))
    def _():
        acc[...] += part

    @pl.when(step == nsteps - 1)
    def _():
        write(acc[...] + part)


def _mm_nn(a, w, out_dtype=F32, name="mm_nn"):
    M, K = a.shape
    J, K2, n = w.shape
    assert K == K2
    tn = _pick(n, 1408, LANES)
    tk = _pick(K, 2048, LANES)
    tm = _pick(M, max(16, (6 << 20) // (tn * 4)), 16)
    npj, nk = n // tn, K // tk

    def body(a_ref, w_ref, o_ref, acc):
        part = jnp.dot(a_ref[...].astype(BF16), w_ref[...].astype(BF16), preferred_element_type=F32)

        def write(total):
            o_ref[...] = total.astype(o_ref.dtype)

        _accumulate(pl.program_id(2), nk, acc, part, write)

    return pl.pallas_call(
        body, grid=(M // tm, J * npj, nk),
        in_specs=[pl.BlockSpec((tm, tk), lambda i, j, k: (i, k)),
                  pl.BlockSpec((None, tk, tn), lambda i, j, k: (j // npj, k, j % npj))],
        out_specs=pl.BlockSpec((tm, tn), lambda i, j, k: (i, j)),
        out_shape=jax.ShapeDtypeStruct((M, J * n), out_dtype),
        scratch_shapes=[pltpu.VMEM((tm, tn), F32)],
        compiler_params=_params("parallel", "parallel", "arbitrary"), name=name)(a, w)


def _mm_nt(dy, w, add=None, name="mm_nt"):
    M, N = dy.shape
    J, K, n = w.shape
    assert N == J * n
    tko = _pick(K, 1408, LANES)
    tc = _pick(n, 2048, LANES)
    tm = _pick(M, max(16, (6200 << 10) // (tko * 4)), 16)
    npj = n // tc
    nc = J * npj
    has_add = add is not None

    def body(*refs):
        if has_add:
            dy_ref, w_ref, add_ref, o_ref, acc = refs
        else:
            dy_ref, w_ref, o_ref, acc = refs
        part = lax.dot_general(dy_ref[...].astype(BF16), w_ref[...].astype(BF16),
                               (((1,), (1,)), ((), ())), preferred_element_type=F32)

        def write(total):
            o_ref[...] = total + add_ref[...] if has_add else total

        _accumulate(pl.program_id(2), nc, acc, part, write)

    in_specs = [pl.BlockSpec((tm, tc), lambda i, ko, c: (i, c)),
                pl.BlockSpec((None, tko, tc), lambda i, ko, c: (c // npj, ko, c % npj))]
    args = [dy, w]
    if has_add:
        in_specs.append(pl.BlockSpec((tm, tko), lambda i, ko, c: (i, ko)))
        args.append(add)
    return pl.pallas_call(
        body, grid=(M // tm, K // tko, nc), in_specs=in_specs,
        out_specs=pl.BlockSpec((tm, tko), lambda i, ko, c: (i, ko)),
        out_shape=jax.ShapeDtypeStruct((M, K), F32),
        scratch_shapes=[pltpu.VMEM((tm, tko), F32)],
        compiler_params=_params("parallel", "parallel", "arbitrary"), name=name)(*args)


def _mm_tn(a, dy, J, name="mm_tn", slots=None, slot0=0, into=None):
    M, K = a.shape
    M2, N = dy.shape
    assert M == M2 and N % J == 0
    n = N // J
    tko = _pick(K, 1024, LANES)
    tn = _pick(n, 1408, LANES)
    tmc = _pick(M, 2176, 16)
    npj, nm = n // tn, M // tmc

    def body(*refs):
        a_ref, dy_ref, o_ref, acc = refs[0], refs[1], refs[-2], refs[-1]
        part = lax.dot_general(a_ref[...].astype(BF16), dy_ref[...].astype(BF16),
                               (((0,), (0,)), ((), ())), preferred_element_type=F32)

        def write(total):
            o_ref[...] = total

        _accumulate(pl.program_id(2), nm, acc, part, write)

    in_specs = [pl.BlockSpec((tmc, tko), lambda ko, j, m: (m, ko)),
                pl.BlockSpec((tmc, tn), lambda ko, j, m: (m, j))]
    args, aliases = [a, dy], {}
    if into is not None:
        in_specs.append(pl.BlockSpec(memory_space=pl.ANY))
        args.append(into)
        aliases = {2: 0}
    return pl.pallas_call(
        body, grid=(K // tko, J * npj, nm), in_specs=in_specs,
        out_specs=pl.BlockSpec((None, tko, tn), lambda ko, j, m: (slot0 + j // npj, ko, j % npj)),
        out_shape=jax.ShapeDtypeStruct((slots or J, K, n), F32), input_output_aliases=aliases,
        scratch_shapes=[pltpu.VMEM((tko, tn), F32)],
        compiler_params=_params("parallel", "parallel", "arbitrary"), name=name)(*args)


def _rowwise(name, fn, T, tr, n_xt, row_in, grp_in=(), const_in=(), row_out=(), gsum_w=(), tsum_w=(), ncol=1):
    nt = T // tr
    n_in = len(row_in) + len(grp_in) + len(const_in)
    n_ro, n_gs = len(row_out), len(gsum_w)
    assert ncol == 1 or not (gsum_w or tsum_w)

    def body(*refs):
        t = pl.program_id(0)
        vals = [r[...] for r in refs[:n_in]]
        rv = vals[:len(row_in)]
        gv = vals[len(row_in):len(row_in) + len(grp_in)]
        cv = vals[len(row_in) + len(grp_in):]
        ro, gs, ts = fn(rv, gv, cv)
        outs = refs[n_in:]
        for ref, val in zip(outs[:n_ro], ro):
            ref[...] = val.astype(ref.dtype)
        first_g = jnp.logical_or(t == 0, t == n_xt)
        for ref, val, first in ([(r, v, first_g) for r, v in zip(outs[n_ro:n_ro + n_gs], gs)]
                                + [(r, v, t == 0) for r, v in zip(outs[n_ro + n_gs:], ts)]):
            s = jnp.sum(val, axis=0, keepdims=True)

            @pl.when(first)
            def _(ref=ref, s=s):
                ref[...] = s

            @pl.when(jnp.logical_not(first))
            def _(ref=ref, s=s):
                ref[...] += s

    in_specs, args = [], []
    for spec in row_in:
        arr, width = spec[:2]
        lead = spec[2] if len(spec) > 2 else None
        step = spec[3] if len(spec) > 3 else 1
        if lead is not None:
            in_specs.append(pl.BlockSpec((None, tr, width), lambda t, cb, lead=lead, step=step: (lead, t, cb * step)))
        else:
            in_specs.append(pl.BlockSpec((tr, width), lambda t, cb, step=step: (t, cb * step)))
        args.append(arr)
    for arr in grp_in:
        in_specs.append(pl.BlockSpec((None,) + arr.shape[1:], lambda t, cb: (jnp.where(t >= n_xt, 1, 0), 0, 0, 0)))
        args.append(arr)
    for arr in const_in:
        in_specs.append(pl.BlockSpec(arr.shape, lambda t, cb: (0, 0, 0)))
        args.append(arr)
    out_specs, out_shape = [], []
    for wtot, wblk, dt in row_out:
        out_specs.append(pl.BlockSpec((tr, wblk), lambda t, cb: (t, cb)))
        out_shape.append(jax.ShapeDtypeStruct((T, wtot), dt))
    for w in gsum_w:
        out_specs.append(pl.BlockSpec((None, 1, w), lambda t, cb: (jnp.where(t >= n_xt, 1, 0), 0, 0)))
        out_shape.append(jax.ShapeDtypeStruct((2, 1, w), F32))
    for w in tsum_w:
        out_specs.append(pl.BlockSpec((1, w), lambda t, cb: (0, 0)))
        out_shape.append(jax.ShapeDtypeStruct((1, w), F32))
    sem = ("arbitrary", "arbitrary") if (gsum_w or tsum_w) else ("parallel", "parallel")
    return pl.pallas_call(body, grid=(nt, ncol), in_specs=in_specs, out_specs=out_specs, out_shape=out_shape,
                          compiler_params=_params(*sem), name=name)(*args)


def _modulate(h, mod, T, tr, n_xt):
    D = h.shape[1]

    def fn(rv, gv, cv):
        m = gv[0]
        return [rv[0] * (1.0 + m[1]) + m[0]], [], []

    return _rowwise("modulate", fn, T, tr, n_xt, [(h, D)], [mod], row_out=[(D, D, BF16)])[0]


def _ln_stats(r):
    mu = jnp.mean(r, axis=-1, keepdims=True)
    xc = r - mu
    var = jnp.mean(xc * xc, axis=-1, keepdims=True)
    rstd = lax.rsqrt(var + LN_EPS)
    return xc * rstd, rstd


def _ln_fwd(h, y, mod, lnp, gate_row, mod_next, next_rows, T, tr, n_xt):
    D = h.shape[1]
    with_u = mod_next is not None

    def fn(rv, gv, cv):
        r = DEEPNORM_ALPHA * rv[0] + gv[0][gate_row] * rv[1]
        xhat, _ = _ln_stats(r)
        out = xhat * cv[0][0] + cv[0][1]
        if not with_u:
            return [out], [], []
        mn = gv[1]
        return [out, out * (1.0 + mn[next_rows[1]]) + mn[next_rows[0]]], [], []

    grp = [mod, mod_next] if with_u else [mod]
    outs = [(D, D, F32), (D, D, BF16)] if with_u else [(D, D, F32)]
    res = _rowwise("ln_fwd", fn, T, tr, n_xt, [(h, D), (y, D)], grp, [lnp], row_out=outs)
    return (res[0], res[1]) if with_u else (res[0], None)


def _ln_bwd(h, y, dout, mod, lnp, gate_row, T, tr, n_xt):
    D = h.shape[1]

    def fn(rv, gv, cv):
        g = gv[0][gate_row]
        r = DEEPNORM_ALPHA * rv[0] + g * rv[1]
        xhat, rstd = _ln_stats(r)
        dxh = rv[2] * cv[0][0]
        m1 = jnp.mean(dxh, axis=-1, keepdims=True)
        m2 = jnp.mean(dxh * xhat, axis=-1, keepdims=True)
        dr = rstd * (dxh - m1 - xhat * m2)
        return [DEEPNORM_ALPHA * dr, g * dr], [dr * rv[1]], [rv[2] * xhat, rv[2]]

    return _rowwise("ln_bwd", fn, T, tr, n_xt, [(h, D), (y, D), (dout, D)], [mod], [lnp],
                    row_out=[(D, D, F32), (D, D, BF16)], gsum_w=[D], tsum_w=[D, D])


def _mod_bwd(dh_res, du, h, mod, scale_row, T, tr, n_xt):
    D = h.shape[1]

    def fn(rv, gv, cv):
        return [rv[0] + rv[1] * (1.0 + gv[0][scale_row])], [rv[1] * rv[2], rv[1]], []

    return _rowwise("mod_bwd", fn, T, tr, n_xt, [(dh_res, D), (du, D), (h, D)], [mod],
                    row_out=[(D, D, F32)], gsum_w=[D, D])


def _swiglu_fwd(a, b, T, tr, n_xt):
    F = a.shape[1]
    wc = _pick(F, 1408, LANES)

    def fn(rv, gv, cv):
        return [rv[0] * _sigmoid(rv[0]) * rv[1]], [], []

    return _rowwise("swiglu_fwd", fn, T, tr, n_xt, [(a, wc), (b, wc)], row_out=[(F, wc, BF16)], ncol=F // wc)[0]


def _swiglu_bwd(a, b, dact, T, tr, n_xt):
    F = a.shape[1]
    wc = _pick(F, 1408, LANES)

    def fn(rv, gv, cv):
        av, bv, dv = rv
        sg = _sigmoid(av)
        return [dv * bv * (sg * (1.0 + av * (1.0 - sg))), dv * av * sg], [], []

    return _rowwise("swiglu_bwd", fn, T, tr, n_xt, [(a, wc), (b, wc), (dact, wc)],
                    row_out=[(F, wc, BF16), (F, wc, BF16)], ncol=F // wc)


def _gn(o):
    mu = jnp.mean(o, axis=-1, keepdims=True)
    xc = o - mu
    var = jnp.mean(xc * xc, axis=-1, keepdims=True)
    rstd = lax.rsqrt(var + GN_EPS)
    return xc * rstd, rstd


def _gate_fwd(gf, gb, o, T, tr, n_xt, dv):
    W = gf.shape[1]

    def fn(rv, gv, cv):
        xf, _ = _gn(rv[2])
        xb, _ = _gn(rv[3])
        return [rv[0] * _sigmoid(rv[0]) * xf + rv[1] * _sigmoid(rv[1]) * xb], [], []

    return _rowwise("gate_fwd", fn, T, tr, n_xt, [(gf, dv), (gb, dv), (o, dv, 0), (o, dv, 1)],
                    row_out=[(W, dv, BF16)], ncol=W // dv)[0]


def _gate_bwd(gf, gb, o, dy, T, tr, n_xt, dv):
    W = gf.shape[1]

    def fn(rv, gv, cv):
        outs_g, outs_o = [], []
        for g, ov in ((rv[0], rv[2]), (rv[1], rv[3])):
            xh, rstd = _gn(ov)
            sg = _sigmoid(g)
            outs_g.append(rv[4] * xh * (sg * (1.0 + g * (1.0 - sg))))
            dxh = rv[4] * g * sg
            m1 = jnp.mean(dxh, axis=-1, keepdims=True)
            m2 = jnp.mean(dxh * xh, axis=-1, keepdims=True)
            outs_o.append(rstd * (dxh - m1 - xh * m2))
        return outs_g + outs_o, [], []

    return _rowwise("gate_bwd", fn, T, tr, n_xt,
                    [(gf, dv), (gb, dv), (o, dv, 0), (o, dv, 1), (dy, dv)],
                    row_out=[(W, dv, BF16)] * 4, ncol=W // dv)


def _add_dirs(dq, dk, dv_, T, tr, n_xt):
    ws = [dq.shape[2], dk.shape[2], dv_.shape[2]]

    def fn(rv, gv, cv):
        return [rv[0] + rv[1], rv[2] + rv[3], rv[4] + rv[5]], [], []

    row_in = []
    for arr, w in zip((dq, dk, dv_), ws):
        row_in += [(arr, w, 0), (arr, w, 1)]
    return _rowwise("add_dirs", fn, T, tr, n_xt, row_in, row_out=[(w, w, BF16) for w in ws])


def _swap16(x):
    lane = lax.broadcasted_iota(jnp.int32, x.shape, x.ndim - 1)
    return jnp.where(lane % 32 < 16, pltpu.roll(x, LANES - 16, x.ndim - 1), pltpu.roll(x, 16, x.ndim - 1))


def _rope2d(x, cos, sin, transpose):
    if transpose:
        return x * cos + _swap16(x * sin)
    return x * cos + _swap16(x) * sin


def _rms(x, g):
    rstd = lax.rsqrt(jnp.mean(x * x, axis=-1, keepdims=True) + RMS_EPS)
    return x * rstd, rstd


def _lora_fwd(lora, gq_gkv, cos, sin, T, tr, n_xt):
    L = MLA_LORA

    def fn(rv, gv, cv):
        x = rv[0]
        xq, _ = _rms(x[:, :L], None)
        xkv, _ = _rms(x[:, L:2 * L], None)
        return [xq * cv[0][0], xkv * cv[0][1], _rope2d(x[:, 2 * L:], rv[1], rv[2], False)], [], []

    return _rowwise("lora_fwd", fn, T, tr, n_xt, [(lora, 2 * L + LANES), (cos, LANES), (sin, LANES)],
                    const_in=[gq_gkv], row_out=[(L, L, BF16), (L, L, BF16), (LANES, LANES, BF16)])


def _lora_bwd(lora, dq, dkv, dkr, gq_gkv, cos, sin, T, tr, n_xt):
    L = MLA_LORA

    def fn(rv, gv, cv):
        x = rv[0]
        outs, sums = [], []
        for xs, dy, g in ((x[:, :L], rv[1], cv[0][0]), (x[:, L:2 * L], rv[2], cv[0][1])):
            xh, rstd = _rms(xs, None)
            dxh = dy * g
            outs.append(rstd * (dxh - xh * jnp.mean(dxh * xh, axis=-1, keepdims=True)))
            sums.append(dy * xh)
        outs.append(_rope2d(rv[3], rv[4], rv[5], True))
        return [jnp.concatenate(outs, axis=1)], [], sums

    W = 2 * L + LANES
    return _rowwise("lora_bwd", fn, T, tr, n_xt,
                    [(lora, W), (dq, L), (dkv, L), (dkr, LANES), (cos, LANES), (sin, LANES)],
                    const_in=[gq_gkv], row_out=[(W, W, BF16)], tsum_w=[L, L])


def _qrope(q, cos, sin, transpose, T, tr, n_xt):
    W = q.shape[1]

    def fn(rv, gv, cv):
        x = rv[0] * MLA_SCALE
        parts = []
        for hd in range(W // (2 * LANES)):
            lo = hd * 2 * LANES
            parts += [x[:, lo:lo + LANES], _rope2d(x[:, lo + LANES:lo + 2 * LANES], rv[1], rv[2], transpose)]
        return [jnp.concatenate(parts, axis=1)], [], []

    return _rowwise("qrope_bwd" if transpose else "qrope_fwd", fn, T, tr, n_xt,
                    [(q, W), (cos, LANES), (sin, LANES)], row_out=[(W, W, BF16)])[0]


def _loss_grad(h, target, T, tr, n_xt):
    D = h.shape[1]
    nt = T // tr

    def body(h_ref, t_ref, dh_ref, s_ref):
        t = pl.program_id(0)
        diff = jnp.where(t < n_xt, h_ref[...] - t_ref[...], 0.0)
        dh_ref[...] = diff * (1.0 / D)
        s = jnp.sum(diff * diff, axis=0, keepdims=True)

        @pl.when(t == 0)
        def _():
            s_ref[...] = s

        @pl.when(t != 0)
        def _():
            s_ref[...] += s

    return pl.pallas_call(
        body, grid=(nt,),
        in_specs=[pl.BlockSpec((tr, D), lambda t: (t, 0)),
                  pl.BlockSpec((tr, D), lambda t: (jnp.minimum(t, n_xt - 1), 0))],
        out_specs=[pl.BlockSpec((tr, D), lambda t: (t, 0)), pl.BlockSpec((1, D), lambda t: (0, 0))],
        out_shape=[jax.ShapeDtypeStruct((T, D), F32), jax.ShapeDtypeStruct((1, D), F32)],
        compiler_params=_params("arbitrary"), name="loss_grad")(h, target)


def _adamw(w, g, m, v):
    shape = w.shape
    C = shape[-1] if w.ndim > 1 else shape[0]
    R = w.size // C
    tr = _pick(R, max(8, (2 << 20) // (C * 4)), 8)
    c1 = 1.0 - ADAM_B1 ** ADAM_STEP
    c2 = 1.0 - ADAM_B2 ** ADAM_STEP

    def body(w_ref, g_ref, m_ref, v_ref, d_ref, nm_ref, nv_ref):
        gv = g_ref[...]
        nm = ADAM_B1 * m_ref[...] + (1.0 - ADAM_B1) * gv
        nv = ADAM_B2 * v_ref[...] + (1.0 - ADAM_B2) * (gv * gv)
        nm_ref[...] = nm
        nv_ref[...] = nv
        d_ref[...] = -ADAM_LR * ((nm / c1) / (jnp.sqrt(nv / c2) + ADAM_EPS) + ADAM_WD * w_ref[...])

    spec = pl.BlockSpec((tr, C), lambda i: (i, 0))
    outs = pl.pallas_call(
        body, grid=(R // tr,), in_specs=[spec] * 4, out_specs=[spec] * 3,
        out_shape=[jax.ShapeDtypeStruct((R, C), F32)] * 3,
        compiler_params=_params("parallel"), name="adamw")(*[t.reshape(R, C) for t in (w, g, m, v)])
    return tuple(o.reshape(shape) for o in outs)


def _ret_chunk(d, s, ncx, ncc):
    fwd = jnp.where(s < ncc, ncx + s, s - ncc)
    bwd = jnp.where(s < ncc, ncx + ncc - 1 - s, ncx - 1 - (s - ncc))
    return jnp.where(d == 0, fwd, bwd)


def _rot_half(x, cos, sin, transpose):
    half = x.shape[-1] // 2
    if transpose:
        return x * cos + pltpu.roll(x * sin, half, x.ndim - 1)
    return x * cos + pltpu.roll(x, half, x.ndim - 1) * sin


def _dot_nt(a, b):
    return lax.dot_general(a, b, (((1,), (1,)), ((), ())), preferred_element_type=F32)


def _dot_tn(a, b):
    return lax.dot_general(a, b, (((0,), (0,)), ((), ())), preferred_element_type=F32)


def _dot(a, b):
    return jnp.dot(a, b, preferred_element_type=F32)


def _ret_fwd(q, k, v, tabs, ncx, ncc):
    T, C, H = q.shape[0], RET_CHUNK, RET_HEADS
    dk, dv = q.shape[1] // H, v.shape[1] // H
    ns = ncx + ncc
    kscale = dk ** -0.5

    G = RET_HEAD_GROUP

    def body(q_ref, k_ref, v_ref, cos_ref, sin_ref, intra_ref, qd_ref, kd_ref, cd_ref, o_ref, st_ref, s_scr):
        @pl.when(pl.program_id(2) == 0)
        def _():
            s_scr[...] = jnp.zeros_like(s_scr)

        cos, sin = cos_ref[...], sin_ref[...]
        for g in range(G):
            ks_, vs_ = slice(g * dk, (g + 1) * dk), slice(g * dv, (g + 1) * dv)
            qv = _rot_half(q_ref[:, ks_], cos, sin, False)
            kv = _rot_half(k_ref[:, ks_], cos, sin, False) * kscale
            vb = v_ref[:, vs_].astype(BF16)
            S = s_scr[g]
            st_ref[g] = S
            scores = _dot_nt(qv.astype(BF16), kv.astype(BF16)) * intra_ref[g]
            o_ref[:, vs_] = _dot(scores.astype(BF16), vb) + _dot((qv * qd_ref[g]).astype(BF16), S.astype(BF16))
            s_scr[g] = S * cd_ref[g] + _dot_tn((kv * kd_ref[g]).astype(BF16), vb)

    chunk = lambda d, h, s: _ret_chunk(d, s, ncx, ncc)
    tab = lambda shape: pl.BlockSpec((None, G) + shape, lambda d, h, s: (d, h, 0, 0))
    return pl.pallas_call(
        body, grid=(2, H // G, ns),
        in_specs=[pl.BlockSpec((C, G * dk), lambda d, h, s: (chunk(d, h, s), h)),
                  pl.BlockSpec((C, G * dk), lambda d, h, s: (chunk(d, h, s), h)),
                  pl.BlockSpec((C, G * dv), lambda d, h, s: (chunk(d, h, s), h)),
                  pl.BlockSpec((C, dk), lambda d, h, s: (chunk(d, h, s), 0)),
                  pl.BlockSpec((C, dk), lambda d, h, s: (chunk(d, h, s), 0)),
                  tab((C, C)), tab((C, dk)), tab((C, dk)), tab((1, dv))],
        out_specs=[pl.BlockSpec((None, C, G * dv), lambda d, h, s: (d, chunk(d, h, s), h)),
                   pl.BlockSpec((None, G, None, dk, dv), lambda d, h, s: (d, h, s, 0, 0))],
        out_shape=[jax.ShapeDtypeStruct((2, T, H * dv), F32), jax.ShapeDtypeStruct((2, H, ns, dk, dv), F32)],
        scratch_shapes=[pltpu.VMEM((G, dk, dv), F32)],
        compiler_params=_params("parallel", "parallel", "arbitrary"), name="ret_fwd",
    )(q, k, v, tabs["cos"], tabs["sin"], tabs["intra"], tabs["qd"], tabs["kd"], tabs["cd"])


def _ret_bwd(q, k, v, do, states, tabs, ncx, ncc):
    T, C, H = q.shape[0], RET_CHUNK, RET_HEADS
    dk, dv = q.shape[1] // H, v.shape[1] // H
    ns = ncx + ncc
    kscale = dk ** -0.5

    G = RET_HEAD_GROUP

    def body(q_ref, k_ref, v_ref, do_ref, st_ref, cos_ref, sin_ref, intra_ref, qd_ref, kd_ref, cd_ref,
             dm_ref, wq_ref, wk_ref, dq_ref, dk_ref, dv_ref, dl_ref, ds_scr):
        first = pl.program_id(2) == 0

        @pl.when(first)
        def _():
            ds_scr[...] = jnp.zeros_like(ds_scr)

        cos, sin = cos_ref[...], sin_ref[...]
        dm, wq, wk = dm_ref[...], wq_ref[...], wk_ref[...]
        for g in range(G):
            ks_, vs_ = slice(g * dk, (g + 1) * dk), slice(g * dv, (g + 1) * dv)
            qv = _rot_half(q_ref[:, ks_], cos, sin, False)
            kv = _rot_half(k_ref[:, ks_], cos, sin, False) * kscale
            qb, kb = qv.astype(BF16), kv.astype(BF16)
            vb = v_ref[:, vs_].astype(BF16)
            dob = do_ref[:, vs_]
            intra, qd, kd, cd = intra_ref[g], qd_ref[g], kd_ref[g], cd_ref[g]
            S, dS = st_ref[g], ds_scr[g]
            Sb, dSb = S.astype(BF16), dS.astype(BF16)
            P = _dot_nt(qb, kb) * intra
            dP_raw = _dot_nt(dob, vb)
            dPb = (dP_raw * intra).astype(BF16)
            dq_cross = _dot_nt(dob, Sb) * qd
            dq_rot = _dot(dPb, kb) + dq_cross
            dk_state = _dot_nt(vb, dSb) * kd
            dk_rot = _dot_tn(dPb, qb) + dk_state
            dv_ref[:, vs_] = _dot_tn(P.astype(BF16), dob) + _dot((kv * kd).astype(BF16), dSb)
            dq_ref[:, ks_] = _rot_half(dq_rot, cos, sin, True)
            dk_ref[:, ks_] = _rot_half(dk_rot, cos, sin, True) * kscale
            dlam = (jnp.sum(dm * P * dP_raw) + jnp.sum(wq * qv * dq_cross)
                    + C * jnp.sum(cd * S * dS) + jnp.sum(wk * kv * dk_state))
            dl = jnp.full((1, LANES), dlam, F32)

            @pl.when(first)
            def _(g=g, dl=dl):
                dl_ref[g] = dl

            @pl.when(jnp.logical_not(first))
            def _(g=g, dl=dl):
                dl_ref[g] += dl

            ds_scr[g] = cd * dS + _dot_tn((qv * qd).astype(BF16), dob)

    chunk = lambda d, h, s: _ret_chunk(d, ns - 1 - s, ncx, ncc)
    tab = lambda shape: pl.BlockSpec((None, G) + shape, lambda d, h, s: (d, h, 0, 0))
    dtab = lambda shape: pl.BlockSpec((None,) + shape, lambda d, h, s: (d, 0, 0))
    return pl.pallas_call(
        body, grid=(2, H // G, ns),
        in_specs=[pl.BlockSpec((C, G * dk), lambda d, h, s: (chunk(d, h, s), h)),
                  pl.BlockSpec((C, G * dk), lambda d, h, s: (chunk(d, h, s), h)),
                  pl.BlockSpec((C, G * dv), lambda d, h, s: (chunk(d, h, s), h)),
                  pl.BlockSpec((None, C, G * dv), lambda d, h, s: (d, chunk(d, h, s), h)),
                  pl.BlockSpec((None, G, None, dk, dv), lambda d, h, s: (d, h, ns - 1 - s, 0, 0)),
                  pl.BlockSpec((C, dk), lambda d, h, s: (chunk(d, h, s), 0)),
                  pl.BlockSpec((C, dk), lambda d, h, s: (chunk(d, h, s), 0)),
                  tab((C, C)), tab((C, dk)), tab((C, dk)), tab((1, dv)),
                  dtab((C, C)), dtab((C, dk)), dtab((C, dk))],
        out_specs=[pl.BlockSpec((None, C, G * dk), lambda d, h, s: (d, chunk(d, h, s), h)),
                   pl.BlockSpec((None, C, G * dk), lambda d, h, s: (d, chunk(d, h, s), h)),
                   pl.BlockSpec((None, C, G * dv), lambda d, h, s: (d, chunk(d, h, s), h)),
                   pl.BlockSpec((None, G, 1, LANES), lambda d, h, s: (d, h, 0, 0))],
        out_shape=[jax.ShapeDtypeStruct((2, T, H * dk), F32), jax.ShapeDtypeStruct((2, T, H * dk), F32),
                   jax.ShapeDtypeStruct((2, T, H * dv), F32), jax.ShapeDtypeStruct((2, H, 1, LANES), F32)],
        scratch_shapes=[pltpu.VMEM((G, dk, dv), F32)],
        compiler_params=_params("parallel", "parallel", "arbitrary"), name="ret_bwd",
    )(q, k, v, do, states, tabs["cos"], tabs["sin"], tabs["intra"], tabs["qd"], tabs["kd"], tabs["cd"],
      tabs["dmat"], tabs["wq"], tabs["wk"])


def _ret_tables(decay_logit, Nx, Nc, dk, dv):
    C, H = RET_CHUNK, RET_HEADS
    inv = RET_ROPE_BASE ** (-jnp.linspace(0.0, 1.0, dk // 2, dtype=F32))
    ang = jnp.arange(Nx, dtype=F32)[:, None] * inv[None, :]
    cos, sin = jnp.cos(ang), jnp.sin(ang)
    cosf = jnp.concatenate([jnp.concatenate([cos, cos], 1), jnp.ones((Nc, dk), F32)], 0)
    sinf = jnp.concatenate([jnp.concatenate([-sin, sin], 1), jnp.zeros((Nc, dk), F32)], 0)
    lg = jax.nn.log_sigmoid(decay_logit.astype(F32))
    idx = jnp.arange(C, dtype=F32)
    diff = idx[:, None] - idx[None, :]
    dmat = jnp.stack([jnp.maximum(diff, 0.0), jnp.maximum(-diff, 0.0)])
    mask = jnp.stack([diff >= 0, diff <= 0])
    intra = jnp.where(mask[:, None], jnp.exp(lg[:, :, None, None] * dmat[:, None]), 0.0)
    wq = jnp.stack([idx + 1.0, C - idx])
    wk = jnp.stack([C - 1.0 - idx, idx])
    qd = jnp.exp(lg[:, :, None] * wq[:, None, :])
    kd = jnp.exp(lg[:, :, None] * wk[:, None, :])
    cd = jnp.exp(lg * C)
    bc = lambda t, w: jnp.broadcast_to(t[..., None], t.shape + (w,))
    return dict(cos=cosf, sin=sinf, intra=intra, qd=bc(qd, dk), kd=bc(kd, dk),
                cd=jnp.broadcast_to(cd[:, :, None, None], (2, H, 1, dv)),
                dmat=dmat, wq=bc(wq, dk), wk=bc(wk, dk), lg=lg)


def _attn_fwd(q, kn, v, kr, Nx, tq):
    T, H = q.shape[0], MLA_HEADS
    n_xq = Nx // tq

    def body(q_ref, kn_ref, v_ref, kr_ref, o_ref, lse_ref):
        def attend(lo):
            kcat = jnp.concatenate([kn_ref[lo:, :], kr_ref[lo:, :]], axis=1)
            s = _dot_nt(q_ref[...], kcat)
            m = jnp.max(s, axis=-1, keepdims=True)
            p = jnp.exp(s - m)
            l = jnp.sum(p, axis=-1, keepdims=True)
            o_ref[...] = (_dot(p.astype(BF16), v_ref[lo:, :]) / l).astype(o_ref.dtype)
            lse_ref[...] = m + jnp.log(l)

        @pl.when(pl.program_id(1) < n_xq)
        def _():
            attend(0)

        @pl.when(pl.program_id(1) >= n_xq)
        def _():
            attend(Nx)

    return pl.pallas_call(
        body, grid=(H, T // tq),
        in_specs=[pl.BlockSpec((tq, 2 * LANES), lambda h, i: (i, h)),
                  pl.BlockSpec((T, LANES), lambda h, i: (0, h)),
                  pl.BlockSpec((T, LANES), lambda h, i: (0, h)),
                  pl.BlockSpec((T, LANES), lambda h, i: (0, 0))],
        out_specs=[pl.BlockSpec((tq, LANES), lambda h, i: (i, h)),
                   pl.BlockSpec((None, tq, 1), lambda h, i: (h, i, 0))],
        out_shape=[jax.ShapeDtypeStruct((T, H * LANES), BF16), jax.ShapeDtypeStruct((H, T, 1), F32)],
        compiler_params=_params("parallel", "arbitrary"), name="attn_fwd")(q, kn, v, kr)


def _attn_bwd(q, kn, v, kr, do, lse, Nx, tq):
    T, H = q.shape[0], MLA_HEADS
    n_xq, nq = Nx // tq, T // tq

    def body(q_ref, kn_ref, v_ref, kr_ref, do_ref, lse_ref, dq_ref, dkn_ref, dv_ref, dkr_ref, dk_acc, dv_acc):
        h, i = pl.program_id(0), pl.program_id(1)

        @pl.when(i == 0)
        def _():
            dk_acc[...] = jnp.zeros_like(dk_acc)
            dv_acc[...] = jnp.zeros_like(dv_acc)

        def attend(lo):
            kcat = jnp.concatenate([kn_ref[lo:, :], kr_ref[lo:, :]], axis=1)
            qb, dob = q_ref[...], do_ref[...]
            p = jnp.exp(_dot_nt(qb, kcat) - lse_ref[...])
            dp = _dot_nt(dob, v_ref[lo:, :])
            delta = jnp.sum(p * dp, axis=-1, keepdims=True)
            dsb = (p * (dp - delta)).astype(BF16)
            dq_ref[...] = _dot(dsb, kcat)
            dk_acc[lo:, :] += _dot_tn(dsb, qb)
            dv_acc[lo:, :] += _dot_tn(p.astype(BF16), dob)

        @pl.when(i < n_xq)
        def _():
            attend(0)

        @pl.when(i >= n_xq)
        def _():
            attend(Nx)

        @pl.when(i == nq - 1)
        def _():
            dkn_ref[...] = dk_acc[:, :LANES].astype(dkn_ref.dtype)
            dv_ref[...] = dv_acc[...].astype(dv_ref.dtype)

        @pl.when(jnp.logical_and(i == nq - 1, h == 0))
        def _():
            dkr_ref[...] = dk_acc[:, LANES:]

        @pl.when(jnp.logical_and(i == nq - 1, h != 0))
        def _():
            dkr_ref[...] += dk_acc[:, LANES:]

    return pl.pallas_call(
        body, grid=(H, nq),
        in_specs=[pl.BlockSpec((tq, 2 * LANES), lambda h, i: (i, h)),
                  pl.BlockSpec((T, LANES), lambda h, i: (0, h)),
                  pl.BlockSpec((T, LANES), lambda h, i: (0, h)),
                  pl.BlockSpec((T, LANES), lambda h, i: (0, 0)),
                  pl.BlockSpec((tq, LANES), lambda h, i: (i, h)),
                  pl.BlockSpec((None, tq, 1), lambda h, i: (h, i, 0))],
        out_specs=[pl.BlockSpec((tq, 2 * LANES), lambda h, i: (i, h)),
                   pl.BlockSpec((T, LANES), lambda h, i: (0, h)),
                   pl.BlockSpec((T, LANES), lambda h, i: (0, h)),
                   pl.BlockSpec((T, LANES), lambda h, i: (0, 0))],
        out_shape=[jax.ShapeDtypeStruct((T, H * 2 * LANES), F32), jax.ShapeDtypeStruct((T, H * LANES), BF16),
                   jax.ShapeDtypeStruct((T, H * LANES), BF16), jax.ShapeDtypeStruct((T, LANES), F32)],
        scratch_shapes=[pltpu.VMEM((T, 2 * LANES), F32), pltpu.VMEM((T, LANES), F32)],
        compiler_params=_params("arbitrary", "arbitrary"), name="attn_bwd")(q, kn, v, kr, do, lse)


def _mla_tables(Nx, Nc):
    ad = MLA_D_ROPE // 2
    inv = AXIAL_ROPE_BASE ** (-jnp.arange(ad // 2, dtype=F32) * 2.0 / ad)
    t = jnp.arange(Nx)
    rang = (t // GRID_W).astype(F32)[:, None] * inv[None, :]
    cang = (t % GRID_W).astype(F32)[:, None] * inv[None, :]
    rc, rs, cc, cs = jnp.cos(rang), jnp.sin(rang), jnp.cos(cang), jnp.sin(cang)
    pad1, pad0 = jnp.ones((Nx, LANES - MLA_D_ROPE), F32), jnp.zeros((Nx, LANES - MLA_D_ROPE), F32)
    cos = jnp.concatenate([rc, rc, cc, cc, pad1], 1)
    sin = jnp.concatenate([-rs, rs, -cs, cs, pad0], 1)
    return (jnp.concatenate([cos, jnp.ones((Nc, LANES), F32)], 0),
            jnp.concatenate([sin, jnp.zeros((Nc, LANES), F32)], 0))


def _place():
    x, y, c = lax.axis_index("x"), lax.axis_index("y"), lax.axis_index("c")
    return x, y, c


def _all_gather8(v):
    R = v.shape[0]

    def body(v_ref, g_ref, s_ref, send_sems, recv_sems):
        x, y, c = _place()
        me = 4 * x + 2 * y + c
        g_ref[me] = v_ref[...]
        copies = []
        for k in range(1, N_DEV):
            peer = (x ^ (k >> 2), y ^ ((k >> 1) & 1), c ^ (k & 1))
            copies.append(pltpu.make_async_remote_copy(
                src_ref=v_ref, dst_ref=g_ref.at[me], send_sem=send_sems.at[k - 1], recv_sem=recv_sems.at[k - 1],
                device_id=peer, device_id_type=MESH))
        for cp in copies:
            cp.start()
        for cp in copies:
            cp.wait_recv()
        for cp in copies:
            cp.wait_send()
        acc = g_ref[0]
        for k in range(1, N_DEV):
            acc = acc + g_ref[k]
        s_ref[...] = acc

    vm = pl.BlockSpec(memory_space=pltpu.VMEM)
    return pl.pallas_call(
        body, in_specs=[vm], out_specs=[vm, vm],
        out_shape=[jax.ShapeDtypeStruct((N_DEV, R, LANES), F32), jax.ShapeDtypeStruct((R, LANES), F32)],
        scratch_shapes=[pltpu.SemaphoreType.DMA((N_DEV - 1,)), pltpu.SemaphoreType.DMA((N_DEV - 1,))],
        compiler_params=pltpu.CompilerParams(vmem_limit_bytes=V7X_VMEM_LIMIT_BYTES), name="all_gather8")(v)


def _other_chips(x, y):
    return [(1 - x, y), (x, 1 - y), (1 - x, 1 - y)]


def _place_own(w, chip_idx, after):
    R, C = w.shape
    tr = _pick(R, max(16, (2 << 20) // (C * 4)), 16)

    def body(s_ref, w_ref, after_ref, o_ref):
        o_ref[...] = w_ref[...].astype(o_ref.dtype)

    return pl.pallas_call(
        body,
        grid_spec=pltpu.PrefetchScalarGridSpec(
            num_scalar_prefetch=1, grid=(R // tr,),
            in_specs=[pl.BlockSpec((tr, C), lambda i, s: (i, 0)), pl.BlockSpec(memory_space=pl.ANY)],
            out_specs=pl.BlockSpec((None, tr, C), lambda i, s: (s[0], i, 0))),
        out_shape=jax.ShapeDtypeStruct((N_CHIPS, R, C), BF16),
        compiler_params=_params("parallel"), name="place_own")(chip_idx, w, after)


HBM_SPEC = pl.BlockSpec(memory_space=pltpu.HBM)
SEM_SPEC = pl.BlockSpec(memory_space=pltpu.SEMAPHORE)
DATAFLOW_EFFECT = pltpu.SideEffectType.DATAFLOW_SIDE_EFFECTING


def _gather_start(buf, name):
    def body(b_ref, send_sem, recv_sem, b_thru, token):
        x, y, c = _place()
        chip = 2 * x + y
        for ch in _other_chips(x, y):
            pltpu.make_async_remote_copy(src_ref=b_ref.at[chip], dst_ref=b_ref.at[chip], send_sem=send_sem,
                                         recv_sem=recv_sem, device_id=(*ch, c), device_id_type=MESH).start()
        token[...] = jnp.zeros_like(token)

    return pl.pallas_call(
        body, name=name,
        out_shape=(pltpu.SemaphoreType.DMA(()), pltpu.SemaphoreType.DMA(()), pltpu.HBM(buf.shape, buf.dtype),
                   jax.ShapeDtypeStruct((8, LANES), F32)),
        in_specs=(HBM_SPEC,), out_specs=(SEM_SPEC, SEM_SPEC, HBM_SPEC, pl.BlockSpec(memory_space=pltpu.VMEM)),
        input_output_aliases={0: 2},
        compiler_params=pltpu.CompilerParams(has_side_effects=DATAFLOW_EFFECT),
    )(pltpu.with_memory_space_constraint(buf, pltpu.HBM))


def _gather_wait(send_sem, recv_sem, buf_thru, after, name):
    def body(b_ref, send_sem, recv_sem, after_ref, b_out):
        x, y, c = _place()
        three = b_ref.at[pl.ds(0, 3)]
        cp = pltpu.make_async_remote_copy(src_ref=three, dst_ref=three, send_sem=send_sem, recv_sem=recv_sem,
                                          device_id=(x, y, 1 - c), device_id_type=MESH)
        cp.wait_send()
        cp.wait_recv()

    return pl.pallas_call(
        body, name=name, out_shape=pltpu.HBM(buf_thru.shape, buf_thru.dtype),
        in_specs=(HBM_SPEC, SEM_SPEC, SEM_SPEC, pl.BlockSpec(memory_space=pl.ANY)), out_specs=HBM_SPEC,
        input_output_aliases={0: 0},
        compiler_params=pltpu.CompilerParams(has_side_effects=DATAFLOW_EFFECT),
    )(buf_thru, send_sem, recv_sem, after)


def _scatter_start(hb, name):
    def body(h_ref, land_ref, send_sem, recv_sem, h_thru, land_thru, token):
        x, y, c = _place()
        chip = 2 * x + y
        for ch in _other_chips(x, y):
            pltpu.make_async_remote_copy(src_ref=h_ref.at[2 * ch[0] + ch[1]], dst_ref=land_ref.at[chip],
                                         send_sem=send_sem, recv_sem=recv_sem, device_id=(*ch, c),
                                         device_id_type=MESH).start()
        token[...] = jnp.zeros_like(token)

    return pl.pallas_call(
        body, name=name,
        out_shape=(pltpu.SemaphoreType.DMA(()), pltpu.SemaphoreType.DMA(()), pltpu.HBM(hb.shape, hb.dtype),
                   pltpu.HBM(hb.shape, hb.dtype), jax.ShapeDtypeStruct((8, LANES), F32)),
        in_specs=(HBM_SPEC, HBM_SPEC),
        out_specs=(SEM_SPEC, SEM_SPEC, HBM_SPEC, HBM_SPEC, pl.BlockSpec(memory_space=pltpu.VMEM)),
        input_output_aliases={0: 2, 1: 3},
        compiler_params=pltpu.CompilerParams(has_side_effects=DATAFLOW_EFFECT),
    )(pltpu.with_memory_space_constraint(hb, pltpu.HBM),
      pltpu.with_memory_space_constraint(lax.empty(hb.shape, hb.dtype), pltpu.HBM))


def _scatter_wait(send_sem, recv_sem, h_thru, land_thru, after, name):
    def body(h_ref, land_ref, send_sem, recv_sem, after_ref, h_dead, got_ref):
        x, y, c = _place()
        cp = pltpu.make_async_remote_copy(src_ref=h_ref.at[pl.ds(0, 3)], dst_ref=land_ref.at[pl.ds(0, 3)],
                                          send_sem=send_sem, recv_sem=recv_sem, device_id=(x, y, 1 - c),
                                          device_id_type=MESH)
        cp.wait_send()
        cp.wait_recv()

    return pl.pallas_call(
        body, name=name,
        out_shape=(pltpu.HBM(h_thru.shape, h_thru.dtype), pltpu.HBM(land_thru.shape, land_thru.dtype)),
        in_specs=(HBM_SPEC, HBM_SPEC, SEM_SPEC, SEM_SPEC, pl.BlockSpec(memory_space=pl.ANY)),
        out_specs=(HBM_SPEC, HBM_SPEC), input_output_aliases={0: 0, 1: 1},
        compiler_params=pltpu.CompilerParams(has_side_effects=DATAFLOW_EFFECT),
    )(h_thru, land_thru, send_sem, recv_sem, after)[1]


def _swap_start(g, name):
    J, R, C = g.shape
    half = R // 2

    def body(g_ref, land_ref, send_sem, recv_sem, g_thru, land_thru, token):
        x, y, c = _place()
        pltpu.make_async_remote_copy(src_ref=g_ref.at[:, pl.ds((1 - c) * half, half), :], dst_ref=land_ref,
                                     send_sem=send_sem, recv_sem=recv_sem, device_id=(x, y, 1 - c),
                                     device_id_type=MESH).start()
        token[...] = jnp.zeros_like(token)

    return pl.pallas_call(
        body, name=name,
        out_shape=(pltpu.SemaphoreType.DMA(()), pltpu.SemaphoreType.DMA(()), pltpu.HBM(g.shape, g.dtype),
                   pltpu.HBM((J, half, C), g.dtype), jax.ShapeDtypeStruct((8, LANES), F32)),
        in_specs=(HBM_SPEC, HBM_SPEC),
        out_specs=(SEM_SPEC, SEM_SPEC, HBM_SPEC, HBM_SPEC, pl.BlockSpec(memory_space=pltpu.VMEM)),
        input_output_aliases={0: 2, 1: 3},
        compiler_params=pltpu.CompilerParams(has_side_effects=DATAFLOW_EFFECT),
    )(pltpu.with_memory_space_constraint(g, pltpu.HBM),
      pltpu.with_memory_space_constraint(lax.empty((J, half, C), g.dtype), pltpu.HBM))


def _swap_wait(send_sem, recv_sem, g_thru, land_thru, after, name):
    half = land_thru.shape[1]

    def body(g_ref, land_ref, send_sem, recv_sem, after_ref, g_out, got_ref):
        x, y, c = _place()
        cp = pltpu.make_async_remote_copy(src_ref=g_ref.at[:, pl.ds((1 - c) * half, half), :], dst_ref=land_ref,
                                          send_sem=send_sem, recv_sem=recv_sem, device_id=(x, y, 1 - c),
                                          device_id_type=MESH)
        cp.wait_send()
        cp.wait_recv()

    return pl.pallas_call(
        body, name=name,
        out_shape=(pltpu.HBM(g_thru.shape, g_thru.dtype), pltpu.HBM(land_thru.shape, land_thru.dtype)),
        in_specs=(HBM_SPEC, HBM_SPEC, SEM_SPEC, SEM_SPEC, pl.BlockSpec(memory_space=pl.ANY)),
        out_specs=(HBM_SPEC, HBM_SPEC), input_output_aliases={0: 0, 1: 1},
        compiler_params=pltpu.CompilerParams(has_side_effects=DATAFLOW_EFFECT),
    )(g_thru, land_thru, send_sem, recv_sem, after)


def _add_half(g, r, place_idx):
    J, R, C = g.shape
    half = R // 2
    tr = _pick(half, max(16, (2 << 20) // (C * 4)), 16)
    nb = half // tr

    def body(s_ref, g_ref, r_ref, o_ref, ob_ref):
        acc = g_ref[...] + r_ref[...]
        o_ref[...] = acc
        ob_ref[...] = acc.astype(ob_ref.dtype)

    spec = pl.BlockSpec((None, tr, C), lambda j, i, s: (j, i, 0))
    return pl.pallas_call(
        body,
        grid_spec=pltpu.PrefetchScalarGridSpec(
            num_scalar_prefetch=1, grid=(J, nb),
            in_specs=[pl.BlockSpec((None, tr, C), lambda j, i, s: (j, s[0] * nb + i, 0)), spec],
            out_specs=[spec, spec]),
        out_shape=[jax.ShapeDtypeStruct((J, half, C), F32), jax.ShapeDtypeStruct((J, half, C), BF16)],
        compiler_params=_params("parallel", "parallel"), name="add_half")(place_idx[4], g, r)


def _add_chips(h, p, place_idx):
    J, R, C = h.shape
    tr = _pick(R, max(16, (2 << 20) // (C * 4)), 16)
    nb = R // tr

    def body(s0, s1, s2, s3, s4, h_ref, p0_ref, p1_ref, p2_ref, o_ref):
        o_ref[...] = ((h_ref[...] + p0_ref[...].astype(F32)) + p1_ref[...].astype(F32)) + p2_ref[...].astype(F32)

    slot = lambda k: pl.BlockSpec((None, tr, C), lambda i, *s: (s[k][0], i, 0))
    return pl.pallas_call(
        body,
        grid_spec=pltpu.PrefetchScalarGridSpec(
            num_scalar_prefetch=5, grid=(nb,),
            in_specs=[slot(0), slot(1), slot(2), slot(3)],
            out_specs=pl.BlockSpec((tr, C), lambda i, *s: (s[4][0] * nb + i, 0))),
        out_shape=jax.ShapeDtypeStruct((2 * R, C), F32),
        compiler_params=_params("parallel"), name="add_chips")(*place_idx, h, p, p, p)


def _join_halves(s):
    R, C = s.shape
    half = R // 2

    def body(in_ref, o_ref, send_sem, recv_sem):
        x, y, c = _place()
        mine = o_ref.at[pl.ds(c * half, half)]
        cp = pltpu.make_async_remote_copy(src_ref=mine, dst_ref=mine, send_sem=send_sem, recv_sem=recv_sem,
                                          device_id=(x, y, 1 - c), device_id_type=MESH)
        cp.start()
        cp.wait()

    hbm = pl.BlockSpec(memory_space=pl.ANY)
    return pl.pallas_call(
        body, in_specs=[hbm], out_specs=hbm, out_shape=jax.ShapeDtypeStruct((R, C), s.dtype),
        input_output_aliases={0: 0},
        scratch_shapes=[pltpu.SemaphoreType.DMA, pltpu.SemaphoreType.DMA], name="join_halves")(s)


def _reduce_scatter_start(swap_state, place_idx, after, name):
    g, r = _swap_wait(*swap_state, after, name + "_swap_wait")
    h, hb = _add_half(g, r, place_idx)
    send_sem, recv_sem, hb_thru, land_thru, token = _scatter_start(hb, name + "_start")
    return (h, send_sem, recv_sem, hb_thru, land_thru), token


def _reduce_scatter_finish(state, place_idx, after, name):
    h, send_sem, recv_sem, hb_thru, land_thru = state
    p = _scatter_wait(send_sem, recv_sem, hb_thru, land_thru, after, name + "_wait")
    return _join_halves(_add_chips(h, p, place_idx))


def _pack(parts):
    flat = jnp.concatenate([p.reshape(-1).astype(F32) for p in parts])
    n = flat.shape[0]
    rows = -(-n // (8 * LANES)) * 8
    return jnp.pad(flat, (0, rows * LANES - n)).reshape(rows, LANES)


def _unpack(buf, shapes):
    flat = buf.reshape(buf.shape[:-2] + (-1,))
    out, off = [], 0
    for s in shapes:
        n = math.prod(s)
        out.append(flat[..., off:off + n].reshape(buf.shape[:-2] + tuple(s)))
        off += n
    return out


def _mod_table(mod_x, mod_c):
    return jnp.stack([mod_x.reshape(6, 1, -1), mod_c.reshape(6, 1, -1)])


def kernel(x, c, ctx, c_ctx, ada_w, ada_b, ln_g, ln_b, ret_w_qkv, ret_w_g, ret_decay_logit, ret_w_o, mla_w_dq, mla_g_q, mla_w_uq, mla_w_dkv, mla_g_kv, mla_w_ukv, mla_w_o, ffn_w_in, ffn_w_out, loss_target, m_c_ctx, m_ada_w, m_ada_b, m_ln_g, m_ln_b, m_ret_w_qkv, m_ret_w_g, m_ret_decay_logit, m_ret_w_o, m_mla_w_dq, m_mla_g_q, m_mla_w_uq, m_mla_w_dkv, m_mla_g_kv, m_mla_w_ukv, m_mla_w_o, m_ffn_w_in, m_ffn_w_out, v_c_ctx, v_ada_w, v_ada_b, v_ln_g, v_ln_b, v_ret_w_qkv, v_ret_w_g, v_ret_decay_logit, v_ret_w_o, v_mla_w_dq, v_mla_g_q, v_mla_w_uq, v_mla_w_dkv, v_mla_g_kv, v_mla_w_ukv, v_mla_w_o, v_ffn_w_in, v_ffn_w_out):
    Nx, D = x.shape[1], x.shape[2]
    Nc = ctx.shape[1]
    T = Nx + Nc
    tr = 256 if (Nx % 256 == 0 and Nc % 256 == 0) else 128
    n_xt = Nx // tr
    C = RET_CHUNK
    ncx, ncc = Nx // C, Nc // C
    H = RET_HEADS
    dk, dv = D // H, 2 * D // H
    L = MLA_LORA
    HM = MLA_HEADS
    D6 = 6 * D
    n6 = D6 // N_CHIPS
    Dq = D // N_CHIPS
    xi, yi, ci = lax.axis_index("x"), lax.axis_index("y"), lax.axis_index("c")
    chip = 2 * xi + yi
    dev = 4 * xi + 2 * yi + ci
    as_index = lambda s: jnp.reshape(s, (1,)).astype(jnp.int32)
    place_idx = (as_index(chip), as_index(2 * (1 - xi) + yi), as_index(2 * xi + 1 - yi),
                 as_index(2 * (1 - xi) + 1 - yi), as_index(ci))
    chip_idx = place_idx[0]
    tile = dict(T=T, tr=tr, n_xt=n_xt)

    shapes1 = [(D,), (DEPTH, 2, Dq), (DEPTH, 2, Dq), (2, L // N_CHIPS), (2, L // N_CHIPS)]
    g1, _ = _all_gather8(_pack([c[0], ln_g, ln_b, mla_g_q, mla_g_kv]))
    c_all, lng_s, lnb_s, gq_s, gkv_s = _unpack(g1, shapes1)
    by_chip = lambda t: jnp.moveaxis(t[0::2], 0, -2).reshape(t.shape[1:-1] + (-1,))
    ln_g_full, ln_b_full = by_chip(lng_s), by_chip(lnb_s)
    gq_full, gkv_full = by_chip(gq_s), by_chip(gkv_s)

    cond = jnp.concatenate([c_all, c_ctx[None]], 0)
    silu_cond = cond * jax.nn.sigmoid(cond)
    s16 = jnp.pad(silu_cond, ((0, 16 - (N_DEV + 1)), (0, 0))).astype(BF16)
    mods = []
    for i in range(DEPTH):
        bias = lax.dynamic_slice_in_dim(ada_b[i], chip * n6, n6)
        mods.append(_mm_nn(s16, ada_w[i][None], name="ada_fwd")[:N_DEV + 1] + bias[None])
    g2, _ = _all_gather8(_pack([jnp.stack(mods)]))
    (mod_all,) = _unpack(g2, [(DEPTH, N_DEV + 1, n6)])
    mod_all = jnp.moveaxis(mod_all[0::2], 0, -2).reshape(DEPTH, N_DEV + 1, D6)
    mod_tabs = [_mod_table(lax.dynamic_index_in_dim(mod_all[i], dev, 0, False), mod_all[i, N_DEV])
                for i in range(DEPTH)]
    lnps = [[jnp.stack([ln_g_full[i, s], ln_b_full[i, s]])[:, None, :] for s in range(2)] for i in range(DEPTH)]

    def mla_pack(j):
        return jnp.concatenate([mla_w_dq[j], mla_w_uq[j], jnp.pad(mla_w_dkv[j], ((0, 0), (0, 64))),
                                mla_w_ukv[j], mla_w_o[j]], axis=1)

    mla_cols = [L, 3 * L // 2, L + LANES, 2 * L, D]

    def mla_unpack_weights(buf):
        offs = [0]
        for wdt in mla_cols:
            offs.append(offs[-1] + wdt)
        dq_, uq_, dkv_, ukv_, wo_ = [buf[:, :, offs[k]:offs[k + 1]] for k in range(5)]
        w_dq = dq_.reshape(D, L)
        w_dkv = dkv_.reshape(D, L + LANES)
        w_lora = jnp.concatenate([w_dq, w_dkv], axis=1)
        w_uq = jnp.moveaxis(uq_, 0, 1).reshape(L, HM, MLA_D_NOPE + MLA_D_ROPE)
        wq_cat = jnp.pad(w_uq, ((0, 0), (0, 0), (0, 2 * LANES - MLA_D_NOPE - MLA_D_ROPE))).reshape(L, HM * 2 * LANES)
        w_ukv = jnp.moveaxis(ukv_, 0, 1).reshape(L, HM, MLA_D_NOPE + MLA_D_V)
        w_kn = w_ukv[:, :, :MLA_D_NOPE].reshape(L, HM * MLA_D_NOPE)
        w_v = w_ukv[:, :, MLA_D_NOPE:].reshape(L, HM * MLA_D_V)
        w_o = wo_.reshape(HM * MLA_D_V, D)
        return dict(lora=w_lora[None], q=wq_cat[None], kn=w_kn[None], v=w_v[None], o=w_o[None])

    def mla_pack_grads(g):
        d_lora = g["lora"][0]
        d_dq = d_lora[:, :L].reshape(N_CHIPS, Dq, L)
        d_dkv = d_lora[:, L:].reshape(N_CHIPS, Dq, L + LANES)
        d_uq = g["q"][0].reshape(L, HM, 2 * LANES)[:, :, :MLA_D_NOPE + MLA_D_ROPE]
        d_uq = jnp.moveaxis(d_uq.reshape(L, N_CHIPS, -1), 1, 0)
        d_ukv = jnp.concatenate([g["kn"][0].reshape(L, HM, MLA_D_NOPE), g["v"][0].reshape(L, HM, MLA_D_V)], axis=2)
        d_ukv = jnp.moveaxis(d_ukv.reshape(L, N_CHIPS, -1), 1, 0)
        d_o = g["o"][0].reshape(N_CHIPS, L, D)
        return jnp.concatenate([d_dq, d_uq, d_dkv, d_ukv, d_o], axis=2)

    assert Dq == L, "the packed MLA buffer assumes D_MODEL / 4 == 512 rows per shard"

    def layer_shards(i):
        j = i // 2
        own = dict(qkv=ret_w_qkv[j], g=ret_w_g[j], o=ret_w_o[j]) if i % 2 == 0 else dict(mla=mla_pack(j))
        own.update(w_in=ffn_w_in[i], w_out=ffn_w_out[i])
        return own

    def arrived(i, n, after, lw):
        full = _gather_wait(*in_flight[i][n], after=after, name=f"gather_wait_{i}_{n}")
        if n == "qkv":
            lw.update(q=full[0:1], k=full[1:2], v=full[2:4])
        elif n == "g":
            lw.update(gf=full[0:2], gb=full[2:4])
        elif n == "o":
            lw.update(o=full.reshape(1, 2 * D, D))
        elif n == "mla":
            lw.update(mla_unpack_weights(full))
        elif n == "w_in":
            lw.update(a=full[0:2], b=full[2:4])
        else:
            lw.update(out=full.reshape(1, -1, D))

    W = [None] * DEPTH
    in_flight, start_tok = [None] * DEPTH, jnp.zeros((8, LANES), F32)
    for i in range(DEPTH):
        in_flight[i] = {}
        for n, w_l in layer_shards(i).items():
            send_sem, recv_sem, thru, tok = _gather_start(_place_own(w_l, chip_idx, start_tok), name=f"gather_start_{i}_{n}")
            in_flight[i][n] = (send_sem, recv_sem, thru)
            start_tok = start_tok + tok
    start_tok = start_tok[0, 0]

    ret_tabs = [_ret_tables(ret_decay_logit[j], Nx, Nc, dk, dv) for j in range(2)]
    mla_cos, mla_sin = _mla_tables(Nx, Nc)
    gqkv = [jnp.stack([gq_full[j], gkv_full[j]])[:, None, :] for j in range(2)]
    tq_f, tq_b = tr, tr

    h = jnp.concatenate([x[0], ctx[0]], axis=0) + start_tok
    u = _modulate(h, mod_tabs[0], **tile)
    saved = []
    for i in range(DEPTH):
        W[i] = lw = {}
        j, mod, sv = i // 2, mod_tabs[i], {}
        sv.update(h=h, u=u)
        if i % 2 == 0:
            arrived(i, "qkv", u, lw)
            q = _mm_nn(u, lw["q"], name="ret_q")
            k = _mm_nn(u, lw["k"], name="ret_k")
            v = _mm_nn(u, lw["v"], name="ret_v")
            arrived(i, "g", v, lw)
            gf = _mm_nn(u, lw["gf"], name="ret_gf")
            gb = _mm_nn(u, lw["gb"], name="ret_gb")
            o, states = _ret_fwd(q, k, v, ret_tabs[j], ncx, ncc)
            yg = _gate_fwd(gf, gb, o, dv=dv, **tile)
            arrived(i, "o", yg, lw)
            y = _mm_nn(yg, lw["o"], name="ret_o")
            sv.update(q=q, k=k, v=v, gf=gf, gb=gb, o=o, states=states, yg=yg)
        else:
            arrived(i, "mla", u, lw)
            lora = _mm_nn(u, lw["lora"], name="mla_lora")
            cqn, ckvn, kr = _lora_fwd(lora, gqkv[j], mla_cos, mla_sin, **tile)
            qcat = _mm_nn(cqn, lw["q"], name="mla_q")
            qrot = _qrope(qcat, mla_cos, mla_sin, False, **tile)
            kn = _mm_nn(ckvn, lw["kn"], out_dtype=BF16, name="mla_kn")
            vv = _mm_nn(ckvn, lw["v"], out_dtype=BF16, name="mla_v")
            att, lse = _attn_fwd(qrot, kn, vv, kr, Nx, tq_f)
            y = _mm_nn(att, lw["o"], name="mla_o")
            sv.update(lora=lora, cqn=cqn, ckvn=ckvn, kr=kr, qrot=qrot, kn=kn, vv=vv, att=att, lse=lse)
        h1, u2 = _ln_fwd(h, y, mod, lnps[i][0], 2, mod, (3, 4), **tile)
        arrived(i, "w_in", u2, lw)
        a = _mm_nn(u2, lw["a"], name="ffn_a")
        b = _mm_nn(u2, lw["b"], name="ffn_b")
        act = _swiglu_fwd(a, b, **tile)
        arrived(i, "w_out", act, lw)
        f = _mm_nn(act, lw["out"], name="ffn_out")
        last = i == DEPTH - 1
        h2, u_next = _ln_fwd(h1, f, mod, lnps[i][1], 5, None if last else mod_tabs[i + 1], (0, 1), **tile)
        sv.update(y=y, h1=h1, u2=u2, a=a, b=b, act=act, f=f)
        saved.append(sv)
        h, u = h2, u_next

    dh, err_cols = _loss_grad(h, loss_target[0], **tile)
    loss = lax.psum(0.5 * jnp.sum(err_cols) / D, ("x", "y", "c"))

    pending, swapping, order_tok = {}, [], [jnp.zeros((), F32)]

    def rs_advance(after):
        name, l, swap_state = swapping.pop(0)
        state, tok = _reduce_scatter_start(swap_state, place_idx, after, f"rs_{name}_{l}")
        pending[(name, l)] = state
        order_tok[0] = order_tok[0] + tok[0, 0]

    def rs_start(name, l, g):
        *swap_state, tok = _swap_start(g.reshape(N_CHIPS, -1, g.shape[-1]), f"rs_{name}_{l}_swap_start")
        if swapping:
            rs_advance(tok)
        swapping.append((name, l, swap_state))
        order_tok[0] = order_tok[0] + tok[0, 0]

    d_mods, d_lng, d_lnb = [None] * DEPTH, [None] * DEPTH, [None] * DEPTH
    d_gq, d_gkv, d_lam = [None] * 2, [None] * 2, [None] * 2
    for i in reversed(range(DEPTH)):
        j, lw, sv = i // 2, W[i], saved[i]
        mod = mod_tabs[i] + order_tok[0]
        dh1_res, df, dg_f, dlg1, dlb1 = _ln_bwd(sv["h1"], sv["f"], dh, mod, lnps[i][1], 5, **tile)
        dact = _mm_nt(df, lw["out"], name="ffn_out_nt")
        rs_start("ffn_w_out", i, _mm_tn(sv["act"], df, 1, name="ffn_out_tn").reshape(N_CHIPS, -1, D))
        da, db = _swiglu_bwd(sv["a"], sv["b"], dact, **tile)
        du2 = _mm_nt(db, lw["b"], add=_mm_nt(da, lw["a"], name="ffn_a_nt"), name="ffn_b_nt")
        g_in = _mm_tn(sv["u2"], da, 2, name="ffn_a_tn", slots=N_CHIPS)
        rs_start("ffn_w_in", i, _mm_tn(sv["u2"], db, 2, name="ffn_b_tn", slots=N_CHIPS, slot0=2, into=g_in))
        mod = mod_tabs[i] + order_tok[0]
        dh1, dsc_f, dsh_f = _mod_bwd(dh1_res, du2, sv["h1"], mod, 4, **tile)
        dh_res, dy, dg_a, dlg0, dlb0 = _ln_bwd(sv["h"], sv["y"], dh1, mod, lnps[i][0], 2, **tile)
        uu = sv["u"]
        if i % 2 == 0:
            dyg = _mm_nt(dy, lw["o"], name="ret_o_nt")
            rs_start("ret_w_o", j, _mm_tn(sv["yg"], dy, 1, name="ret_o_tn").reshape(N_CHIPS, -1, D))
            dgf, dgb, do_f, do_b = _gate_bwd(sv["gf"], sv["gb"], sv["o"], dyg, dv=dv, **tile)
            dq2, dk2, dv2, dlam = _ret_bwd(sv["q"], sv["k"], sv["v"], jnp.stack([do_f, do_b]), sv["states"],
                                           ret_tabs[j], ncx, ncc)
            dq, dkk, dvv = _add_dirs(dq2, dk2, dv2, **tile)
            du = _mm_nt(dq, lw["q"], name="ret_q_nt")
            du = _mm_nt(dkk, lw["k"], add=du, name="ret_k_nt")
            du = _mm_nt(dvv, lw["v"], add=du, name="ret_v_nt")
            du = _mm_nt(dgf, lw["gf"], add=du, name="ret_gf_nt")
            du = _mm_nt(dgb, lw["gb"], add=du, name="ret_gb_nt")
            g_qkv = _mm_tn(uu, dq, 1, name="ret_q_tn", slots=N_CHIPS)
            g_qkv = _mm_tn(uu, dkk, 1, name="ret_k_tn", slots=N_CHIPS, slot0=1, into=g_qkv)
            rs_start("ret_w_qkv", j, _mm_tn(uu, dvv, 2, name="ret_v_tn", slots=N_CHIPS, slot0=2, into=g_qkv))
            g_g = _mm_tn(uu, dgf, 2, name="ret_gf_tn", slots=N_CHIPS)
            rs_start("ret_w_g", j, _mm_tn(uu, dgb, 2, name="ret_gb_tn", slots=N_CHIPS, slot0=2, into=g_g))
            d_lam[j] = dlam[:, :, 0, 0]
        else:
            datt = _mm_nt(dy, lw["o"], name="mla_o_nt").astype(BF16)
            gm = dict(o=_mm_tn(sv["att"], dy, 1, name="mla_o_tn"))
            dqcat, dkn, dvv, dkr = _attn_bwd(sv["qrot"], sv["kn"], sv["vv"], sv["kr"], datt, sv["lse"], Nx, tq_b)
            dqraw = _qrope(dqcat, mla_cos, mla_sin, True, **tile)
            dcqn = _mm_nt(dqraw, lw["q"], name="mla_q_nt")
            gm["q"] = _mm_tn(sv["cqn"], dqraw, 1, name="mla_q_tn")
            dckvn = _mm_nt(dvv, lw["v"], add=_mm_nt(dkn, lw["kn"], name="mla_kn_nt"), name="mla_v_nt")
            gm["kn"] = _mm_tn(sv["ckvn"], dkn, 1, name="mla_kn_tn")
            gm["v"] = _mm_tn(sv["ckvn"], dvv, 1, name="mla_v_tn")
            dlora, dgq, dgkv = _lora_bwd(sv["lora"], dcqn, dckvn, dkr, gqkv[j], mla_cos, mla_sin, **tile)
            du = _mm_nt(dlora, lw["lora"], name="mla_lora_nt")
            gm["lora"] = _mm_tn(uu, dlora, 1, name="mla_lora_tn")
            rs_start("mla", j, mla_pack_grads(gm))
            d_gq[j], d_gkv[j] = dgq[0], dgkv[0]
        mod = mod_tabs[i] + order_tok[0]
        dh, dsc_a, dsh_a = _mod_bwd(dh_res, du, sv["h"], mod, 1, **tile)
        d_mods[i] = jnp.concatenate([dsh_a, dsc_a, dg_a, dsh_f, dsc_f, dg_f], axis=2)[:, 0, :]
        d_lng[i] = jnp.concatenate([dlg0, dlg1], 0)
        d_lnb[i] = jnp.concatenate([dlb0, dlb1], 0)

    rs_advance(dh)
    grad_x = dh[:Nx][None]

    d_mods = jnp.stack(d_mods)
    dlogit = jnp.stack([d_lam[j] * jax.nn.sigmoid(-ret_decay_logit[j]) for j in range(2)])
    shapes3 = [(DEPTH, D6), (DEPTH, D6), (DEPTH, 2, D), (DEPTH, 2, D), (2, L), (2, L), (2, 2, H)]
    g3, s3 = _all_gather8(_pack([d_mods[:, 0], d_mods[:, 1], jnp.stack(d_lng), jnp.stack(d_lnb),
                                 jnp.stack(d_gq), jnp.stack(d_gkv), dlogit]))
    dmod_x_all = _unpack(g3, shapes3)[0]
    dmod_x_sum, dmod_c_sum, g_lng, g_lnb, g_gq, g_gkv, g_decay = _unpack(s3, shapes3)
    grad_ada_b = dmod_x_sum + dmod_c_sum
    dmod9 = jnp.concatenate([jnp.moveaxis(dmod_x_all, 0, 1), dmod_c_sum[:, None]], axis=1)
    dmod16 = jnp.pad(lax.dynamic_slice_in_dim(dmod9, chip * n6, n6, axis=2), ((0, 0), (0, 16 - (N_DEV + 1)), (0, 0)))
    dmod16 = dmod16.astype(BF16)
    grad_ada_w = jnp.stack([_mm_tn(s16, dmod16[i], 1, name="ada_tn")[0] for i in range(DEPTH)])
    dsilu = _mm_nt(jnp.moveaxis(dmod16, 0, 1).reshape(16, DEPTH * n6), ada_w, name="ada_nt")
    _, s4 = _all_gather8(_pack([dsilu[N_DEV]]))
    sg = jax.nn.sigmoid(c_ctx)
    grad_c_ctx = (0.5 * s4.reshape(-1)[:D]) * (sg * (1.0 + c_ctx * (1.0 - sg)))

    my_cols = lambda t, n: lax.dynamic_slice_in_dim(t, chip * n, n, axis=t.ndim - 1)
    grad_ln_g, grad_ln_b = my_cols(g_lng, Dq), my_cols(g_lnb, Dq)
    grad_gq, grad_gkv = my_cols(g_gq, L // N_CHIPS), my_cols(g_gkv, L // N_CHIPS)

    def rs_done(name, l):
        return _reduce_scatter_finish(pending[(name, l)], place_idx, dh, f"rs_{name}_{l}")

    def rs(name, n_layers, shard_shape):
        return jnp.stack([rs_done(name, l).reshape(shard_shape) for l in range(n_layers)])

    grad_ret_w_qkv = rs("ret_w_qkv", 2, ret_w_qkv.shape[1:])
    grad_ret_w_g = rs("ret_w_g", 2, ret_w_g.shape[1:])
    grad_ret_w_o = rs("ret_w_o", 2, ret_w_o.shape[1:])
    grad_ffn_w_in = rs("ffn_w_in", DEPTH, ffn_w_in.shape[1:])
    grad_ffn_w_out = rs("ffn_w_out", DEPTH, ffn_w_out.shape[1:])
    mla_red = [rs_done("mla", l) for l in range(2)]
    offs = [0]
    for wdt in mla_cols:
        offs.append(offs[-1] + wdt)
    mla_parts = [jnp.stack([mla_red[l][:, offs[k]:offs[k + 1]] for l in range(2)]) for k in range(5)]
    grad_mla_w_dq, grad_mla_w_uq, grad_mla_w_ukv, grad_mla_w_o = mla_parts[0], mla_parts[1], mla_parts[3], mla_parts[4]
    grad_mla_w_dkv = mla_parts[2][:, :, :L + MLA_D_ROPE]

    grads = [grad_c_ctx, grad_ada_w, grad_ada_b, grad_ln_g, grad_ln_b, grad_ret_w_qkv, grad_ret_w_g, g_decay,
             grad_ret_w_o, grad_mla_w_dq, grad_gq, grad_mla_w_uq, grad_mla_w_dkv, grad_gkv, grad_mla_w_ukv,
             grad_mla_w_o, grad_ffn_w_in, grad_ffn_w_out]
    weights = [c_ctx, ada_w, ada_b, ln_g, ln_b, ret_w_qkv, ret_w_g, ret_decay_logit, ret_w_o, mla_w_dq, mla_g_q,
               mla_w_uq, mla_w_dkv, mla_g_kv, mla_w_ukv, mla_w_o, ffn_w_in, ffn_w_out]
    ms = [m_c_ctx, m_ada_w, m_ada_b, m_ln_g, m_ln_b, m_ret_w_qkv, m_ret_w_g, m_ret_decay_logit, m_ret_w_o,
          m_mla_w_dq, m_mla_g_q, m_mla_w_uq, m_mla_w_dkv, m_mla_g_kv, m_mla_w_ukv, m_mla_w_o, m_ffn_w_in, m_ffn_w_out]
    vs = [v_c_ctx, v_ada_w, v_ada_b, v_ln_g, v_ln_b, v_ret_w_qkv, v_ret_w_g, v_ret_decay_logit, v_ret_w_o,
          v_mla_w_dq, v_mla_g_q, v_mla_w_uq, v_mla_w_dkv, v_mla_g_kv, v_mla_w_ukv, v_mla_w_o, v_ffn_w_in, v_ffn_w_out]
    upd = [_adamw(w_, g_, m_, v_) for w_, g_, m_, v_ in zip(weights, grads, ms, vs)]
    return (loss, grad_x, *grads, *[u_[0] for u_ in upd], *[u_[1] for u_ in upd], *[u_[2] for u_ in upd])
```

```python
import functools
import math

import jax
import jax.numpy as jnp
from jax import lax
from jax.experimental import pallas as pl
from jax.experimental.pallas import tpu as pltpu

F32, BF16 = jnp.float32, jnp.bfloat16
MESH = pl.DeviceIdType.MESH
V7X_VMEM_LIMIT_BYTES = 56 * 1024 * 1024
LANES = 128

GRID_W = 64
RET_HEADS = 8
RET_CHUNK = 128
RET_HEAD_GROUP = 8
RET_ROPE_BASE = 10000.0
GN_EPS = 1e-6
MLA_HEADS = 16
MLA_LORA = 512
MLA_D_NOPE = 128
MLA_D_ROPE = 64
MLA_D_V = 128
MLA_SCALE = (MLA_D_NOPE + MLA_D_ROPE) ** -0.5
AXIAL_ROPE_BASE = 10000.0
RMS_EPS = 1e-6
DEPTH = 4
DEEPNORM_ALPHA = (2 * DEPTH) ** 0.25
LN_EPS = 1e-5
ADAM_LR, ADAM_B1, ADAM_B2, ADAM_EPS, ADAM_WD, ADAM_STEP = 0.001, 0.9, 0.999, 1e-08, 0.01, 10
N_CHIPS = 4
N_DEV = 8


def _pick(dim, target, mult):
    best = None
    for d in range(mult, min(dim, target) + 1, mult):
        if dim % d == 0:
            best = d
    return dim if best is None else best


def _params(*sem):
    return pltpu.CompilerParams(dimension_semantics=sem, vmem_limit_bytes=V7X_VMEM_LIMIT_BYTES)


def _sigmoid(x):
    return 1.0 / (1.0 + jnp.exp(-x))


def _accumulate(step, nsteps, acc, part, write):
    if nsteps == 1:
        write(part)
        return

    @pl.when(step == 0)
    def _():
        acc[...] = part

    @pl.when(jnp.logical_and(step > 0, step < nsteps - 1))
    def _():
        acc[...] += part

    @pl.when(step == nsteps - 1)
    def _():
        write(acc[...] + part)


def _slots(w):
    return w if isinstance(w, tuple) else (w, 0, w.shape[0])


def _mm_nn(a, w, out_dtype=F32, name="mm_nn"):
    M, K = a.shape
    w, slot0, J = _slots(w)
    _, K2, n = w.shape
    assert K == K2
    tn = _pick(n, 1408, LANES)
    tk = _pick(K, 2048, LANES)
    tm = _pick(M, max(16, (6 << 20) // (tn * 4)), 16)
    npj, nk = n // tn, K // tk

    def body(a_ref, w_ref, o_ref, acc):
        part = jnp.dot(a_ref[...].astype(BF16), w_ref[...].astype(BF16), preferred_element_type=F32)

        def write(total):
            o_ref[...] = total.astype(o_ref.dtype)

        _accumulate(pl.program_id(2), nk, acc, part, write)

    return pl.pallas_call(
        body, grid=(M // tm, J * npj, nk),
        in_specs=[pl.BlockSpec((tm, tk), lambda i, j, k: (i, k)),
                  pl.BlockSpec((None, tk, tn), lambda i, j, k: (slot0 + j // npj, k, j % npj))],
        out_specs=pl.BlockSpec((tm, tn), lambda i, j, k: (i, j)),
        out_shape=jax.ShapeDtypeStruct((M, J * n), out_dtype),
        scratch_shapes=[pltpu.VMEM((tm, tn), F32)],
        compiler_params=_params("parallel", "parallel", "arbitrary"), name=name)(a, w)


def _mm_nt(dy, w, add=None, name="mm_nt"):
    M, N = dy.shape
    w, slot0, J = _slots(w)
    _, K, n = w.shape
    assert N == J * n
    tko = _pick(K, 1408, LANES)
    tc = _pick(n, 2048, LANES)
    tm = _pick(M, max(16, (6200 << 10) // (tko * 4)), 16)
    npj = n // tc
    nc = J * npj
    has_add = add is not None

    def body(*refs):
        if has_add:
            dy_ref, w_ref, add_ref, o_ref, acc = refs
        else:
            dy_ref, w_ref, o_ref, acc = refs
        part = lax.dot_general(dy_ref[...].astype(BF16), w_ref[...].astype(BF16),
                               (((1,), (1,)), ((), ())), preferred_element_type=F32)

        def write(total):
            o_ref[...] = total + add_ref[...] if has_add else total

        _accumulate(pl.program_id(2), nc, acc, part, write)

    in_specs = [pl.BlockSpec((tm, tc), lambda i, ko, c: (i, c)),
                pl.BlockSpec((None, tko, tc), lambda i, ko, c: (slot0 + c // npj, ko, c % npj))]
    args = [dy, w]
    if has_add:
        in_specs.append(pl.BlockSpec((tm, tko), lambda i, ko, c: (i, ko)))
        args.append(add)
    return pl.pallas_call(
        body, grid=(M // tm, K // tko, nc), in_specs=in_specs,
        out_specs=pl.BlockSpec((tm, tko), lambda i, ko, c: (i, ko)),
        out_shape=jax.ShapeDtypeStruct((M, K), F32),
        scratch_shapes=[pltpu.VMEM((tm, tko), F32)],
        compiler_params=_params("parallel", "parallel", "arbitrary"), name=name)(*args)


def _mm_tn(a, dy, J, name="mm_tn", slots=None, slot0=0, into=None):
    M, K = a.shape
    M2, N = dy.shape
    assert M == M2 and N % J == 0
    n = N // J
    tko = _pick(K, 1024, LANES)
    tn = _pick(n, 1408, LANES)
    tmc = _pick(M, 2176, 16)
    npj, nm = n // tn, M // tmc

    def body(*refs):
        a_ref, dy_ref, o_ref, acc = refs[0], refs[1], refs[-2], refs[-1]
        part = lax.dot_general(a_ref[...].astype(BF16), dy_ref[...].astype(BF16),
                               (((0,), (0,)), ((), ())), preferred_element_type=F32)

        def write(total):
            o_ref[...] = total

        _accumulate(pl.program_id(2), nm, acc, part, write)

    in_specs = [pl.BlockSpec((tmc, tko), lambda ko, j, m: (m, ko)),
                pl.BlockSpec((tmc, tn), lambda ko, j, m: (m, j))]
    args, aliases = [a, dy], {}
    if into is not None:
        in_specs.append(pl.BlockSpec(memory_space=pl.ANY))
        args.append(into)
        aliases = {2: 0}
    return pl.pallas_call(
        body, grid=(K // tko, J * npj, nm), in_specs=in_specs,
        out_specs=pl.BlockSpec((None, tko, tn), lambda ko, j, m: (slot0 + j // npj, ko, j % npj)),
        out_shape=jax.ShapeDtypeStruct((slots or J, K, n), F32), input_output_aliases=aliases,
        scratch_shapes=[pltpu.VMEM((tko, tn), F32)],
        compiler_params=_params("parallel", "parallel", "arbitrary"), name=name)(*args)


def _rowwise(name, fn, T, tr, n_xt, row_in, grp_in=(), const_in=(), row_out=(), gsum_w=(), tsum_w=(), ncol=1):
    nt = T // tr
    n_in = len(row_in) + len(grp_in) + len(const_in)
    n_ro, n_gs = len(row_out), len(gsum_w)
    assert ncol == 1 or not (gsum_w or tsum_w)

    def body(*refs):
        t = pl.program_id(0)
        vals = [r[...] for r in refs[:n_in]]
        rv = vals[:len(row_in)]
        gv = vals[len(row_in):len(row_in) + len(grp_in)]
        cv = vals[len(row_in) + len(grp_in):]
        ro, gs, ts = fn(rv, gv, cv)
        outs = refs[n_in:]
        for ref, val in zip(outs[:n_ro], ro):
            ref[...] = val.astype(ref.dtype)
        first_g = jnp.logical_or(t == 0, t == n_xt)
        for ref, val, first in ([(r, v, first_g) for r, v in zip(outs[n_ro:n_ro + n_gs], gs)]
                                + [(r, v, t == 0) for r, v in zip(outs[n_ro + n_gs:], ts)]):
            s = jnp.sum(val, axis=0, keepdims=True)

            @pl.when(first)
            def _(ref=ref, s=s):
                ref[...] = s

            @pl.when(jnp.logical_not(first))
            def _(ref=ref, s=s):
                ref[...] += s

    in_specs, args = [], []
    for spec in row_in:
        arr, width = spec[:2]
        lead = spec[2] if len(spec) > 2 else None
        step = spec[3] if len(spec) > 3 else 1
        if lead is not None:
            in_specs.append(pl.BlockSpec((None, tr, width), lambda t, cb, lead=lead, step=step: (lead, t, cb * step)))
        else:
            in_specs.append(pl.BlockSpec((tr, width), lambda t, cb, step=step: (t, cb * step)))
        args.append(arr)
    for arr in grp_in:
        in_specs.append(pl.BlockSpec((None,) + arr.shape[1:], lambda t, cb: (jnp.where(t >= n_xt, 1, 0), 0, 0, 0)))
        args.append(arr)
    for arr in const_in:
        in_specs.append(pl.BlockSpec(arr.shape, lambda t, cb: (0, 0, 0)))
        args.append(arr)
    out_specs, out_shape = [], []
    for wtot, wblk, dt in row_out:
        out_specs.append(pl.BlockSpec((tr, wblk), lambda t, cb: (t, cb)))
        out_shape.append(jax.ShapeDtypeStruct((T, wtot), dt))
    for w in gsum_w:
        out_specs.append(pl.BlockSpec((None, 1, w), lambda t, cb: (jnp.where(t >= n_xt, 1, 0), 0, 0)))
        out_shape.append(jax.ShapeDtypeStruct((2, 1, w), F32))
    for w in tsum_w:
        out_specs.append(pl.BlockSpec((1, w), lambda t, cb: (0, 0)))
        out_shape.append(jax.ShapeDtypeStruct((1, w), F32))
    sem = ("arbitrary", "arbitrary") if (gsum_w or tsum_w) else ("parallel", "parallel")
    return pl.pallas_call(body, grid=(nt, ncol), in_specs=in_specs, out_specs=out_specs, out_shape=out_shape,
                          compiler_params=_params(*sem), name=name)(*args)


def _modulate(h, mod, T, tr, n_xt):
    D = h.shape[1]

    def fn(rv, gv, cv):
        m = gv[0]
        return [rv[0] * (1.0 + m[1]) + m[0]], [], []

    return _rowwise("modulate", fn, T, tr, n_xt, [(h, D)], [mod], row_out=[(D, D, BF16)])[0]


def _ln_stats(r):
    mu = jnp.mean(r, axis=-1, keepdims=True)
    xc = r - mu
    var = jnp.mean(xc * xc, axis=-1, keepdims=True)
    rstd = lax.rsqrt(var + LN_EPS)
    return xc * rstd, rstd


def _ln_fwd(h, y, mod, lnp, gate_row, mod_next, next_rows, T, tr, n_xt):
    D = h.shape[1]
    with_u = mod_next is not None

    def fn(rv, gv, cv):
        r = DEEPNORM_ALPHA * rv[0] + gv[0][gate_row] * rv[1]
        xhat, _ = _ln_stats(r)
        out = xhat * cv[0][0] + cv[0][1]
        if not with_u:
            return [out], [], []
        mn = gv[1]
        return [out, out * (1.0 + mn[next_rows[1]]) + mn[next_rows[0]]], [], []

    grp = [mod, mod_next] if with_u else [mod]
    outs = [(D, D, F32), (D, D, BF16)] if with_u else [(D, D, F32)]
    res = _rowwise("ln_fwd", fn, T, tr, n_xt, [(h, D), (y, D)], grp, [lnp], row_out=outs)
    return (res[0], res[1]) if with_u else (res[0], None)


def _ln_bwd(h, y, dout, mod, lnp, gate_row, T, tr, n_xt):
    D = h.shape[1]

    def fn(rv, gv, cv):
        g = gv[0][gate_row]
        r = DEEPNORM_ALPHA * rv[0] + g * rv[1]
        xhat, rstd = _ln_stats(r)
        dxh = rv[2] * cv[0][0]
        m1 = jnp.mean(dxh, axis=-1, keepdims=True)
        m2 = jnp.mean(dxh * xhat, axis=-1, keepdims=True)
        dr = rstd * (dxh - m1 - xhat * m2)
        return [DEEPNORM_ALPHA * dr, g * dr], [dr * rv[1]], [rv[2] * xhat, rv[2]]

    return _rowwise("ln_bwd", fn, T, tr, n_xt, [(h, D), (y, D), (dout, D)], [mod], [lnp],
                    row_out=[(D, D, F32), (D, D, BF16)], gsum_w=[D], tsum_w=[D, D])


def _mod_bwd(dh_res, du, h, mod, scale_row, T, tr, n_xt):
    D = h.shape[1]

    def fn(rv, gv, cv):
        return [rv[0] + rv[1] * (1.0 + gv[0][scale_row])], [rv[1] * rv[2], rv[1]], []

    return _rowwise("mod_bwd", fn, T, tr, n_xt, [(dh_res, D), (du, D), (h, D)], [mod],
                    row_out=[(D, D, F32)], gsum_w=[D, D])


def _swiglu_fwd(a, b, T, tr, n_xt):
    F = a.shape[1]
    wc = _pick(F, 1408, LANES)

    def fn(rv, gv, cv):
        return [rv[0] * _sigmoid(rv[0]) * rv[1]], [], []

    return _rowwise("swiglu_fwd", fn, T, tr, n_xt, [(a, wc), (b, wc)], row_out=[(F, wc, BF16)], ncol=F // wc)[0]


def _swiglu_bwd(a, b, dact, T, tr, n_xt):
    F = a.shape[1]
    wc = _pick(F, 1408, LANES)

    def fn(rv, gv, cv):
        av, bv, dv = rv
        sg = _sigmoid(av)
        return [dv * bv * (sg * (1.0 + av * (1.0 - sg))), dv * av * sg], [], []

    return _rowwise("swiglu_bwd", fn, T, tr, n_xt, [(a, wc), (b, wc), (dact, wc)],
                    row_out=[(F, wc, BF16), (F, wc, BF16)], ncol=F // wc)


def _gn(o):
    mu = jnp.mean(o, axis=-1, keepdims=True)
    xc = o - mu
    var = jnp.mean(xc * xc, axis=-1, keepdims=True)
    rstd = lax.rsqrt(var + GN_EPS)
    return xc * rstd, rstd


def _gate_fwd(gf, gb, o, T, tr, n_xt, dv):
    W = gf.shape[1]

    def fn(rv, gv, cv):
        xf, _ = _gn(rv[2])
        xb, _ = _gn(rv[3])
        return [rv[0] * _sigmoid(rv[0]) * xf + rv[1] * _sigmoid(rv[1]) * xb], [], []

    return _rowwise("gate_fwd", fn, T, tr, n_xt, [(gf, dv), (gb, dv), (o, dv, 0), (o, dv, 1)],
                    row_out=[(W, dv, BF16)], ncol=W // dv)[0]


def _gate_bwd(gf, gb, o, dy, T, tr, n_xt, dv):
    W = gf.shape[1]

    def fn(rv, gv, cv):
        outs_g, outs_o = [], []
        for g, ov in ((rv[0], rv[2]), (rv[1], rv[3])):
            xh, rstd = _gn(ov)
            sg = _sigmoid(g)
            outs_g.append(rv[4] * xh * (sg * (1.0 + g * (1.0 - sg))))
            dxh = rv[4] * g * sg
            m1 = jnp.mean(dxh, axis=-1, keepdims=True)
            m2 = jnp.mean(dxh * xh, axis=-1, keepdims=True)
            outs_o.append(rstd * (dxh - m1 - xh * m2))
        return outs_g + outs_o, [], []

    return _rowwise("gate_bwd", fn, T, tr, n_xt,
                    [(gf, dv), (gb, dv), (o, dv, 0), (o, dv, 1), (dy, dv)],
                    row_out=[(W, dv, BF16)] * 4, ncol=W // dv)


def _add_dirs(dq, dk, dv_, T, tr, n_xt):
    ws = [dq.shape[2], dk.shape[2], dv_.shape[2]]

    def fn(rv, gv, cv):
        return [rv[0] + rv[1], rv[2] + rv[3], rv[4] + rv[5]], [], []

    row_in = []
    for arr, w in zip((dq, dk, dv_), ws):
        row_in += [(arr, w, 0), (arr, w, 1)]
    return _rowwise("add_dirs", fn, T, tr, n_xt, row_in, row_out=[(w, w, BF16) for w in ws])


def _swap16(x):
    lane = lax.broadcasted_iota(jnp.int32, x.shape, x.ndim - 1)
    return jnp.where(lane % 32 < 16, pltpu.roll(x, LANES - 16, x.ndim - 1), pltpu.roll(x, 16, x.ndim - 1))


def _rope2d(x, cos, sin, transpose):
    if transpose:
        return x * cos + _swap16(x * sin)
    return x * cos + _swap16(x) * sin


def _rms(x, g):
    rstd = lax.rsqrt(jnp.mean(x * x, axis=-1, keepdims=True) + RMS_EPS)
    return x * rstd, rstd


def _lora_fwd(lora, gq_gkv, cos, sin, T, tr, n_xt):
    L = MLA_LORA

    def fn(rv, gv, cv):
        x = rv[0]
        xq, _ = _rms(x[:, :L], None)
        xkv, _ = _rms(x[:, L:2 * L], None)
        return [xq * cv[0][0], xkv * cv[0][1], _rope2d(x[:, 2 * L:], rv[1], rv[2], False)], [], []

    return _rowwise("lora_fwd", fn, T, tr, n_xt, [(lora, 2 * L + LANES), (cos, LANES), (sin, LANES)],
                    const_in=[gq_gkv], row_out=[(L, L, BF16), (L, L, BF16), (LANES, LANES, BF16)])


def _lora_bwd(lora, dq, dkv, dkr, gq_gkv, cos, sin, T, tr, n_xt):
    L = MLA_LORA

    def fn(rv, gv, cv):
        x = rv[0]
        outs, sums = [], []
        for xs, dy, g in ((x[:, :L], rv[1], cv[0][0]), (x[:, L:2 * L], rv[2], cv[0][1])):
            xh, rstd = _rms(xs, None)
            dxh = dy * g
            outs.append(rstd * (dxh - xh * jnp.mean(dxh * xh, axis=-1, keepdims=True)))
            sums.append(dy * xh)
        outs.append(_rope2d(rv[3], rv[4], rv[5], True))
        return [jnp.concatenate(outs, axis=1)], [], sums

    W = 2 * L + LANES
    return _rowwise("lora_bwd", fn, T, tr, n_xt,
                    [(lora, W), (dq, L), (dkv, L), (dkr, LANES), (cos, LANES), (sin, LANES)],
                    const_in=[gq_gkv], row_out=[(W, W, BF16)], tsum_w=[L, L])


def _qrope(q, cos, sin, transpose, T, tr, n_xt):
    W = q.shape[1]

    def fn(rv, gv, cv):
        x = rv[0] * MLA_SCALE
        parts = []
        for hd in range(W // (2 * LANES)):
            lo = hd * 2 * LANES
            parts += [x[:, lo:lo + LANES], _rope2d(x[:, lo + LANES:lo + 2 * LANES], rv[1], rv[2], transpose)]
        return [jnp.concatenate(parts, axis=1)], [], []

    return _rowwise("qrope_bwd" if transpose else "qrope_fwd", fn, T, tr, n_xt,
                    [(q, W), (cos, LANES), (sin, LANES)], row_out=[(W, W, BF16)])[0]


def _loss_grad(h, target, T, tr, n_xt):
    D = h.shape[1]
    nt = T // tr

    def body(h_ref, t_ref, dh_ref, s_ref):
        t = pl.program_id(0)
        diff = jnp.where(t < n_xt, h_ref[...] - t_ref[...], 0.0)
        dh_ref[...] = diff * (1.0 / D)
        s = jnp.sum(diff * diff, axis=0, keepdims=True)

        @pl.when(t == 0)
        def _():
            s_ref[...] = s

        @pl.when(t != 0)
        def _():
            s_ref[...] += s

    return pl.pallas_call(
        body, grid=(nt,),
        in_specs=[pl.BlockSpec((tr, D), lambda t: (t, 0)),
                  pl.BlockSpec((tr, D), lambda t: (jnp.minimum(t, n_xt - 1), 0))],
        out_specs=[pl.BlockSpec((tr, D), lambda t: (t, 0)), pl.BlockSpec((1, D), lambda t: (0, 0))],
        out_shape=[jax.ShapeDtypeStruct((T, D), F32), jax.ShapeDtypeStruct((1, D), F32)],
        compiler_params=_params("arbitrary"), name="loss_grad")(h, target)


def _adamw(w, g, m, v):
    shape = w.shape
    C = shape[-1] if w.ndim > 1 else shape[0]
    R = w.size // C
    tr = _pick(R, max(8, (2 << 20) // (C * 4)), 8)
    c1 = 1.0 - ADAM_B1 ** ADAM_STEP
    c2 = 1.0 - ADAM_B2 ** ADAM_STEP

    def body(w_ref, g_ref, m_ref, v_ref, d_ref, nm_ref, nv_ref):
        gv = g_ref[...]
        nm = ADAM_B1 * m_ref[...] + (1.0 - ADAM_B1) * gv
        nv = ADAM_B2 * v_ref[...] + (1.0 - ADAM_B2) * (gv * gv)
        nm_ref[...] = nm
        nv_ref[...] = nv
        d_ref[...] = -ADAM_LR * ((nm / c1) / (jnp.sqrt(nv / c2) + ADAM_EPS) + ADAM_WD * w_ref[...])

    spec = pl.BlockSpec((tr, C), lambda i: (i, 0))
    outs = pl.pallas_call(
        body, grid=(R // tr,), in_specs=[spec] * 4, out_specs=[spec] * 3,
        out_shape=[jax.ShapeDtypeStruct((R, C), F32)] * 3,
        compiler_params=_params("parallel"), name="adamw")(*[t.reshape(R, C) for t in (w, g, m, v)])
    return tuple(o.reshape(shape) for o in outs)


def _ret_chunk(d, s, ncx, ncc):
    fwd = jnp.where(s < ncc, ncx + s, s - ncc)
    bwd = jnp.where(s < ncc, ncx + ncc - 1 - s, ncx - 1 - (s - ncc))
    return jnp.where(d == 0, fwd, bwd)


def _rot_half(x, cos, sin, transpose):
    half = x.shape[-1] // 2
    if transpose:
        return x * cos + pltpu.roll(x * sin, half, x.ndim - 1)
    return x * cos + pltpu.roll(x, half, x.ndim - 1) * sin


def _dot_nt(a, b):
    return lax.dot_general(a, b, (((1,), (1,)), ((), ())), preferred_element_type=F32)


def _dot_tn(a, b):
    return lax.dot_general(a, b, (((0,), (0,)), ((), ())), preferred_element_type=F32)


def _dot(a, b):
    return jnp.dot(a, b, preferred_element_type=F32)


def _ret_fwd(q, k, v, tabs, ncx, ncc):
    T, C, H = q.shape[0], RET_CHUNK, RET_HEADS
    dk, dv = q.shape[1] // H, v.shape[1] // H
    ns = ncx + ncc
    kscale = dk ** -0.5

    G = RET_HEAD_GROUP

    def body(q_ref, k_ref, v_ref, cos_ref, sin_ref, intra_ref, qd_ref, kd_ref, cd_ref, o_ref, st_ref, s_scr):
        @pl.when(pl.program_id(2) == 0)
        def _():
            s_scr[...] = jnp.zeros_like(s_scr)

        cos, sin = cos_ref[...], sin_ref[...]
        for g in range(G):
            ks_, vs_ = slice(g * dk, (g + 1) * dk), slice(g * dv, (g + 1) * dv)
            qv = _rot_half(q_ref[:, ks_], cos, sin, False)
            kv = _rot_half(k_ref[:, ks_], cos, sin, False) * kscale
            vb = v_ref[:, vs_].astype(BF16)
            S = s_scr[g]
            st_ref[g] = S
            scores = _dot_nt(qv.astype(BF16), kv.astype(BF16)) * intra_ref[g]
            o_ref[:, vs_] = _dot(scores.astype(BF16), vb) + _dot((qv * qd_ref[g]).astype(BF16), S.astype(BF16))
            s_scr[g] = S * cd_ref[g] + _dot_tn((kv * kd_ref[g]).astype(BF16), vb)

    chunk = lambda d, h, s: _ret_chunk(d, s, ncx, ncc)
    tab = lambda shape: pl.BlockSpec((None, G) + shape, lambda d, h, s: (d, h, 0, 0))
    return pl.pallas_call(
        body, grid=(2, H // G, ns),
        in_specs=[pl.BlockSpec((C, G * dk), lambda d, h, s: (chunk(d, h, s), h)),
                  pl.BlockSpec((C, G * dk), lambda d, h, s: (chunk(d, h, s), h)),
                  pl.BlockSpec((C, G * dv), lambda d, h, s: (chunk(d, h, s), h)),
                  pl.BlockSpec((C, dk), lambda d, h, s: (chunk(d, h, s), 0)),
                  pl.BlockSpec((C, dk), lambda d, h, s: (chunk(d, h, s), 0)),
                  tab((C, C)), tab((C, dk)), tab((C, dk)), tab((1, dv))],
        out_specs=[pl.BlockSpec((None, C, G * dv), lambda d, h, s: (d, chunk(d, h, s), h)),
                   pl.BlockSpec((None, G, None, dk, dv), lambda d, h, s: (d, h, s, 0, 0))],
        out_shape=[jax.ShapeDtypeStruct((2, T, H * dv), F32), jax.ShapeDtypeStruct((2, H, ns, dk, dv), F32)],
        scratch_shapes=[pltpu.VMEM((G, dk, dv), F32)],
        compiler_params=_params("parallel", "parallel", "arbitrary"), name="ret_fwd",
    )(q, k, v, tabs["cos"], tabs["sin"], tabs["intra"], tabs["qd"], tabs["kd"], tabs["cd"])


def _ret_bwd(q, k, v, do, states, tabs, ncx, ncc):
    T, C, H = q.shape[0], RET_CHUNK, RET_HEADS
    dk, dv = q.shape[1] // H, v.shape[1] // H
    ns = ncx + ncc
    kscale = dk ** -0.5

    G = RET_HEAD_GROUP

    def body(q_ref, k_ref, v_ref, do_ref, st_ref, cos_ref, sin_ref, intra_ref, qd_ref, kd_ref, cd_ref,
             dm_ref, wq_ref, wk_ref, dq_ref, dk_ref, dv_ref, dl_ref, ds_scr):
        first = pl.program_id(2) == 0

        @pl.when(first)
        def _():
            ds_scr[...] = jnp.zeros_like(ds_scr)

        cos, sin = cos_ref[...], sin_ref[...]
        dm, wq, wk = dm_ref[...], wq_ref[...], wk_ref[...]
        for g in range(G):
            ks_, vs_ = slice(g * dk, (g + 1) * dk), slice(g * dv, (g + 1) * dv)
            qv = _rot_half(q_ref[:, ks_], cos, sin, False)
            kv = _rot_half(k_ref[:, ks_], cos, sin, False) * kscale
            qb, kb = qv.astype(BF16), kv.astype(BF16)
            vb = v_ref[:, vs_].astype(BF16)
            dob = do_ref[:, vs_]
            intra, qd, kd, cd = intra_ref[g], qd_ref[g], kd_ref[g], cd_ref[g]
            S, dS = st_ref[g], ds_scr[g]
            Sb, dSb = S.astype(BF16), dS.astype(BF16)
            P = _dot_nt(qb, kb) * intra
            dP_raw = _dot_nt(dob, vb)
            dPb = (dP_raw * intra).astype(BF16)
            dq_cross = _dot_nt(dob, Sb) * qd
            dq_rot = _dot(dPb, kb) + dq_cross
            dk_state = _dot_nt(vb, dSb) * kd
            dk_rot = _dot_tn(dPb, qb) + dk_state
            dv_ref[:, vs_] = _dot_tn(P.astype(BF16), dob) + _dot((kv * kd).astype(BF16), dSb)
            dq_ref[:, ks_] = _rot_half(dq_rot, cos, sin, True)
            dk_ref[:, ks_] = _rot_half(dk_rot, cos, sin, True) * kscale
            dlam = (jnp.sum(dm * P * dP_raw) + jnp.sum(wq * qv * dq_cross)
                    + C * jnp.sum(cd * S * dS) + jnp.sum(wk * kv * dk_state))
            dl = jnp.full((1, LANES), dlam, F32)

            @pl.when(first)
            def _(g=g, dl=dl):
                dl_ref[g] = dl

            @pl.when(jnp.logical_not(first))
            def _(g=g, dl=dl):
                dl_ref[g] += dl

            ds_scr[g] = cd * dS + _dot_tn((qv * qd).astype(BF16), dob)

    chunk = lambda d, h, s: _ret_chunk(d, ns - 1 - s, ncx, ncc)
    tab = lambda shape: pl.BlockSpec((None, G) + shape, lambda d, h, s: (d, h, 0, 0))
    dtab = lambda shape: pl.BlockSpec((None,) + shape, lambda d, h, s: (d, 0, 0))
    return pl.pallas_call(
        body, grid=(2, H // G, ns),
        in_specs=[pl.BlockSpec((C, G * dk), lambda d, h, s: (chunk(d, h, s), h)),
                  pl.BlockSpec((C, G * dk), lambda d, h, s: (chunk(d, h, s), h)),
                  pl.BlockSpec((C, G * dv), lambda d, h, s: (chunk(d, h, s), h)),
                  pl.BlockSpec((None, C, G * dv), lambda d, h, s: (d, chunk(d, h, s), h)),
                  pl.BlockSpec((None, G, None, dk, dv), lambda d, h, s: (d, h, ns - 1 - s, 0, 0)),
                  pl.BlockSpec((C, dk), lambda d, h, s: (chunk(d, h, s), 0)),
                  pl.BlockSpec((C, dk), lambda d, h, s: (chunk(d, h, s), 0)),
                  tab((C, C)), tab((C, dk)), tab((C, dk)), tab((1, dv)),
                  dtab((C, C)), dtab((C, dk)), dtab((C, dk))],
        out_specs=[pl.BlockSpec((None, C, G * dk), lambda d, h, s: (d, chunk(d, h, s), h)),
                   pl.BlockSpec((None, C, G * dk), lambda d, h, s: (d, chunk(d, h, s), h)),
                   pl.BlockSpec((None, C, G * dv), lambda d, h, s: (d, chunk(d, h, s), h)),
                   pl.BlockSpec((None, G, 1, LANES), lambda d, h, s: (d, h, 0, 0))],
        out_shape=[jax.ShapeDtypeStruct((2, T, H * dk), F32), jax.ShapeDtypeStruct((2, T, H * dk), F32),
                   jax.ShapeDtypeStruct((2, T, H * dv), F32), jax.ShapeDtypeStruct((2, H, 1, LANES), F32)],
        scratch_shapes=[pltpu.VMEM((G, dk, dv), F32)],
        compiler_params=_params("parallel", "parallel", "arbitrary"), name="ret_bwd",
    )(q, k, v, do, states, tabs["cos"], tabs["sin"], tabs["intra"], tabs["qd"], tabs["kd"], tabs["cd"],
      tabs["dmat"], tabs["wq"], tabs["wk"])


def _ret_tables(decay_logit, Nx, Nc, dk, dv):
    C, H = RET_CHUNK, RET_HEADS
    inv = RET_ROPE_BASE ** (-jnp.linspace(0.0, 1.0, dk // 2, dtype=F32))
    ang = jnp.arange(Nx, dtype=F32)[:, None] * inv[None, :]
    cos, sin = jnp.cos(ang), jnp.sin(ang)
    cosf = jnp.concatenate([jnp.concatenate([cos, cos], 1), jnp.ones((Nc, dk), F32)], 0)
    sinf = jnp.concatenate([jnp.concatenate([-sin, sin], 1), jnp.zeros((Nc, dk), F32)], 0)
    lg = jax.nn.log_sigmoid(decay_logit.astype(F32))
    idx = jnp.arange(C, dtype=F32)
    diff = idx[:, None] - idx[None, :]
    dmat = jnp.stack([jnp.maximum(diff, 0.0), jnp.maximum(-diff, 0.0)])
    mask = jnp.stack([diff >= 0, diff <= 0])
    intra = jnp.where(mask[:, None], jnp.exp(lg[:, :, None, None] * dmat[:, None]), 0.0)
    wq = jnp.stack([idx + 1.0, C - idx])
    wk = jnp.stack([C - 1.0 - idx, idx])
    qd = jnp.exp(lg[:, :, None] * wq[:, None, :])
    kd = jnp.exp(lg[:, :, None] * wk[:, None, :])
    cd = jnp.exp(lg * C)
    bc = lambda t, w: jnp.broadcast_to(t[..., None], t.shape + (w,))
    return dict(cos=cosf, sin=sinf, intra=intra, qd=bc(qd, dk), kd=bc(kd, dk),
                cd=jnp.broadcast_to(cd[:, :, None, None], (2, H, 1, dv)),
                dmat=dmat, wq=bc(wq, dk), wk=bc(wk, dk), lg=lg)


def _attn_fwd(q, kn, v, kr, Nx, tq):
    T, H = q.shape[0], MLA_HEADS
    n_xq = Nx // tq

    def body(q_ref, kn_ref, v_ref, kr_ref, o_ref, lse_ref):
        def attend(lo):
            kcat = jnp.concatenate([kn_ref[lo:, :], kr_ref[lo:, :]], axis=1)
            s = _dot_nt(q_ref[...], kcat)
            m = jnp.max(s, axis=-1, keepdims=True)
            p = jnp.exp(s - m)
            l = jnp.sum(p, axis=-1, keepdims=True)
            o_ref[...] = (_dot(p.astype(BF16), v_ref[lo:, :]) / l).astype(o_ref.dtype)
            lse_ref[...] = m + jnp.log(l)

        @pl.when(pl.program_id(1) < n_xq)
        def _():
            attend(0)

        @pl.when(pl.program_id(1) >= n_xq)
        def _():
            attend(Nx)

    return pl.pallas_call(
        body, grid=(H, T // tq),
        in_specs=[pl.BlockSpec((tq, 2 * LANES), lambda h, i: (i, h)),
                  pl.BlockSpec((T, LANES), lambda h, i: (0, h)),
                  pl.BlockSpec((T, LANES), lambda h, i: (0, h)),
                  pl.BlockSpec((T, LANES), lambda h, i: (0, 0))],
        out_specs=[pl.BlockSpec((tq, LANES), lambda h, i: (i, h)),
                   pl.BlockSpec((None, tq, 1), lambda h, i: (h, i, 0))],
        out_shape=[jax.ShapeDtypeStruct((T, H * LANES), BF16), jax.ShapeDtypeStruct((H, T, 1), F32)],
        compiler_params=_params("parallel", "arbitrary"), name="attn_fwd")(q, kn, v, kr)


def _attn_bwd(q, kn, v, kr, do, lse, Nx, tq):
    T, H = q.shape[0], MLA_HEADS
    n_xq, nq = Nx // tq, T // tq

    def body(q_ref, kn_ref, v_ref, kr_ref, do_ref, lse_ref, dq_ref, dkn_ref, dv_ref, dkr_ref, dk_acc, dv_acc):
        h, i = pl.program_id(0), pl.program_id(1)

        @pl.when(i == 0)
        def _():
            dk_acc[...] = jnp.zeros_like(dk_acc)
            dv_acc[...] = jnp.zeros_like(dv_acc)

        def attend(lo):
            kcat = jnp.concatenate([kn_ref[lo:, :], kr_ref[lo:, :]], axis=1)
            qb, dob = q_ref[...], do_ref[...]
            p = jnp.exp(_dot_nt(qb, kcat) - lse_ref[...])
            dp = _dot_nt(dob, v_ref[lo:, :])
            delta = jnp.sum(p * dp, axis=-1, keepdims=True)
            dsb = (p * (dp - delta)).astype(BF16)
            dq_ref[...] = _dot(dsb, kcat)
            dk_acc[lo:, :] += _dot_tn(dsb, qb)
            dv_acc[lo:, :] += _dot_tn(p.astype(BF16), dob)

        @pl.when(i < n_xq)
        def _():
            attend(0)

        @pl.when(i >= n_xq)
        def _():
            attend(Nx)

        @pl.when(i == nq - 1)
        def _():
            dkn_ref[...] = dk_acc[:, :LANES].astype(dkn_ref.dtype)
            dv_ref[...] = dv_acc[...].astype(dv_ref.dtype)

        @pl.when(jnp.logical_and(i == nq - 1, h == 0))
        def _():
            dkr_ref[...] = dk_acc[:, LANES:]

        @pl.when(jnp.logical_and(i == nq - 1, h != 0))
        def _():
            dkr_ref[...] += dk_acc[:, LANES:]

    return pl.pallas_call(
        body, grid=(H, nq),
        in_specs=[pl.BlockSpec((tq, 2 * LANES), lambda h, i: (i, h)),
                  pl.BlockSpec((T, LANES), lambda h, i: (0, h)),
                  pl.BlockSpec((T, LANES), lambda h, i: (0, h)),
                  pl.BlockSpec((T, LANES), lambda h, i: (0, 0)),
                  pl.BlockSpec((tq, LANES), lambda h, i: (i, h)),
                  pl.BlockSpec((None, tq, 1), lambda h, i: (h, i, 0))],
        out_specs=[pl.BlockSpec((tq, 2 * LANES), lambda h, i: (i, h)),
                   pl.BlockSpec((T, LANES), lambda h, i: (0, h)),
                   pl.BlockSpec((T, LANES), lambda h, i: (0, h)),
                   pl.BlockSpec((T, LANES), lambda h, i: (0, 0))],
        out_shape=[jax.ShapeDtypeStruct((T, H * 2 * LANES), F32), jax.ShapeDtypeStruct((T, H * LANES), BF16),
                   jax.ShapeDtypeStruct((T, H * LANES), BF16), jax.ShapeDtypeStruct((T, LANES), F32)],
        scratch_shapes=[pltpu.VMEM((T, 2 * LANES), F32), pltpu.VMEM((T, LANES), F32)],
        compiler_params=_params("arbitrary", "arbitrary"), name="attn_bwd")(q, kn, v, kr, do, lse)


def _mla_tables(Nx, Nc):
    ad = MLA_D_ROPE // 2
    inv = AXIAL_ROPE_BASE ** (-jnp.arange(ad // 2, dtype=F32) * 2.0 / ad)
    t = jnp.arange(Nx)
    rang = (t // GRID_W).astype(F32)[:, None] * inv[None, :]
    cang = (t % GRID_W).astype(F32)[:, None] * inv[None, :]
    rc, rs, cc, cs = jnp.cos(rang), jnp.sin(rang), jnp.cos(cang), jnp.sin(cang)
    pad1, pad0 = jnp.ones((Nx, LANES - MLA_D_ROPE), F32), jnp.zeros((Nx, LANES - MLA_D_ROPE), F32)
    cos = jnp.concatenate([rc, rc, cc, cc, pad1], 1)
    sin = jnp.concatenate([-rs, rs, -cs, cs, pad0], 1)
    return (jnp.concatenate([cos, jnp.ones((Nc, LANES), F32)], 0),
            jnp.concatenate([sin, jnp.zeros((Nc, LANES), F32)], 0))


def _place():
    x, y, c = lax.axis_index("x"), lax.axis_index("y"), lax.axis_index("c")
    return x, y, c


def _all_gather8(v):
    R = v.shape[0]

    def body(v_ref, g_ref, s_ref, send_sems, recv_sems):
        x, y, c = _place()
        me = 4 * x + 2 * y + c
        g_ref[me] = v_ref[...]
        copies = []
        for k in range(1, N_DEV):
            peer = (x ^ (k >> 2), y ^ ((k >> 1) & 1), c ^ (k & 1))
            copies.append(pltpu.make_async_remote_copy(
                src_ref=v_ref, dst_ref=g_ref.at[me], send_sem=send_sems.at[k - 1], recv_sem=recv_sems.at[k - 1],
                device_id=peer, device_id_type=MESH))
        for cp in copies:
            cp.start()
        for cp in copies:
            cp.wait_recv()
        for cp in copies:
            cp.wait_send()
        acc = g_ref[0]
        for k in range(1, N_DEV):
            acc = acc + g_ref[k]
        s_ref[...] = acc

    vm = pl.BlockSpec(memory_space=pltpu.VMEM)
    return pl.pallas_call(
        body, in_specs=[vm], out_specs=[vm, vm],
        out_shape=[jax.ShapeDtypeStruct((N_DEV, R, LANES), F32), jax.ShapeDtypeStruct((R, LANES), F32)],
        scratch_shapes=[pltpu.SemaphoreType.DMA((N_DEV - 1,)), pltpu.SemaphoreType.DMA((N_DEV - 1,))],
        compiler_params=pltpu.CompilerParams(vmem_limit_bytes=V7X_VMEM_LIMIT_BYTES), name="all_gather8")(v)


def _other_chips(x, y):
    return [(1 - x, y), (x, 1 - y), (1 - x, 1 - y)]


def _place_own(w, chip_idx, after):
    R, C = w.shape
    tr = _pick(R, max(16, (2 << 20) // (C * 4)), 16)

    def body(s_ref, w_ref, after_ref, o_ref):
        o_ref[...] = w_ref[...].astype(o_ref.dtype)

    return pl.pallas_call(
        body,
        grid_spec=pltpu.PrefetchScalarGridSpec(
            num_scalar_prefetch=1, grid=(R // tr,),
            in_specs=[pl.BlockSpec((tr, C), lambda i, s: (i, 0)), pl.BlockSpec(memory_space=pl.ANY)],
            out_specs=pl.BlockSpec((None, tr, C), lambda i, s: (s[0], i, 0))),
        out_shape=jax.ShapeDtypeStruct((N_CHIPS, R, C), BF16),
        compiler_params=_params("parallel"), name="place_own")(chip_idx, w, after)


HBM_SPEC = pl.BlockSpec(memory_space=pltpu.HBM)
SEM_SPEC = pl.BlockSpec(memory_space=pltpu.SEMAPHORE)
DATAFLOW_EFFECT = pltpu.SideEffectType.DATAFLOW_SIDE_EFFECTING


def _gather_start(buf, name):
    def body(b_ref, send_sem, recv_sem, b_thru, token):
        x, y, c = _place()
        chip = 2 * x + y
        for ch in _other_chips(x, y):
            pltpu.make_async_remote_copy(src_ref=b_ref.at[chip], dst_ref=b_ref.at[chip], send_sem=send_sem,
                                         recv_sem=recv_sem, device_id=(*ch, c), device_id_type=MESH).start()
        token[...] = jnp.zeros_like(token)

    return pl.pallas_call(
        body, name=name,
        out_shape=(pltpu.SemaphoreType.DMA(()), pltpu.SemaphoreType.DMA(()), pltpu.HBM(buf.shape, buf.dtype),
                   jax.ShapeDtypeStruct((8, LANES), F32)),
        in_specs=(HBM_SPEC,), out_specs=(SEM_SPEC, SEM_SPEC, HBM_SPEC, pl.BlockSpec(memory_space=pltpu.VMEM)),
        input_output_aliases={0: 2},
        compiler_params=pltpu.CompilerParams(has_side_effects=DATAFLOW_EFFECT),
    )(pltpu.with_memory_space_constraint(buf, pltpu.HBM))


def _gather_wait(send_sem, recv_sem, buf_thru, after, name):
    def body(b_ref, send_sem, recv_sem, after_ref, b_out):
        x, y, c = _place()
        three = b_ref.at[pl.ds(0, 3)]
        cp = pltpu.make_async_remote_copy(src_ref=three, dst_ref=three, send_sem=send_sem, recv_sem=recv_sem,
                                          device_id=(x, y, 1 - c), device_id_type=MESH)
        cp.wait_send()
        cp.wait_recv()

    return pl.pallas_call(
        body, name=name, out_shape=pltpu.HBM(buf_thru.shape, buf_thru.dtype),
        in_specs=(HBM_SPEC, SEM_SPEC, SEM_SPEC, pl.BlockSpec(memory_space=pl.ANY)), out_specs=HBM_SPEC,
        input_output_aliases={0: 0},
        compiler_params=pltpu.CompilerParams(has_side_effects=DATAFLOW_EFFECT),
    )(buf_thru, send_sem, recv_sem, after)


def _scatter_start(hb, name):
    def body(h_ref, land_ref, send_sem, recv_sem, h_thru, land_thru, token):
        x, y, c = _place()
        chip = 2 * x + y
        for ch in _other_chips(x, y):
            pltpu.make_async_remote_copy(src_ref=h_ref.at[2 * ch[0] + ch[1]], dst_ref=land_ref.at[chip],
                                         send_sem=send_sem, recv_sem=recv_sem, device_id=(*ch, c),
                                         device_id_type=MESH).start()
        token[...] = jnp.zeros_like(token)

    return pl.pallas_call(
        body, name=name,
        out_shape=(pltpu.SemaphoreType.DMA(()), pltpu.SemaphoreType.DMA(()), pltpu.HBM(hb.shape, hb.dtype),
                   pltpu.HBM(hb.shape, hb.dtype), jax.ShapeDtypeStruct((8, LANES), F32)),
        in_specs=(HBM_SPEC, HBM_SPEC),
        out_specs=(SEM_SPEC, SEM_SPEC, HBM_SPEC, HBM_SPEC, pl.BlockSpec(memory_space=pltpu.VMEM)),
        input_output_aliases={0: 2, 1: 3},
        compiler_params=pltpu.CompilerParams(has_side_effects=DATAFLOW_EFFECT),
    )(pltpu.with_memory_space_constraint(hb, pltpu.HBM),
      pltpu.with_memory_space_constraint(lax.empty(hb.shape, hb.dtype), pltpu.HBM))


def _scatter_wait(send_sem, recv_sem, h_thru, land_thru, after, name):
    def body(h_ref, land_ref, send_sem, recv_sem, after_ref, h_dead, got_ref):
        x, y, c = _place()
        cp = pltpu.make_async_remote_copy(src_ref=h_ref.at[pl.ds(0, 3)], dst_ref=land_ref.at[pl.ds(0, 3)],
                                          send_sem=send_sem, recv_sem=recv_sem, device_id=(x, y, 1 - c),
                                          device_id_type=MESH)
        cp.wait_send()
        cp.wait_recv()

    return pl.pallas_call(
        body, name=name,
        out_shape=(pltpu.HBM(h_thru.shape, h_thru.dtype), pltpu.HBM(land_thru.shape, land_thru.dtype)),
        in_specs=(HBM_SPEC, HBM_SPEC, SEM_SPEC, SEM_SPEC, pl.BlockSpec(memory_space=pl.ANY)),
        out_specs=(HBM_SPEC, HBM_SPEC), input_output_aliases={0: 0, 1: 1},
        compiler_params=pltpu.CompilerParams(has_side_effects=DATAFLOW_EFFECT),
    )(h_thru, land_thru, send_sem, recv_sem, after)[1]


def _swap_start(g, name):
    J, R, C = g.shape
    half = R // 2

    def body(g_ref, land_ref, send_sem, recv_sem, g_thru, land_thru, token):
        x, y, c = _place()
        pltpu.make_async_remote_copy(src_ref=g_ref.at[:, pl.ds((1 - c) * half, half), :], dst_ref=land_ref,
                                     send_sem=send_sem, recv_sem=recv_sem, device_id=(x, y, 1 - c),
                                     device_id_type=MESH).start()
        token[...] = jnp.zeros_like(token)

    return pl.pallas_call(
        body, name=name,
        out_shape=(pltpu.SemaphoreType.DMA(()), pltpu.SemaphoreType.DMA(()), pltpu.HBM(g.shape, g.dtype),
                   pltpu.HBM((J, half, C), g.dtype), jax.ShapeDtypeStruct((8, LANES), F32)),
        in_specs=(HBM_SPEC, HBM_SPEC),
        out_specs=(SEM_SPEC, SEM_SPEC, HBM_SPEC, HBM_SPEC, pl.BlockSpec(memory_space=pltpu.VMEM)),
        input_output_aliases={0: 2, 1: 3},
        compiler_params=pltpu.CompilerParams(has_side_effects=DATAFLOW_EFFECT),
    )(pltpu.with_memory_space_constraint(g, pltpu.HBM),
      pltpu.with_memory_space_constraint(lax.empty((J, half, C), g.dtype), pltpu.HBM))


def _swap_wait(send_sem, recv_sem, g_thru, land_thru, after, name):
    half = land_thru.shape[1]

    def body(g_ref, land_ref, send_sem, recv_sem, after_ref, g_out, got_ref):
        x, y, c = _place()
        cp = pltpu.make_async_remote_copy(src_ref=g_ref.at[:, pl.ds((1 - c) * half, half), :], dst_ref=land_ref,
                                          send_sem=send_sem, recv_sem=recv_sem, device_id=(x, y, 1 - c),
                                          device_id_type=MESH)
        cp.wait_send()
        cp.wait_recv()

    return pl.pallas_call(
        body, name=name,
        out_shape=(pltpu.HBM(g_thru.shape, g_thru.dtype), pltpu.HBM(land_thru.shape, land_thru.dtype)),
        in_specs=(HBM_SPEC, HBM_SPEC, SEM_SPEC, SEM_SPEC, pl.BlockSpec(memory_space=pl.ANY)),
        out_specs=(HBM_SPEC, HBM_SPEC), input_output_aliases={0: 0, 1: 1},
        compiler_params=pltpu.CompilerParams(has_side_effects=DATAFLOW_EFFECT),
    )(g_thru, land_thru, send_sem, recv_sem, after)


def _add_half(g, r, place_idx):
    J, R, C = g.shape
    half = R // 2
    tr = _pick(half, max(16, (2 << 20) // (C * 4)), 16)
    nb = half // tr

    def body(s_ref, g_ref, r_ref, o_ref, ob_ref):
        acc = g_ref[...] + r_ref[...]
        o_ref[...] = acc
        ob_ref[...] = acc.astype(ob_ref.dtype)

    spec = pl.BlockSpec((None, tr, C), lambda j, i, s: (j, i, 0))
    return pl.pallas_call(
        body,
        grid_spec=pltpu.PrefetchScalarGridSpec(
            num_scalar_prefetch=1, grid=(J, nb),
            in_specs=[pl.BlockSpec((None, tr, C), lambda j, i, s: (j, s[0] * nb + i, 0)), spec],
            out_specs=[spec, spec]),
        out_shape=[jax.ShapeDtypeStruct((J, half, C), F32), jax.ShapeDtypeStruct((J, half, C), BF16)],
        compiler_params=_params("parallel", "parallel"), name="add_half")(place_idx[4], g, r)


def _add_chips(h, p, place_idx):
    J, R, C = h.shape
    tr = _pick(R, max(16, (2 << 20) // (C * 4)), 16)
    nb = R // tr

    def body(s0, s1, s2, s3, s4, h_ref, p0_ref, p1_ref, p2_ref, o_ref):
        o_ref[...] = ((h_ref[...] + p0_ref[...].astype(F32)) + p1_ref[...].astype(F32)) + p2_ref[...].astype(F32)

    slot = lambda k: pl.BlockSpec((None, tr, C), lambda i, *s: (s[k][0], i, 0))
    return pl.pallas_call(
        body,
        grid_spec=pltpu.PrefetchScalarGridSpec(
            num_scalar_prefetch=5, grid=(nb,),
            in_specs=[slot(0), slot(1), slot(2), slot(3)],
            out_specs=pl.BlockSpec((tr, C), lambda i, *s: (s[4][0] * nb + i, 0))),
        out_shape=jax.ShapeDtypeStruct((2 * R, C), F32),
        compiler_params=_params("parallel"), name="add_chips")(*place_idx, h, p, p, p)


def _join_halves(s):
    R, C = s.shape
    half = R // 2

    def body(in_ref, o_ref, send_sem, recv_sem):
        x, y, c = _place()
        mine = o_ref.at[pl.ds(c * half, half)]
        cp = pltpu.make_async_remote_copy(src_ref=mine, dst_ref=mine, send_sem=send_sem, recv_sem=recv_sem,
                                          device_id=(x, y, 1 - c), device_id_type=MESH)
        cp.start()
        cp.wait()

    hbm = pl.BlockSpec(memory_space=pl.ANY)
    return pl.pallas_call(
        body, in_specs=[hbm], out_specs=hbm, out_shape=jax.ShapeDtypeStruct((R, C), s.dtype),
        input_output_aliases={0: 0},
        scratch_shapes=[pltpu.SemaphoreType.DMA, pltpu.SemaphoreType.DMA], name="join_halves")(s)


def _reduce_scatter_start(swap_state, place_idx, after, name):
    g, r = _swap_wait(*swap_state, after, name + "_swap_wait")
    h, hb = _add_half(g, r, place_idx)
    send_sem, recv_sem, hb_thru, land_thru, token = _scatter_start(hb, name + "_start")
    return (h, send_sem, recv_sem, hb_thru, land_thru), token


def _reduce_scatter_finish(state, place_idx, after, name):
    h, send_sem, recv_sem, hb_thru, land_thru = state
    p = _scatter_wait(send_sem, recv_sem, hb_thru, land_thru, after, name + "_wait")
    return _join_halves(_add_chips(h, p, place_idx))


def _pack(parts):
    flat = jnp.concatenate([p.reshape(-1).astype(F32) for p in parts])
    n = flat.shape[0]
    rows = -(-n // (8 * LANES)) * 8
    return jnp.pad(flat, (0, rows * LANES - n)).reshape(rows, LANES)


def _unpack(buf, shapes):
    flat = buf.reshape(buf.shape[:-2] + (-1,))
    out, off = [], 0
    for s in shapes:
        n = math.prod(s)
        out.append(flat[..., off:off + n].reshape(buf.shape[:-2] + tuple(s)))
        off += n
    return out


def _mod_table(mod_x, mod_c):
    return jnp.stack([mod_x.reshape(6, 1, -1), mod_c.reshape(6, 1, -1)])


def kernel(x, c, ctx, c_ctx, ada_w, ada_b, ln_g, ln_b, ret_w_qkv, ret_w_g, ret_decay_logit, ret_w_o, mla_w_dq, mla_g_q, mla_w_uq, mla_w_dkv, mla_g_kv, mla_w_ukv, mla_w_o, ffn_w_in, ffn_w_out, loss_target, m_c_ctx, m_ada_w, m_ada_b, m_ln_g, m_ln_b, m_ret_w_qkv, m_ret_w_g, m_ret_decay_logit, m_ret_w_o, m_mla_w_dq, m_mla_g_q, m_mla_w_uq, m_mla_w_dkv, m_mla_g_kv, m_mla_w_ukv, m_mla_w_o, m_ffn_w_in, m_ffn_w_out, v_c_ctx, v_ada_w, v_ada_b, v_ln_g, v_ln_b, v_ret_w_qkv, v_ret_w_g, v_ret_decay_logit, v_ret_w_o, v_mla_w_dq, v_mla_g_q, v_mla_w_uq, v_mla_w_dkv, v_mla_g_kv, v_mla_w_ukv, v_mla_w_o, v_ffn_w_in, v_ffn_w_out):
    Nx, D = x.shape[1], x.shape[2]
    Nc = ctx.shape[1]
    T = Nx + Nc
    tr = 256 if (Nx % 256 == 0 and Nc % 256 == 0) else 128
    n_xt = Nx // tr
    C = RET_CHUNK
    ncx, ncc = Nx // C, Nc // C
    H = RET_HEADS
    dk, dv = D // H, 2 * D // H
    L = MLA_LORA
    HM = MLA_HEADS
    D6 = 6 * D
    n6 = D6 // N_CHIPS
    Dq = D // N_CHIPS
    xi, yi, ci = lax.axis_index("x"), lax.axis_index("y"), lax.axis_index("c")
    chip = 2 * xi + yi
    dev = 4 * xi + 2 * yi + ci
    as_index = lambda s: jnp.reshape(s, (1,)).astype(jnp.int32)
    place_idx = (as_index(chip), as_index(2 * (1 - xi) + yi), as_index(2 * xi + 1 - yi),
                 as_index(2 * (1 - xi) + 1 - yi), as_index(ci))
    chip_idx = place_idx[0]
    tile = dict(T=T, tr=tr, n_xt=n_xt)

    def mla_pack(j):
        return jnp.concatenate([mla_w_dq[j], mla_w_uq[j], jnp.pad(mla_w_dkv[j], ((0, 0), (0, 64))),
                                mla_w_ukv[j], mla_w_o[j]], axis=1)

    mla_cols = [L, 3 * L // 2, L + LANES, 2 * L, D]

    def mla_unpack_weights(buf):
        offs = [0]
        for wdt in mla_cols:
            offs.append(offs[-1] + wdt)
        dq_, uq_, dkv_, ukv_, wo_ = [buf[:, :, offs[k]:offs[k + 1]] for k in range(5)]
        w_dq = dq_.reshape(D, L)
        w_dkv = dkv_.reshape(D, L + LANES)
        w_lora = jnp.concatenate([w_dq, w_dkv], axis=1)
        w_uq = jnp.moveaxis(uq_, 0, 1).reshape(L, HM, MLA_D_NOPE + MLA_D_ROPE)
        wq_cat = jnp.pad(w_uq, ((0, 0), (0, 0), (0, 2 * LANES - MLA_D_NOPE - MLA_D_ROPE))).reshape(L, HM * 2 * LANES)
        w_ukv = jnp.moveaxis(ukv_, 0, 1).reshape(L, HM, MLA_D_NOPE + MLA_D_V)
        w_kn = w_ukv[:, :, :MLA_D_NOPE].reshape(L, HM * MLA_D_NOPE)
        w_v = w_ukv[:, :, MLA_D_NOPE:].reshape(L, HM * MLA_D_V)
        w_o = wo_.reshape(HM * MLA_D_V, D)
        return dict(lora=w_lora[None], q=wq_cat[None], kn=w_kn[None], v=w_v[None], o=w_o[None])

    def mla_pack_grads(g):
        d_lora = g["lora"][0]
        d_dq = d_lora[:, :L].reshape(N_CHIPS, Dq, L)
        d_dkv = d_lora[:, L:].reshape(N_CHIPS, Dq, L + LANES)
        d_uq = g["q"][0].reshape(L, HM, 2 * LANES)[:, :, :MLA_D_NOPE + MLA_D_ROPE]
        d_uq = jnp.moveaxis(d_uq.reshape(L, N_CHIPS, -1), 1, 0)
        d_ukv = jnp.concatenate([g["kn"][0].reshape(L, HM, MLA_D_NOPE), g["v"][0].reshape(L, HM, MLA_D_V)], axis=2)
        d_ukv = jnp.moveaxis(d_ukv.reshape(L, N_CHIPS, -1), 1, 0)
        d_o = g["o"][0].reshape(N_CHIPS, L, D)
        return jnp.concatenate([d_dq, d_uq, d_dkv, d_ukv, d_o], axis=2)

    assert Dq == L, "the packed MLA buffer assumes D_MODEL / 4 == 512 rows per shard"

    def layer_shards(i):
        j = i // 2
        own = dict(qkv=ret_w_qkv[j], g=ret_w_g[j], o=ret_w_o[j]) if i % 2 == 0 else dict(mla=mla_pack(j))
        own.update(w_in=ffn_w_in[i], w_out=ffn_w_out[i])
        return own

    def arrived(i, n, after, lw):
        full = _gather_wait(*in_flight[i][n], after=after, name=f"gather_wait_{i}_{n}")
        if n == "qkv":
            lw.update(q=(full, 0, 1), k=(full, 1, 1), v=(full, 2, 2))
        elif n == "g":
            lw.update(gf=(full, 0, 2), gb=(full, 2, 2))
        elif n == "o":
            lw.update(o=full.reshape(1, 2 * D, D))
        elif n == "mla":
            lw.update(mla_unpack_weights(full))
        elif n == "w_in":
            lw.update(a=(full, 0, 2), b=(full, 2, 2))
        else:
            lw.update(out=full.reshape(1, -1, D))

    W = [None] * DEPTH
    in_flight, start_tok = [None] * DEPTH, jnp.zeros((8, LANES), F32)
    for i in range(DEPTH):
        in_flight[i] = {}
        for n, w_l in layer_shards(i).items():
            send_sem, recv_sem, thru, tok = _gather_start(_place_own(w_l, chip_idx, start_tok), name=f"gather_start_{i}_{n}")
            in_flight[i][n] = (send_sem, recv_sem, thru)
            start_tok = start_tok + tok
    start_tok = start_tok[0, 0]

    shapes1 = [(D,), (DEPTH, 2, Dq), (DEPTH, 2, Dq), (2, L // N_CHIPS), (2, L // N_CHIPS)]
    g1, _ = _all_gather8(_pack([c[0] + start_tok, ln_g, ln_b, mla_g_q, mla_g_kv]))
    c_all, lng_s, lnb_s, gq_s, gkv_s = _unpack(g1, shapes1)
    by_chip = lambda t: jnp.moveaxis(t[0::2], 0, -2).reshape(t.shape[1:-1] + (-1,))
    ln_g_full, ln_b_full = by_chip(lng_s), by_chip(lnb_s)
    gq_full, gkv_full = by_chip(gq_s), by_chip(gkv_s)

    cond = jnp.concatenate([c_all, c_ctx[None]], 0)
    silu_cond = cond * jax.nn.sigmoid(cond)
    s16 = jnp.pad(silu_cond, ((0, 16 - (N_DEV + 1)), (0, 0))).astype(BF16)
    mods = []
    for i in range(DEPTH):
        bias = lax.dynamic_slice_in_dim(ada_b[i], chip * n6, n6)
        mods.append(_mm_nn(s16, ada_w[i][None], name="ada_fwd")[:N_DEV + 1] + bias[None])
    g2, _ = _all_gather8(_pack([jnp.stack(mods)]))
    (mod_all,) = _unpack(g2, [(DEPTH, N_DEV + 1, n6)])
    mod_all = jnp.moveaxis(mod_all[0::2], 0, -2).reshape(DEPTH, N_DEV + 1, D6)
    mod_tabs = [_mod_table(lax.dynamic_index_in_dim(mod_all[i], dev, 0, False), mod_all[i, N_DEV])
                for i in range(DEPTH)]
    lnps = [[jnp.stack([ln_g_full[i, s], ln_b_full[i, s]])[:, None, :] for s in range(2)] for i in range(DEPTH)]

    ret_tabs = [_ret_tables(ret_decay_logit[j], Nx, Nc, dk, dv) for j in range(2)]
    mla_cos, mla_sin = _mla_tables(Nx, Nc)
    gqkv = [jnp.stack([gq_full[j], gkv_full[j]])[:, None, :] for j in range(2)]
    tq_f, tq_b = tr, tr

    h = jnp.concatenate([x[0], ctx[0]], axis=0) + start_tok
    u = _modulate(h, mod_tabs[0], **tile)
    saved = []
    for i in range(DEPTH):
        W[i] = lw = {}
        j, mod, sv = i // 2, mod_tabs[i], {}
        sv.update(h=h, u=u)
        if i % 2 == 0:
            arrived(i, "qkv", u, lw)
            q = _mm_nn(u, lw["q"], name="ret_q")
            k = _mm_nn(u, lw["k"], name="ret_k")
            v = _mm_nn(u, lw["v"], name="ret_v")
            arrived(i, "g", v, lw)
            gf = _mm_nn(u, lw["gf"], name="ret_gf")
            gb = _mm_nn(u, lw["gb"], name="ret_gb")
            o, states = _ret_fwd(q, k, v, ret_tabs[j], ncx, ncc)
            yg = _gate_fwd(gf, gb, o, dv=dv, **tile)
            arrived(i, "o", yg, lw)
            y = _mm_nn(yg, lw["o"], name="ret_o")
            sv.update(q=q, k=k, v=v, gf=gf, gb=gb, o=o, states=states, yg=yg)
        else:
            arrived(i, "mla", u, lw)
            lora = _mm_nn(u, lw["lora"], name="mla_lora")
            cqn, ckvn, kr = _lora_fwd(lora, gqkv[j], mla_cos, mla_sin, **tile)
            qcat = _mm_nn(cqn, lw["q"], name="mla_q")
            qrot = _qrope(qcat, mla_cos, mla_sin, False, **tile)
            kn = _mm_nn(ckvn, lw["kn"], out_dtype=BF16, name="mla_kn")
            vv = _mm_nn(ckvn, lw["v"], out_dtype=BF16, name="mla_v")
            att, lse = _attn_fwd(qrot, kn, vv, kr, Nx, tq_f)
            y = _mm_nn(att, lw["o"], name="mla_o")
            sv.update(lora=lora, cqn=cqn, ckvn=ckvn, kr=kr, qrot=qrot, kn=kn, vv=vv, att=att, lse=lse)
        h1, u2 = _ln_fwd(h, y, mod, lnps[i][0], 2, mod, (3, 4), **tile)
        arrived(i, "w_in", u2, lw)
        a = _mm_nn(u2, lw["a"], name="ffn_a")
        b = _mm_nn(u2, lw["b"], name="ffn_b")
        act = _swiglu_fwd(a, b, **tile)
        arrived(i, "w_out", act, lw)
        f = _mm_nn(act, lw["out"], name="ffn_out")
        last = i == DEPTH - 1
        h2, u_next = _ln_fwd(h1, f, mod, lnps[i][1], 5, None if last else mod_tabs[i + 1], (0, 1), **tile)
        sv.update(y=y, h1=h1, u2=u2, a=a, b=b, act=act, f=f)
        saved.append(sv)
        h, u = h2, u_next

    dh, err_cols = _loss_grad(h, loss_target[0], **tile)
    loss = lax.psum(0.5 * jnp.sum(err_cols) / D, ("x", "y", "c"))

    pending, swapping, order_tok = {}, [], [jnp.zeros((), F32)]

    def rs_advance(after):
        name, l, swap_state = swapping.pop(0)
        state, tok = _reduce_scatter_start(swap_state, place_idx, after, f"rs_{name}_{l}")
        pending[(name, l)] = state
        order_tok[0] = order_tok[0] + tok[0, 0]

    def rs_start(name, l, g):
        *swap_state, tok = _swap_start(g.reshape(N_CHIPS, -1, g.shape[-1]), f"rs_{name}_{l}_swap_start")
        if swapping:
            rs_advance(tok)
        swapping.append((name, l, swap_state))
        order_tok[0] = order_tok[0] + tok[0, 0]

    d_mods, d_lng, d_lnb = [None] * DEPTH, [None] * DEPTH, [None] * DEPTH
    d_gq, d_gkv, d_lam = [None] * 2, [None] * 2, [None] * 2
    for i in reversed(range(DEPTH)):
        j, lw, sv = i // 2, W[i], saved[i]
        mod = mod_tabs[i] + order_tok[0]
        dh1_res, df, dg_f, dlg1, dlb1 = _ln_bwd(sv["h1"], sv["f"], dh, mod, lnps[i][1], 5, **tile)
        dact = _mm_nt(df, lw["out"], name="ffn_out_nt")
        rs_start("ffn_w_out", i, _mm_tn(sv["act"], df, 1, name="ffn_out_tn").reshape(N_CHIPS, -1, D))
        da, db = _swiglu_bwd(sv["a"], sv["b"], dact, **tile)
        du2 = _mm_nt(db, lw["b"], add=_mm_nt(da, lw["a"], name="ffn_a_nt"), name="ffn_b_nt")
        g_in = _mm_tn(sv["u2"], da, 2, name="ffn_a_tn", slots=N_CHIPS)
        rs_start("ffn_w_in", i, _mm_tn(sv["u2"], db, 2, name="ffn_b_tn", slots=N_CHIPS, slot0=2, into=g_in))
        mod = mod_tabs[i] + order_tok[0]
        dh1, dsc_f, dsh_f = _mod_bwd(dh1_res, du2, sv["h1"], mod, 4, **tile)
        dh_res, dy, dg_a, dlg0, dlb0 = _ln_bwd(sv["h"], sv["y"], dh1, mod, lnps[i][0], 2, **tile)
        uu = sv["u"]
        if i % 2 == 0:
            dyg = _mm_nt(dy, lw["o"], name="ret_o_nt")
            rs_start("ret_w_o", j, _mm_tn(sv["yg"], dy, 1, name="ret_o_tn").reshape(N_CHIPS, -1, D))
            dgf, dgb, do_f, do_b = _gate_bwd(sv["gf"], sv["gb"], sv["o"], dyg, dv=dv, **tile)
            dq2, dk2, dv2, dlam = _ret_bwd(sv["q"], sv["k"], sv["v"], jnp.stack([do_f, do_b]), sv["states"],
                                           ret_tabs[j], ncx, ncc)
            dq, dkk, dvv = _add_dirs(dq2, dk2, dv2, **tile)
            du = _mm_nt(dq, lw["q"], name="ret_q_nt")
            du = _mm_nt(dkk, lw["k"], add=du, name="ret_k_nt")
            du = _mm_nt(dvv, lw["v"], add=du, name="ret_v_nt")
            du = _mm_nt(dgf, lw["gf"], add=du, name="ret_gf_nt")
            du = _mm_nt(dgb, lw["gb"], add=du, name="ret_gb_nt")
            g_qkv = _mm_tn(uu, dq, 1, name="ret_q_tn", slots=N_CHIPS)
            g_qkv = _mm_tn(uu, dkk, 1, name="ret_k_tn", slots=N_CHIPS, slot0=1, into=g_qkv)
            rs_start("ret_w_qkv", j, _mm_tn(uu, dvv, 2, name="ret_v_tn", slots=N_CHIPS, slot0=2, into=g_qkv))
            g_g = _mm_tn(uu, dgf, 2, name="ret_gf_tn", slots=N_CHIPS)
            rs_start("ret_w_g", j, _mm_tn(uu, dgb, 2, name="ret_gb_tn", slots=N_CHIPS, slot0=2, into=g_g))
            d_lam[j] = dlam[:, :, 0, 0]
        else:
            datt = _mm_nt(dy, lw["o"], name="mla_o_nt").astype(BF16)
            gm = dict(o=_mm_tn(sv["att"], dy, 1, name="mla_o_tn"))
            dqcat, dkn, dvv, dkr = _attn_bwd(sv["qrot"], sv["kn"], sv["vv"], sv["kr"], datt, sv["lse"], Nx, tq_b)
            dqraw = _qrope(dqcat, mla_cos, mla_sin, True, **tile)
            dcqn = _mm_nt(dqraw, lw["q"], name="mla_q_nt")
            gm["q"] = _mm_tn(sv["cqn"], dqraw, 1, name="mla_q_tn")
            dckvn = _mm_nt(dvv, lw["v"], add=_mm_nt(dkn, lw["kn"], name="mla_kn_nt"), name="mla_v_nt")
            gm["kn"] = _mm_tn(sv["ckvn"], dkn, 1, name="mla_kn_tn")
            gm["v"] = _mm_tn(sv["ckvn"], dvv, 1, name="mla_v_tn")
            dlora, dgq, dgkv = _lora_bwd(sv["lora"], dcqn, dckvn, dkr, gqkv[j], mla_cos, mla_sin, **tile)
            du = _mm_nt(dlora, lw["lora"], name="mla_lora_nt")
            gm["lora"] = _mm_tn(uu, dlora, 1, name="mla_lora_tn")
            rs_start("mla", j, mla_pack_grads(gm))
            d_gq[j], d_gkv[j] = dgq[0], dgkv[0]
        mod = mod_tabs[i] + order_tok[0]
        dh, dsc_a, dsh_a = _mod_bwd(dh_res, du, sv["h"], mod, 1, **tile)
        d_mods[i] = jnp.concatenate([dsh_a, dsc_a, dg_a, dsh_f, dsc_f, dg_f], axis=2)[:, 0, :]
        d_lng[i] = jnp.concatenate([dlg0, dlg1], 0)
        d_lnb[i] = jnp.concatenate([dlb0, dlb1], 0)

    rs_advance(dh)
    grad_x = dh[:Nx][None]

    d_mods = jnp.stack(d_mods)
    dlogit = jnp.stack([d_lam[j] * jax.nn.sigmoid(-ret_decay_logit[j]) for j in range(2)])
    shapes3 = [(DEPTH, D6), (DEPTH, D6), (DEPTH, 2, D), (DEPTH, 2, D), (2, L), (2, L), (2, 2, H)]
    g3, s3 = _all_gather8(_pack([d_mods[:, 0], d_mods[:, 1], jnp.stack(d_lng), jnp.stack(d_lnb),
                                 jnp.stack(d_gq), jnp.stack(d_gkv), dlogit]))
    dmod_x_all = _unpack(g3, shapes3)[0]
    dmod_x_sum, dmod_c_sum, g_lng, g_lnb, g_gq, g_gkv, g_decay = _unpack(s3, shapes3)
    grad_ada_b = dmod_x_sum + dmod_c_sum
    dmod9 = jnp.concatenate([jnp.moveaxis(dmod_x_all, 0, 1), dmod_c_sum[:, None]], axis=1)
    dmod16 = jnp.pad(lax.dynamic_slice_in_dim(dmod9, chip * n6, n6, axis=2), ((0, 0), (0, 16 - (N_DEV + 1)), (0, 0)))
    dmod16 = dmod16.astype(BF16)
    grad_ada_w = jnp.stack([_mm_tn(s16, dmod16[i], 1, name="ada_tn")[0] for i in range(DEPTH)])
    dsilu = _mm_nt(jnp.moveaxis(dmod16, 0, 1).reshape(16, DEPTH * n6), ada_w, name="ada_nt")
    _, s4 = _all_gather8(_pack([dsilu[N_DEV]]))
    sg = jax.nn.sigmoid(c_ctx)
    grad_c_ctx = (0.5 * s4.reshape(-1)[:D]) * (sg * (1.0 + c_ctx * (1.0 - sg)))

    my_cols = lambda t, n: lax.dynamic_slice_in_dim(t, chip * n, n, axis=t.ndim - 1)
    grad_ln_g, grad_ln_b = my_cols(g_lng, Dq), my_cols(g_lnb, Dq)
    grad_gq, grad_gkv = my_cols(g_gq, L // N_CHIPS), my_cols(g_gkv, L // N_CHIPS)

    upd_ada_w = _adamw(ada_w, grad_ada_w, m_ada_w, v_ada_w)
    behind = grad_c_ctx + upd_ada_w[0][0, 0, :D]

    def rs_done(name, l):
        return _reduce_scatter_finish(pending[(name, l)], place_idx, behind, f"rs_{name}_{l}")

    def rs(name, n_layers, shard_shape):
        return jnp.stack([rs_done(name, l).reshape(shard_shape) for l in range(n_layers)])

    grad_ret_w_qkv = rs("ret_w_qkv", 2, ret_w_qkv.shape[1:])
    grad_ret_w_g = rs("ret_w_g", 2, ret_w_g.shape[1:])
    grad_ret_w_o = rs("ret_w_o", 2, ret_w_o.shape[1:])
    grad_ffn_w_in = rs("ffn_w_in", DEPTH, ffn_w_in.shape[1:])
    grad_ffn_w_out = rs("ffn_w_out", DEPTH, ffn_w_out.shape[1:])
    mla_red = [rs_done("mla", l) for l in range(2)]
    offs = [0]
    for wdt in mla_cols:
        offs.append(offs[-1] + wdt)
    mla_parts = [jnp.stack([mla_red[l][:, offs[k]:offs[k + 1]] for l in range(2)]) for k in range(5)]
    grad_mla_w_dq, grad_mla_w_uq, grad_mla_w_ukv, grad_mla_w_o = mla_parts[0], mla_parts[1], mla_parts[3], mla_parts[4]
    grad_mla_w_dkv = mla_parts[2][:, :, :L + MLA_D_ROPE]

    grads = [grad_c_ctx, grad_ada_w, grad_ada_b, grad_ln_g, grad_ln_b, grad_ret_w_qkv, grad_ret_w_g, g_decay,
             grad_ret_w_o, grad_mla_w_dq, grad_gq, grad_mla_w_uq, grad_mla_w_dkv, grad_gkv, grad_mla_w_ukv,
             grad_mla_w_o, grad_ffn_w_in, grad_ffn_w_out]
    weights = [c_ctx, ada_w, ada_b, ln_g, ln_b, ret_w_qkv, ret_w_g, ret_decay_logit, ret_w_o, mla_w_dq, mla_g_q,
               mla_w_uq, mla_w_dkv, mla_g_kv, mla_w_ukv, mla_w_o, ffn_w_in, ffn_w_out]
    ms = [m_c_ctx, m_ada_w, m_ada_b, m_ln_g, m_ln_b, m_ret_w_qkv, m_ret_w_g, m_ret_decay_logit, m_ret_w_o,
          m_mla_w_dq, m_mla_g_q, m_mla_w_uq, m_mla_w_dkv, m_mla_g_kv, m_mla_w_ukv, m_mla_w_o, m_ffn_w_in, m_ffn_w_out]
    vs = [v_c_ctx, v_ada_w, v_ada_b, v_ln_g, v_ln_b, v_ret_w_qkv, v_ret_w_g, v_ret_decay_logit, v_ret_w_o,
          v_mla_w_dq, v_mla_g_q, v_mla_w_uq, v_mla_w_dkv, v_mla_g_kv, v_mla_w_ukv, v_mla_w_o, v_ffn_w_in, v_ffn_w_out]
    upd = [upd_ada_w if w_ is ada_w else _adamw(w_, g_, m_, v_) for w_, g_, m_, v_ in zip(weights, grads, ms, vs)]
    return (loss, grad_x, *grads, *[u_[0] for u_ in upd], *[u_[1] for u_ in upd], *[u_[2] for u_ in upd])
```

```python
import functools
import math

import jax
import jax.numpy as jnp
from jax import lax
from jax.experimental import pallas as pl
from jax.experimental.pallas import tpu as pltpu

F32, BF16 = jnp.float32, jnp.bfloat16
MESH = pl.DeviceIdType.MESH
V7X_VMEM_LIMIT_BYTES = 56 * 1024 * 1024
LANES = 128

GRID_W = 64
RET_HEADS = 8
RET_CHUNK = 128
RET_HEAD_GROUP = 8
RET_ROPE_BASE = 10000.0
GN_EPS = 1e-6
MLA_HEADS = 16
MLA_LORA = 512
MLA_D_NOPE = 128
MLA_D_ROPE = 64
MLA_D_V = 128
MLA_SCALE = (MLA_D_NOPE + MLA_D_ROPE) ** -0.5
AXIAL_ROPE_BASE = 10000.0
RMS_EPS = 1e-6
DEPTH = 4
DEEPNORM_ALPHA = (2 * DEPTH) ** 0.25
LN_EPS = 1e-5
ADAM_LR, ADAM_B1, ADAM_B2, ADAM_EPS, ADAM_WD, ADAM_STEP = 0.001, 0.9, 0.999, 1e-08, 0.01, 10
N_CHIPS = 4
N_DEV = 8


def _pick(dim, target, mult):
    best = None
    for d in range(mult, min(dim, target) + 1, mult):
        if dim % d == 0:
            best = d
    return dim if best is None else best


def _params(*sem):
    return pltpu.CompilerParams(dimension_semantics=sem, vmem_limit_bytes=V7X_VMEM_LIMIT_BYTES)


def _sigmoid(x):
    return 1.0 / (1.0 + jnp.exp(-x))


def _accumulate(step, nsteps, acc, part, write):
    if nsteps == 1:
        write(part)
        return

    @pl.when(step == 0)
    def _():
        acc[...] = part

    @pl.when(jnp.logical_and(step > 0, step < nsteps - 1))
    def _():
        acc[...] += part

    @pl.when(step == nsteps - 1)
    def _():
        write(acc[...] + part)


def _slots(w):
    return w if isinstance(w, tuple) else (w, 0, w.shape[0])


def _mm_nn(a, w, out_dtype=F32, name="mm_nn"):
    M, K = a.shape
    w, slot0, J = _slots(w)
    _, K2, n = w.shape
    assert K == K2
    tn = _pick(n, 1408, LANES)
    tk = _pick(K, 2048, LANES)
    tm = _pick(M, max(16, (6 << 20) // (tn * 4)), 16)
    npj, nk = n // tn, K // tk

    def body(a_ref, w_ref, o_ref, acc):
        part = jnp.dot(a_ref[...].astype(BF16), w_ref[...].astype(BF16), preferred_element_type=F32)

        def write(total):
            o_ref[...] = total.astype(o_ref.dtype)

        _accumulate(pl.program_id(2), nk, acc, part, write)

    return pl.pallas_call(
        body, grid=(M // tm, J * npj, nk),
        in_specs=[pl.BlockSpec((tm, tk), lambda i, j, k: (i, k)),
                  pl.BlockSpec((None, tk, tn), lambda i, j, k: (slot0 + j // npj, k, j % npj))],
        out_specs=pl.BlockSpec((tm, tn), lambda i, j, k: (i, j)),
        out_shape=jax.ShapeDtypeStruct((M, J * n), out_dtype),
        scratch_shapes=[pltpu.VMEM((tm, tn), F32)],
        compiler_params=_params("parallel", "parallel", "arbitrary"), name=name)(a, w)


def _mm_nt(dy, w, add=None, name="mm_nt"):
    M, N = dy.shape
    w, slot0, J = _slots(w)
    _, K, n = w.shape
    assert N == J * n
    tko = _pick(K, 1408, LANES)
    tc = _pick(n, 2048, LANES)
    tm = _pick(M, max(16, (6200 << 10) // (tko * 4)), 16)
    npj = n // tc
    nc = J * npj
    has_add = add is not None

    def body(*refs):
        if has_add:
            dy_ref, w_ref, add_ref, o_ref, acc = refs
        else:
            dy_ref, w_ref, o_ref, acc = refs
        part = lax.dot_general(dy_ref[...].astype(BF16), w_ref[...].astype(BF16),
                               (((1,), (1,)), ((), ())), preferred_element_type=F32)

        def write(total):
            o_ref[...] = total + add_ref[...] if has_add else total

        _accumulate(pl.program_id(2), nc, acc, part, write)

    in_specs = [pl.BlockSpec((tm, tc), lambda i, ko, c: (i, c)),
                pl.BlockSpec((None, tko, tc), lambda i, ko, c: (slot0 + c // npj, ko, c % npj))]
    args = [dy, w]
    if has_add:
        in_specs.append(pl.BlockSpec((tm, tko), lambda i, ko, c: (i, ko)))
        args.append(add)
    return pl.pallas_call(
        body, grid=(M // tm, K // tko, nc), in_specs=in_specs,
        out_specs=pl.BlockSpec((tm, tko), lambda i, ko, c: (i, ko)),
        out_shape=jax.ShapeDtypeStruct((M, K), F32),
        scratch_shapes=[pltpu.VMEM((tm, tko), F32)],
        compiler_params=_params("parallel", "parallel", "arbitrary"), name=name)(*args)


def _mm_tn(a, dy, J, name="mm_tn", slots=None, slot0=0, into=None):
    M, K = a.shape
    M2, N = dy.shape
    assert M == M2 and N % J == 0
    n = N // J
    tko = _pick(K, 1024, LANES)
    tn = _pick(n, 1408, LANES)
    tmc = _pick(M, 2176, 16)
    npj, nm = n // tn, M // tmc

    def body(*refs):
        a_ref, dy_ref, o_ref, acc = refs[0], refs[1], refs[-2], refs[-1]
        part = lax.dot_general(a_ref[...].astype(BF16), dy_ref[...].astype(BF16),
                               (((0,), (0,)), ((), ())), preferred_element_type=F32)

        def write(total):
            o_ref[...] = total

        _accumulate(pl.program_id(2), nm, acc, part, write)

    in_specs = [pl.BlockSpec((tmc, tko), lambda ko, j, m: (m, ko)),
                pl.BlockSpec((tmc, tn), lambda ko, j, m: (m, j))]
    args, aliases = [a, dy], {}
    if into is not None:
        in_specs.append(pl.BlockSpec(memory_space=pl.ANY))
        args.append(into)
        aliases = {2: 0}
    return pl.pallas_call(
        body, grid=(K // tko, J * npj, nm), in_specs=in_specs,
        out_specs=pl.BlockSpec((None, tko, tn), lambda ko, j, m: (slot0 + j // npj, ko, j % npj)),
        out_shape=jax.ShapeDtypeStruct((slots or J, K, n), F32), input_output_aliases=aliases,
        scratch_shapes=[pltpu.VMEM((tko, tn), F32)],
        compiler_params=_params("parallel", "parallel", "arbitrary"), name=name)(*args)


def _rowwise(name, fn, T, tr, n_xt, row_in, grp_in=(), const_in=(), row_out=(), gsum_w=(), tsum_w=(), ncol=1):
    nt = T // tr
    n_in = len(row_in) + len(grp_in) + len(const_in)
    n_ro, n_gs = len(row_out), len(gsum_w)
    assert ncol == 1 or not (gsum_w or tsum_w)

    def body(*refs):
        t = pl.program_id(0)
        vals = [r[...] for r in refs[:n_in]]
        rv = vals[:len(row_in)]
        gv = vals[len(row_in):len(row_in) + len(grp_in)]
        cv = vals[len(row_in) + len(grp_in):]
        ro, gs, ts = fn(rv, gv, cv)
        outs = refs[n_in:]
        for ref, val in zip(outs[:n_ro], ro):
            ref[...] = val.astype(ref.dtype)
        first_g = jnp.logical_or(t == 0, t == n_xt)
        for ref, val, first in ([(r, v, first_g) for r, v in zip(outs[n_ro:n_ro + n_gs], gs)]
                                + [(r, v, t == 0) for r, v in zip(outs[n_ro + n_gs:], ts)]):
            s = jnp.sum(val, axis=0, keepdims=True)

            @pl.when(first)
            def _(ref=ref, s=s):
                ref[...] = s

            @pl.when(jnp.logical_not(first))
            def _(ref=ref, s=s):
                ref[...] += s

    in_specs, args = [], []
    for spec in row_in:
        arr, width = spec[:2]
        lead = spec[2] if len(spec) > 2 else None
        step = spec[3] if len(spec) > 3 else 1
        if lead is not None:
            in_specs.append(pl.BlockSpec((None, tr, width), lambda t, cb, lead=lead, step=step: (lead, t, cb * step)))
        else:
            in_specs.append(pl.BlockSpec((tr, width), lambda t, cb, step=step: (t, cb * step)))
        args.append(arr)
    for arr in grp_in:
        in_specs.append(pl.BlockSpec((None,) + arr.shape[1:], lambda t, cb: (jnp.where(t >= n_xt, 1, 0), 0, 0, 0)))
        args.append(arr)
    for arr in const_in:
        in_specs.append(pl.BlockSpec(arr.shape, lambda t, cb: (0, 0, 0)))
        args.append(arr)
    out_specs, out_shape = [], []
    for wtot, wblk, dt in row_out:
        out_specs.append(pl.BlockSpec((tr, wblk), lambda t, cb: (t, cb)))
        out_shape.append(jax.ShapeDtypeStruct((T, wtot), dt))
    for w in gsum_w:
        out_specs.append(pl.BlockSpec((None, 1, w), lambda t, cb: (jnp.where(t >= n_xt, 1, 0), 0, 0)))
        out_shape.append(jax.ShapeDtypeStruct((2, 1, w), F32))
    for w in tsum_w:
        out_specs.append(pl.BlockSpec((1, w), lambda t, cb: (0, 0)))
        out_shape.append(jax.ShapeDtypeStruct((1, w), F32))
    sem = ("arbitrary", "arbitrary") if (gsum_w or tsum_w) else ("parallel", "parallel")
    return pl.pallas_call(body, grid=(nt, ncol), in_specs=in_specs, out_specs=out_specs, out_shape=out_shape,
                          compiler_params=_params(*sem), name=name)(*args)


def _modulate(h, mod, T, tr, n_xt):
    D = h.shape[1]

    def fn(rv, gv, cv):
        m = gv[0]
        return [rv[0] * (1.0 + m[1]) + m[0]], [], []

    return _rowwise("modulate", fn, T, tr, n_xt, [(h, D)], [mod], row_out=[(D, D, BF16)])[0]


def _ln_stats(r):
    mu = jnp.mean(r, axis=-1, keepdims=True)
    xc = r - mu
    var = jnp.mean(xc * xc, axis=-1, keepdims=True)
    rstd = lax.rsqrt(var + LN_EPS)
    return xc * rstd, rstd


def _ln_fwd(h, y, mod, lnp, gate_row, mod_next, next_rows, T, tr, n_xt):
    D = h.shape[1]
    with_u = mod_next is not None

    def fn(rv, gv, cv):
        r = DEEPNORM_ALPHA * rv[0] + gv[0][gate_row] * rv[1]
        xhat, _ = _ln_stats(r)
        out = xhat * cv[0][0] + cv[0][1]
        if not with_u:
            return [out], [], []
        mn = gv[1]
        return [out, out * (1.0 + mn[next_rows[1]]) + mn[next_rows[0]]], [], []

    grp = [mod, mod_next] if with_u else [mod]
    outs = [(D, D, F32), (D, D, BF16)] if with_u else [(D, D, F32)]
    res = _rowwise("ln_fwd", fn, T, tr, n_xt, [(h, D), (y, D)], grp, [lnp], row_out=outs)
    return (res[0], res[1]) if with_u else (res[0], None)


def _ln_bwd(h, y, dout, mod, lnp, gate_row, T, tr, n_xt):
    D = h.shape[1]

    def fn(rv, gv, cv):
        g = gv[0][gate_row]
        r = DEEPNORM_ALPHA * rv[0] + g * rv[1]
        xhat, rstd = _ln_stats(r)
        dxh = rv[2] * cv[0][0]
        m1 = jnp.mean(dxh, axis=-1, keepdims=True)
        m2 = jnp.mean(dxh * xhat, axis=-1, keepdims=True)
        dr = rstd * (dxh - m1 - xhat * m2)
        return [DEEPNORM_ALPHA * dr, g * dr], [dr * rv[1]], [rv[2] * xhat, rv[2]]

    return _rowwise("ln_bwd", fn, T, tr, n_xt, [(h, D), (y, D), (dout, D)], [mod], [lnp],
                    row_out=[(D, D, F32), (D, D, BF16)], gsum_w=[D], tsum_w=[D, D])


def _mod_bwd(dh_res, du, h, mod, scale_row, T, tr, n_xt):
    D = h.shape[1]

    def fn(rv, gv, cv):
        return [rv[0] + rv[1] * (1.0 + gv[0][scale_row])], [rv[1] * rv[2], rv[1]], []

    return _rowwise("mod_bwd", fn, T, tr, n_xt, [(dh_res, D), (du, D), (h, D)], [mod],
                    row_out=[(D, D, F32)], gsum_w=[D, D])


def _swiglu_fwd(a, b, T, tr, n_xt):
    F = a.shape[1]
    wc = _pick(F, 1408, LANES)

    def fn(rv, gv, cv):
        return [rv[0] * _sigmoid(rv[0]) * rv[1]], [], []

    return _rowwise("swiglu_fwd", fn, T, tr, n_xt, [(a, wc), (b, wc)], row_out=[(F, wc, BF16)], ncol=F // wc)[0]


def _swiglu_bwd(a, b, dact, T, tr, n_xt):
    F = a.shape[1]
    wc = _pick(F, 1408, LANES)

    def fn(rv, gv, cv):
        av, bv, dv = rv
        sg = _sigmoid(av)
        return [dv * bv * (sg * (1.0 + av * (1.0 - sg))), dv * av * sg], [], []

    return _rowwise("swiglu_bwd", fn, T, tr, n_xt, [(a, wc), (b, wc), (dact, wc)],
                    row_out=[(F, wc, BF16), (F, wc, BF16)], ncol=F // wc)


def _gn(o):
    mu = jnp.mean(o, axis=-1, keepdims=True)
    xc = o - mu
    var = jnp.mean(xc * xc, axis=-1, keepdims=True)
    rstd = lax.rsqrt(var + GN_EPS)
    return xc * rstd, rstd


def _gate_fwd(gf, gb, o, T, tr, n_xt, dv):
    W = gf.shape[1]

    def fn(rv, gv, cv):
        xf, _ = _gn(rv[2])
        xb, _ = _gn(rv[3])
        return [rv[0] * _sigmoid(rv[0]) * xf + rv[1] * _sigmoid(rv[1]) * xb], [], []

    return _rowwise("gate_fwd", fn, T, tr, n_xt, [(gf, dv), (gb, dv), (o, dv, 0), (o, dv, 1)],
                    row_out=[(W, dv, BF16)], ncol=W // dv)[0]


def _gate_bwd(gf, gb, o, dy, T, tr, n_xt, dv):
    W = gf.shape[1]

    def fn(rv, gv, cv):
        outs_g, outs_o = [], []
        for g, ov in ((rv[0], rv[2]), (rv[1], rv[3])):
            xh, rstd = _gn(ov)
            sg = _sigmoid(g)
            outs_g.append(rv[4] * xh * (sg * (1.0 + g * (1.0 - sg))))
            dxh = rv[4] * g * sg
            m1 = jnp.mean(dxh, axis=-1, keepdims=True)
            m2 = jnp.mean(dxh * xh, axis=-1, keepdims=True)
            outs_o.append(rstd * (dxh - m1 - xh * m2))
        return outs_g + outs_o, [], []

    return _rowwise("gate_bwd", fn, T, tr, n_xt,
                    [(gf, dv), (gb, dv), (o, dv, 0), (o, dv, 1), (dy, dv)],
                    row_out=[(W, dv, BF16)] * 4, ncol=W // dv)


def _add_dirs(dq, dk, dv_, T, tr, n_xt):
    ws = [dq.shape[2], dk.shape[2], dv_.shape[2]]

    def fn(rv, gv, cv):
        return [rv[0] + rv[1], rv[2] + rv[3], rv[4] + rv[5]], [], []

    row_in = []
    for arr, w in zip((dq, dk, dv_), ws):
        row_in += [(arr, w, 0), (arr, w, 1)]
    return _rowwise("add_dirs", fn, T, tr, n_xt, row_in, row_out=[(w, w, BF16) for w in ws])


def _swap16(x):
    lane = lax.broadcasted_iota(jnp.int32, x.shape, x.ndim - 1)
    return jnp.where(lane % 32 < 16, pltpu.roll(x, LANES - 16, x.ndim - 1), pltpu.roll(x, 16, x.ndim - 1))


def _rope2d(x, cos, sin, transpose):
    if transpose:
        return x * cos + _swap16(x * sin)
    return x * cos + _swap16(x) * sin


def _rms(x, g):
    rstd = lax.rsqrt(jnp.mean(x * x, axis=-1, keepdims=True) + RMS_EPS)
    return x * rstd, rstd


def _lora_fwd(lora, gq_gkv, cos, sin, T, tr, n_xt):
    L = MLA_LORA

    def fn(rv, gv, cv):
        x = rv[0]
        xq, _ = _rms(x[:, :L], None)
        xkv, _ = _rms(x[:, L:2 * L], None)
        return [xq * cv[0][0], xkv * cv[0][1], _rope2d(x[:, 2 * L:], rv[1], rv[2], False)], [], []

    return _rowwise("lora_fwd", fn, T, tr, n_xt, [(lora, 2 * L + LANES), (cos, LANES), (sin, LANES)],
                    const_in=[gq_gkv], row_out=[(L, L, BF16), (L, L, BF16), (LANES, LANES, BF16)])


def _lora_bwd(lora, dq, dkv, dkr, gq_gkv, cos, sin, T, tr, n_xt):
    L = MLA_LORA

    def fn(rv, gv, cv):
        x = rv[0]
        outs, sums = [], []
        for xs, dy, g in ((x[:, :L], rv[1], cv[0][0]), (x[:, L:2 * L], rv[2], cv[0][1])):
            xh, rstd = _rms(xs, None)
            dxh = dy * g
            outs.append(rstd * (dxh - xh * jnp.mean(dxh * xh, axis=-1, keepdims=True)))
            sums.append(dy * xh)
        outs.append(_rope2d(rv[3], rv[4], rv[5], True))
        return [jnp.concatenate(outs, axis=1)], [], sums

    W = 2 * L + LANES
    return _rowwise("lora_bwd", fn, T, tr, n_xt,
                    [(lora, W), (dq, L), (dkv, L), (dkr, LANES), (cos, LANES), (sin, LANES)],
                    const_in=[gq_gkv], row_out=[(W, W, BF16)], tsum_w=[L, L])


def _qrope(q, cos, sin, transpose, T, tr, n_xt):
    W = q.shape[1]

    def fn(rv, gv, cv):
        x = rv[0] * MLA_SCALE
        parts = []
        for hd in range(W // (2 * LANES)):
            lo = hd * 2 * LANES
            parts += [x[:, lo:lo + LANES], _rope2d(x[:, lo + LANES:lo + 2 * LANES], rv[1], rv[2], transpose)]
        return [jnp.concatenate(parts, axis=1)], [], []

    return _rowwise("qrope_bwd" if transpose else "qrope_fwd", fn, T, tr, n_xt,
                    [(q, W), (cos, LANES), (sin, LANES)], row_out=[(W, W, BF16)])[0]


def _loss_grad(h, target, T, tr, n_xt):
    D = h.shape[1]
    nt = T // tr

    def body(h_ref, t_ref, dh_ref, s_ref):
        t = pl.program_id(0)
        diff = jnp.where(t < n_xt, h_ref[...] - t_ref[...], 0.0)
        dh_ref[...] = diff * (1.0 / D)
        s = jnp.sum(diff * diff, axis=0, keepdims=True)

        @pl.when(t == 0)
        def _():
            s_ref[...] = s

        @pl.when(t != 0)
        def _():
            s_ref[...] += s

    return pl.pallas_call(
        body, grid=(nt,),
        in_specs=[pl.BlockSpec((tr, D), lambda t: (t, 0)),
                  pl.BlockSpec((tr, D), lambda t: (jnp.minimum(t, n_xt - 1), 0))],
        out_specs=[pl.BlockSpec((tr, D), lambda t: (t, 0)), pl.BlockSpec((1, D), lambda t: (0, 0))],
        out_shape=[jax.ShapeDtypeStruct((T, D), F32), jax.ShapeDtypeStruct((1, D), F32)],
        compiler_params=_params("arbitrary"), name="loss_grad")(h, target)


def _adamw(w, g, m, v):
    shape = w.shape
    C = shape[-1] if w.ndim > 1 else shape[0]
    R = w.size // C
    tr = _pick(R, max(8, (2 << 20) // (C * 4)), 8)
    c1 = 1.0 - ADAM_B1 ** ADAM_STEP
    c2 = 1.0 - ADAM_B2 ** ADAM_STEP

    def body(w_ref, g_ref, m_ref, v_ref, d_ref, nm_ref, nv_ref):
        gv = g_ref[...]
        nm = ADAM_B1 * m_ref[...] + (1.0 - ADAM_B1) * gv
        nv = ADAM_B2 * v_ref[...] + (1.0 - ADAM_B2) * (gv * gv)
        nm_ref[...] = nm
        nv_ref[...] = nv
        d_ref[...] = -ADAM_LR * ((nm / c1) / (jnp.sqrt(nv / c2) + ADAM_EPS) + ADAM_WD * w_ref[...])

    spec = pl.BlockSpec((tr, C), lambda i: (i, 0))
    outs = pl.pallas_call(
        body, grid=(R // tr,), in_specs=[spec] * 4, out_specs=[spec] * 3,
        out_shape=[jax.ShapeDtypeStruct((R, C), F32)] * 3,
        compiler_params=_params("parallel"), name="adamw")(*[t.reshape(R, C) for t in (w, g, m, v)])
    return tuple(o.reshape(shape) for o in outs)


def _ret_chunk(d, s, ncx, ncc):
    fwd = jnp.where(s < ncc, ncx + s, s - ncc)
    bwd = jnp.where(s < ncc, ncx + ncc - 1 - s, ncx - 1 - (s - ncc))
    return jnp.where(d == 0, fwd, bwd)


def _rot_half(x, cos, sin, transpose):
    half = x.shape[-1] // 2
    if transpose:
        return x * cos + pltpu.roll(x * sin, half, x.ndim - 1)
    return x * cos + pltpu.roll(x, half, x.ndim - 1) * sin


def _dot_nt(a, b):
    return lax.dot_general(a, b, (((1,), (1,)), ((), ())), preferred_element_type=F32)


def _dot_tn(a, b):
    return lax.dot_general(a, b, (((0,), (0,)), ((), ())), preferred_element_type=F32)


def _dot(a, b):
    return jnp.dot(a, b, preferred_element_type=F32)


def _ret_fwd(q, k, v, tabs, ncx, ncc):
    T, C, H = q.shape[0], RET_CHUNK, RET_HEADS
    dk, dv = q.shape[1] // H, v.shape[1] // H
    ns = ncx + ncc
    kscale = dk ** -0.5

    G = RET_HEAD_GROUP

    def body(q_ref, k_ref, v_ref, cos_ref, sin_ref, intra_ref, qd_ref, kd_ref, cd_ref, o_ref, st_ref, s_scr):
        @pl.when(pl.program_id(2) == 0)
        def _():
            s_scr[...] = jnp.zeros_like(s_scr)

        cos, sin = cos_ref[...], sin_ref[...]
        for g in range(G):
            ks_, vs_ = slice(g * dk, (g + 1) * dk), slice(g * dv, (g + 1) * dv)
            qv = _rot_half(q_ref[:, ks_], cos, sin, False)
            kv = _rot_half(k_ref[:, ks_], cos, sin, False) * kscale
            vb = v_ref[:, vs_].astype(BF16)
            S = s_scr[g]
            st_ref[g] = S
            scores = _dot_nt(qv.astype(BF16), kv.astype(BF16)) * intra_ref[g]
            o_ref[:, vs_] = _dot(scores.astype(BF16), vb) + _dot((qv * qd_ref[g]).astype(BF16), S.astype(BF16))
            s_scr[g] = S * cd_ref[g] + _dot_tn((kv * kd_ref[g]).astype(BF16), vb)

    chunk = lambda d, h, s: _ret_chunk(d, s, ncx, ncc)
    tab = lambda shape: pl.BlockSpec((None, G) + shape, lambda d, h, s: (d, h, 0, 0))
    return pl.pallas_call(
        body, grid=(2, H // G, ns),
        in_specs=[pl.BlockSpec((C, G * dk), lambda d, h, s: (chunk(d, h, s), h)),
                  pl.BlockSpec((C, G * dk), lambda d, h, s: (chunk(d, h, s), h)),
                  pl.BlockSpec((C, G * dv), lambda d, h, s: (chunk(d, h, s), h)),
                  pl.BlockSpec((C, dk), lambda d, h, s: (chunk(d, h, s), 0)),
                  pl.BlockSpec((C, dk), lambda d, h, s: (chunk(d, h, s), 0)),
                  tab((C, C)), tab((C, dk)), tab((C, dk)), tab((1, dv))],
        out_specs=[pl.BlockSpec((None, C, G * dv), lambda d, h, s: (d, chunk(d, h, s), h)),
                   pl.BlockSpec((None, G, None, dk, dv), lambda d, h, s: (d, h, s, 0, 0))],
        out_shape=[jax.ShapeDtypeStruct((2, T, H * dv), F32), jax.ShapeDtypeStruct((2, H, ns, dk, dv), F32)],
        scratch_shapes=[pltpu.VMEM((G, dk, dv), F32)],
        compiler_params=_params("parallel", "parallel", "arbitrary"), name="ret_fwd",
    )(q, k, v, tabs["cos"], tabs["sin"], tabs["intra"], tabs["qd"], tabs["kd"], tabs["cd"])


def _ret_bwd(q, k, v, do, states, tabs, ncx, ncc):
    T, C, H = q.shape[0], RET_CHUNK, RET_HEADS
    dk, dv = q.shape[1] // H, v.shape[1] // H
    ns = ncx + ncc
    kscale = dk ** -0.5

    G = RET_HEAD_GROUP

    def body(q_ref, k_ref, v_ref, do_ref, st_ref, cos_ref, sin_ref, intra_ref, qd_ref, kd_ref, cd_ref,
             dm_ref, wq_ref, wk_ref, dq_ref, dk_ref, dv_ref, dl_ref, ds_scr):
        first = pl.program_id(2) == 0

        @pl.when(first)
        def _():
            ds_scr[...] = jnp.zeros_like(ds_scr)

        cos, sin = cos_ref[...], sin_ref[...]
        dm, wq, wk = dm_ref[...], wq_ref[...], wk_ref[...]
        for g in range(G):
            ks_, vs_ = slice(g * dk, (g + 1) * dk), slice(g * dv, (g + 1) * dv)
            qv = _rot_half(q_ref[:, ks_], cos, sin, False)
            kv = _rot_half(k_ref[:, ks_], cos, sin, False) * kscale
            qb, kb = qv.astype(BF16), kv.astype(BF16)
            vb = v_ref[:, vs_].astype(BF16)
            dob = do_ref[:, vs_]
            intra, qd, kd, cd = intra_ref[g], qd_ref[g], kd_ref[g], cd_ref[g]
            S, dS = st_ref[g], ds_scr[g]
            Sb, dSb = S.astype(BF16), dS.astype(BF16)
            P = _dot_nt(qb, kb) * intra
            dP_raw = _dot_nt(dob, vb)
            dPb = (dP_raw * intra).astype(BF16)
            dq_cross = _dot_nt(dob, Sb) * qd
            dq_rot = _dot(dPb, kb) + dq_cross
            dk_state = _dot_nt(vb, dSb) * kd
            dk_rot = _dot_tn(dPb, qb) + dk_state
            dv_ref[:, vs_] = _dot_tn(P.astype(BF16), dob) + _dot((kv * kd).astype(BF16), dSb)
            dq_ref[:, ks_] = _rot_half(dq_rot, cos, sin, True)
            dk_ref[:, ks_] = _rot_half(dk_rot, cos, sin, True) * kscale
            dlam = (jnp.sum(dm * P * dP_raw) + jnp.sum(wq * qv * dq_cross)
                    + C * jnp.sum(cd * S * dS) + jnp.sum(wk * kv * dk_state))
            dl = jnp.full((1, LANES), dlam, F32)

            @pl.when(first)
            def _(g=g, dl=dl):
                dl_ref[g] = dl

            @pl.when(jnp.logical_not(first))
            def _(g=g, dl=dl):
                dl_ref[g] += dl

            ds_scr[g] = cd * dS + _dot_tn((qv * qd).astype(BF16), dob)

    chunk = lambda d, h, s: _ret_chunk(d, ns - 1 - s, ncx, ncc)
    tab = lambda shape: pl.BlockSpec((None, G) + shape, lambda d, h, s: (d, h, 0, 0))
    dtab = lambda shape: pl.BlockSpec((None,) + shape, lambda d, h, s: (d, 0, 0))
    return pl.pallas_call(
        body, grid=(2, H // G, ns),
        in_specs=[pl.BlockSpec((C, G * dk), lambda d, h, s: (chunk(d, h, s), h)),
                  pl.BlockSpec((C, G * dk), lambda d, h, s: (chunk(d, h, s), h)),
                  pl.BlockSpec((C, G * dv), lambda d, h, s: (chunk(d, h, s), h)),
                  pl.BlockSpec((None, C, G * dv), lambda d, h, s: (d, chunk(d, h, s), h)),
                  pl.BlockSpec((None, G, None, dk, dv), lambda d, h, s: (d, h, ns - 1 - s, 0, 0)),
                  pl.BlockSpec((C, dk), lambda d, h, s: (chunk(d, h, s), 0)),
                  pl.BlockSpec((C, dk), lambda d, h, s: (chunk(d, h, s), 0)),
                  tab((C, C)), tab((C, dk)), tab((C, dk)), tab((1, dv)),
                  dtab((C, C)), dtab((C, dk)), dtab((C, dk))],
        out_specs=[pl.BlockSpec((None, C, G * dk), lambda d, h, s: (d, chunk(d, h, s), h)),
                   pl.BlockSpec((None, C, G * dk), lambda d, h, s: (d, chunk(d, h, s), h)),
                   pl.BlockSpec((None, C, G * dv), lambda d, h, s: (d, chunk(d, h, s), h)),
                   pl.BlockSpec((None, G, 1, LANES), lambda d, h, s: (d, h, 0, 0))],
        out_shape=[jax.ShapeDtypeStruct((2, T, H * dk), F32), jax.ShapeDtypeStruct((2, T, H * dk), F32),
                   jax.ShapeDtypeStruct((2, T, H * dv), F32), jax.ShapeDtypeStruct((2, H, 1, LANES), F32)],
        scratch_shapes=[pltpu.VMEM((G, dk, dv), F32)],
        compiler_params=_params("parallel", "parallel", "arbitrary"), name="ret_bwd",
    )(q, k, v, do, states, tabs["cos"], tabs["sin"], tabs["intra"], tabs["qd"], tabs["kd"], tabs["cd"],
      tabs["dmat"], tabs["wq"], tabs["wk"])


def _ret_tables(decay_logit, Nx, Nc, dk, dv):
    C, H = RET_CHUNK, RET_HEADS
    inv = RET_ROPE_BASE ** (-jnp.linspace(0.0, 1.0, dk // 2, dtype=F32))
    ang = jnp.arange(Nx, dtype=F32)[:, None] * inv[None, :]
    cos, sin = jnp.cos(ang), jnp.sin(ang)
    cosf = jnp.concatenate([jnp.concatenate([cos, cos], 1), jnp.ones((Nc, dk), F32)], 0)
    sinf = jnp.concatenate([jnp.concatenate([-sin, sin], 1), jnp.zeros((Nc, dk), F32)], 0)
    lg = jax.nn.log_sigmoid(decay_logit.astype(F32))
    idx = jnp.arange(C, dtype=F32)
    diff = idx[:, None] - idx[None, :]
    dmat = jnp.stack([jnp.maximum(diff, 0.0), jnp.maximum(-diff, 0.0)])
    mask = jnp.stack([diff >= 0, diff <= 0])
    intra = jnp.where(mask[:, None], jnp.exp(lg[:, :, None, None] * dmat[:, None]), 0.0)
    wq = jnp.stack([idx + 1.0, C - idx])
    wk = jnp.stack([C - 1.0 - idx, idx])
    qd = jnp.exp(lg[:, :, None] * wq[:, None, :])
    kd = jnp.exp(lg[:, :, None] * wk[:, None, :])
    cd = jnp.exp(lg * C)
    bc = lambda t, w: jnp.broadcast_to(t[..., None], t.shape + (w,))
    return dict(cos=cosf, sin=sinf, intra=intra, qd=bc(qd, dk), kd=bc(kd, dk),
                cd=jnp.broadcast_to(cd[:, :, None, None], (2, H, 1, dv)),
                dmat=dmat, wq=bc(wq, dk), wk=bc(wk, dk), lg=lg)


def _attn_fwd(q, kn, v, kr, Nx, tq):
    T, H = q.shape[0], MLA_HEADS
    n_xq = Nx // tq

    def body(q_ref, kn_ref, v_ref, kr_ref, o_ref, lse_ref):
        def attend(lo):
            kcat = jnp.concatenate([kn_ref[lo:, :], kr_ref[lo:, :]], axis=1)
            s = _dot_nt(q_ref[...], kcat)
            m = jnp.max(s, axis=-1, keepdims=True)
            p = jnp.exp(s - m)
            l = jnp.sum(p, axis=-1, keepdims=True)
            o_ref[...] = (_dot(p.astype(BF16), v_ref[lo:, :]) / l).astype(o_ref.dtype)
            lse_ref[...] = m + jnp.log(l)

        @pl.when(pl.program_id(1) < n_xq)
        def _():
            attend(0)

        @pl.when(pl.program_id(1) >= n_xq)
        def _():
            attend(Nx)

    return pl.pallas_call(
        body, grid=(H, T // tq),
        in_specs=[pl.BlockSpec((tq, 2 * LANES), lambda h, i: (i, h)),
                  pl.BlockSpec((T, LANES), lambda h, i: (0, h)),
                  pl.BlockSpec((T, LANES), lambda h, i: (0, h)),
                  pl.BlockSpec((T, LANES), lambda h, i: (0, 0))],
        out_specs=[pl.BlockSpec((tq, LANES), lambda h, i: (i, h)),
                   pl.BlockSpec((None, tq, 1), lambda h, i: (h, i, 0))],
        out_shape=[jax.ShapeDtypeStruct((T, H * LANES), BF16), jax.ShapeDtypeStruct((H, T, 1), F32)],
        compiler_params=_params("parallel", "arbitrary"), name="attn_fwd")(q, kn, v, kr)


def _attn_bwd(q, kn, v, kr, do, lse, Nx, tq):
    T, H = q.shape[0], MLA_HEADS
    n_xq, nq = Nx // tq, T // tq

    def body(q_ref, kn_ref, v_ref, kr_ref, do_ref, lse_ref, dq_ref, dkn_ref, dv_ref, dkr_ref, dk_acc, dv_acc):
        h, i = pl.program_id(0), pl.program_id(1)

        @pl.when(i == 0)
        def _():
            dk_acc[...] = jnp.zeros_like(dk_acc)
            dv_acc[...] = jnp.zeros_like(dv_acc)

        def attend(lo):
            kcat = jnp.concatenate([kn_ref[lo:, :], kr_ref[lo:, :]], axis=1)
            qb, dob = q_ref[...], do_ref[...]
            p = jnp.exp(_dot_nt(qb, kcat) - lse_ref[...])
            dp = _dot_nt(dob, v_ref[lo:, :])
            delta = jnp.sum(p * dp, axis=-1, keepdims=True)
            dsb = (p * (dp - delta)).astype(BF16)
            dq_ref[...] = _dot(dsb, kcat)
            dk_acc[lo:, :] += _dot_tn(dsb, qb)
            dv_acc[lo:, :] += _dot_tn(p.astype(BF16), dob)

        @pl.when(i < n_xq)
        def _():
            attend(0)

        @pl.when(i >= n_xq)
        def _():
            attend(Nx)

        @pl.when(i == nq - 1)
        def _():
            dkn_ref[...] = dk_acc[:, :LANES].astype(dkn_ref.dtype)
            dv_ref[...] = dv_acc[...].astype(dv_ref.dtype)

        @pl.when(jnp.logical_and(i == nq - 1, h == 0))
        def _():
            dkr_ref[...] = dk_acc[:, LANES:]

        @pl.when(jnp.logical_and(i == nq - 1, h != 0))
        def _():
            dkr_ref[...] += dk_acc[:, LANES:]

    return pl.pallas_call(
        body, grid=(H, nq),
        in_specs=[pl.BlockSpec((tq, 2 * LANES), lambda h, i: (i, h)),
                  pl.BlockSpec((T, LANES), lambda h, i: (0, h)),
                  pl.BlockSpec((T, LANES), lambda h, i: (0, h)),
                  pl.BlockSpec((T, LANES), lambda h, i: (0, 0)),
                  pl.BlockSpec((tq, LANES), lambda h, i: (i, h)),
                  pl.BlockSpec((None, tq, 1), lambda h, i: (h, i, 0))],
        out_specs=[pl.BlockSpec((tq, 2 * LANES), lambda h, i: (i, h)),
                   pl.BlockSpec((T, LANES), lambda h, i: (0, h)),
                   pl.BlockSpec((T, LANES), lambda h, i: (0, h)),
                   pl.BlockSpec((T, LANES), lambda h, i: (0, 0))],
        out_shape=[jax.ShapeDtypeStruct((T, H * 2 * LANES), F32), jax.ShapeDtypeStruct((T, H * LANES), BF16),
                   jax.ShapeDtypeStruct((T, H * LANES), BF16), jax.ShapeDtypeStruct((T, LANES), F32)],
        scratch_shapes=[pltpu.VMEM((T, 2 * LANES), F32), pltpu.VMEM((T, LANES), F32)],
        compiler_params=_params("arbitrary", "arbitrary"), name="attn_bwd")(q, kn, v, kr, do, lse)


def _mla_tables(Nx, Nc):
    ad = MLA_D_ROPE // 2
    inv = AXIAL_ROPE_BASE ** (-jnp.arange(ad // 2, dtype=F32) * 2.0 / ad)
    t = jnp.arange(Nx)
    rang = (t // GRID_W).astype(F32)[:, None] * inv[None, :]
    cang = (t % GRID_W).astype(F32)[:, None] * inv[None, :]
    rc, rs, cc, cs = jnp.cos(rang), jnp.sin(rang), jnp.cos(cang), jnp.sin(cang)
    pad1, pad0 = jnp.ones((Nx, LANES - MLA_D_ROPE), F32), jnp.zeros((Nx, LANES - MLA_D_ROPE), F32)
    cos = jnp.concatenate([rc, rc, cc, cc, pad1], 1)
    sin = jnp.concatenate([-rs, rs, -cs, cs, pad0], 1)
    return (jnp.concatenate([cos, jnp.ones((Nc, LANES), F32)], 0),
            jnp.concatenate([sin, jnp.zeros((Nc, LANES), F32)], 0))


def _place():
    x, y, c = lax.axis_index("x"), lax.axis_index("y"), lax.axis_index("c")
    return x, y, c


def _all_gather8(v):
    R = v.shape[0]

    def body(v_ref, g_ref, s_ref, send_sems, recv_sems):
        x, y, c = _place()
        me = 4 * x + 2 * y + c
        g_ref[me] = v_ref[...]
        copies = []
        for k in range(1, N_DEV):
            peer = (x ^ (k >> 2), y ^ ((k >> 1) & 1), c ^ (k & 1))
            copies.append(pltpu.make_async_remote_copy(
                src_ref=v_ref, dst_ref=g_ref.at[me], send_sem=send_sems.at[k - 1], recv_sem=recv_sems.at[k - 1],
                device_id=peer, device_id_type=MESH))
        for cp in copies:
            cp.start()
        for cp in copies:
            cp.wait_recv()
        for cp in copies:
            cp.wait_send()
        acc = g_ref[0]
        for k in range(1, N_DEV):
            acc = acc + g_ref[k]
        s_ref[...] = acc

    vm = pl.BlockSpec(memory_space=pltpu.VMEM)
    return pl.pallas_call(
        body, in_specs=[vm], out_specs=[vm, vm],
        out_shape=[jax.ShapeDtypeStruct((N_DEV, R, LANES), F32), jax.ShapeDtypeStruct((R, LANES), F32)],
        scratch_shapes=[pltpu.SemaphoreType.DMA((N_DEV - 1,)), pltpu.SemaphoreType.DMA((N_DEV - 1,))],
        compiler_params=pltpu.CompilerParams(vmem_limit_bytes=V7X_VMEM_LIMIT_BYTES), name="all_gather8")(v)


def _other_chips(x, y):
    return [(1 - x, y), (x, 1 - y), (1 - x, 1 - y)]


def _place_own(w, chip_idx, after):
    R, C = w.shape
    tr = _pick(R, max(16, (2 << 20) // (C * 4)), 16)

    def body(s_ref, w_ref, after_ref, o_ref):
        o_ref[...] = w_ref[...].astype(o_ref.dtype)

    return pl.pallas_call(
        body,
        grid_spec=pltpu.PrefetchScalarGridSpec(
            num_scalar_prefetch=1, grid=(R // tr,),
            in_specs=[pl.BlockSpec((tr, C), lambda i, s: (i, 0)), pl.BlockSpec(memory_space=pl.ANY)],
            out_specs=pl.BlockSpec((None, tr, C), lambda i, s: (s[0], i, 0))),
        out_shape=jax.ShapeDtypeStruct((N_CHIPS, R, C), BF16),
        compiler_params=_params("parallel"), name="place_own")(chip_idx, w, after)


HBM_SPEC = pl.BlockSpec(memory_space=pltpu.HBM)
SEM_SPEC = pl.BlockSpec(memory_space=pltpu.SEMAPHORE)
DATAFLOW_EFFECT = pltpu.SideEffectType.DATAFLOW_SIDE_EFFECTING


def _gather_start(buf, name):
    def body(b_ref, send_sem, recv_sem, b_thru, token):
        x, y, c = _place()
        chip = 2 * x + y
        for ch in _other_chips(x, y):
            pltpu.make_async_remote_copy(src_ref=b_ref.at[chip], dst_ref=b_ref.at[chip], send_sem=send_sem,
                                         recv_sem=recv_sem, device_id=(*ch, c), device_id_type=MESH).start()
        token[...] = jnp.zeros_like(token)

    return pl.pallas_call(
        body, name=name,
        out_shape=(pltpu.SemaphoreType.DMA(()), pltpu.SemaphoreType.DMA(()), pltpu.HBM(buf.shape, buf.dtype),
                   jax.ShapeDtypeStruct((8, LANES), F32)),
        in_specs=(HBM_SPEC,), out_specs=(SEM_SPEC, SEM_SPEC, HBM_SPEC, pl.BlockSpec(memory_space=pltpu.VMEM)),
        input_output_aliases={0: 2},
        compiler_params=pltpu.CompilerParams(has_side_effects=DATAFLOW_EFFECT),
    )(pltpu.with_memory_space_constraint(buf, pltpu.HBM))


def _gather_wait(send_sem, recv_sem, buf_thru, after, name):
    def body(b_ref, send_sem, recv_sem, after_ref, b_out):
        x, y, c = _place()
        three = b_ref.at[pl.ds(0, 3)]
        cp = pltpu.make_async_remote_copy(src_ref=three, dst_ref=three, send_sem=send_sem, recv_sem=recv_sem,
                                          device_id=(x, y, 1 - c), device_id_type=MESH)
        cp.wait_send()
        cp.wait_recv()

    return pl.pallas_call(
        body, name=name, out_shape=pltpu.HBM(buf_thru.shape, buf_thru.dtype),
        in_specs=(HBM_SPEC, SEM_SPEC, SEM_SPEC, pl.BlockSpec(memory_space=pl.ANY)), out_specs=HBM_SPEC,
        input_output_aliases={0: 0},
        compiler_params=pltpu.CompilerParams(has_side_effects=DATAFLOW_EFFECT),
    )(buf_thru, send_sem, recv_sem, after)


def _scatter_start(hb, name):
    def body(h_ref, land_ref, send_sem, recv_sem, h_thru, land_thru, token):
        x, y, c = _place()
        chip = 2 * x + y
        for ch in _other_chips(x, y):
            pltpu.make_async_remote_copy(src_ref=h_ref.at[2 * ch[0] + ch[1]], dst_ref=land_ref.at[chip],
                                         send_sem=send_sem, recv_sem=recv_sem, device_id=(*ch, c),
                                         device_id_type=MESH).start()
        token[...] = jnp.zeros_like(token)

    return pl.pallas_call(
        body, name=name,
        out_shape=(pltpu.SemaphoreType.DMA(()), pltpu.SemaphoreType.DMA(()), pltpu.HBM(hb.shape, hb.dtype),
                   pltpu.HBM(hb.shape, hb.dtype), jax.ShapeDtypeStruct((8, LANES), F32)),
        in_specs=(HBM_SPEC, HBM_SPEC),
        out_specs=(SEM_SPEC, SEM_SPEC, HBM_SPEC, HBM_SPEC, pl.BlockSpec(memory_space=pltpu.VMEM)),
        input_output_aliases={0: 2, 1: 3},
        compiler_params=pltpu.CompilerParams(has_side_effects=DATAFLOW_EFFECT),
    )(pltpu.with_memory_space_constraint(hb, pltpu.HBM),
      pltpu.with_memory_space_constraint(lax.empty(hb.shape, hb.dtype), pltpu.HBM))


def _scatter_wait(send_sem, recv_sem, h_thru, land_thru, after, name):
    def body(h_ref, land_ref, send_sem, recv_sem, after_ref, h_dead, got_ref):
        x, y, c = _place()
        cp = pltpu.make_async_remote_copy(src_ref=h_ref.at[pl.ds(0, 3)], dst_ref=land_ref.at[pl.ds(0, 3)],
                                          send_sem=send_sem, recv_sem=recv_sem, device_id=(x, y, 1 - c),
                                          device_id_type=MESH)
        cp.wait_send()
        cp.wait_recv()

    return pl.pallas_call(
        body, name=name,
        out_shape=(pltpu.HBM(h_thru.shape, h_thru.dtype), pltpu.HBM(land_thru.shape, land_thru.dtype)),
        in_specs=(HBM_SPEC, HBM_SPEC, SEM_SPEC, SEM_SPEC, pl.BlockSpec(memory_space=pl.ANY)),
        out_specs=(HBM_SPEC, HBM_SPEC), input_output_aliases={0: 0, 1: 1},
        compiler_params=pltpu.CompilerParams(has_side_effects=DATAFLOW_EFFECT),
    )(h_thru, land_thru, send_sem, recv_sem, after)[1]


def _swap_start(g, name):
    J, R, C = g.shape
    half = R // 2

    def body(g_ref, land_ref, send_sem, recv_sem, g_thru, land_thru, token):
        x, y, c = _place()
        pltpu.make_async_remote_copy(src_ref=g_ref.at[:, pl.ds((1 - c) * half, half), :], dst_ref=land_ref,
                                     send_sem=send_sem, recv_sem=recv_sem, device_id=(x, y, 1 - c),
                                     device_id_type=MESH).start()
        token[...] = jnp.zeros_like(token)

    return pl.pallas_call(
        body, name=name,
        out_shape=(pltpu.SemaphoreType.DMA(()), pltpu.SemaphoreType.DMA(()), pltpu.HBM(g.shape, g.dtype),
                   pltpu.HBM((J, half, C), g.dtype), jax.ShapeDtypeStruct((8, LANES), F32)),
        in_specs=(HBM_SPEC, HBM_SPEC),
        out_specs=(SEM_SPEC, SEM_SPEC, HBM_SPEC, HBM_SPEC, pl.BlockSpec(memory_space=pltpu.VMEM)),
        input_output_aliases={0: 2, 1: 3},
        compiler_params=pltpu.CompilerParams(has_side_effects=DATAFLOW_EFFECT),
    )(pltpu.with_memory_space_constraint(g, pltpu.HBM),
      pltpu.with_memory_space_constraint(lax.empty((J, half, C), g.dtype), pltpu.HBM))


def _swap_wait(send_sem, recv_sem, g_thru, land_thru, after, name):
    half = land_thru.shape[1]

    def body(g_ref, land_ref, send_sem, recv_sem, after_ref, g_out, got_ref):
        x, y, c = _place()
        cp = pltpu.make_async_remote_copy(src_ref=g_ref.at[:, pl.ds((1 - c) * half, half), :], dst_ref=land_ref,
                                          send_sem=send_sem, recv_sem=recv_sem, device_id=(x, y, 1 - c),
                                          device_id_type=MESH)
        cp.wait_send()
        cp.wait_recv()

    return pl.pallas_call(
        body, name=name,
        out_shape=(pltpu.HBM(g_thru.shape, g_thru.dtype), pltpu.HBM(land_thru.shape, land_thru.dtype)),
        in_specs=(HBM_SPEC, HBM_SPEC, SEM_SPEC, SEM_SPEC, pl.BlockSpec(memory_space=pl.ANY)),
        out_specs=(HBM_SPEC, HBM_SPEC), input_output_aliases={0: 0, 1: 1},
        compiler_params=pltpu.CompilerParams(has_side_effects=DATAFLOW_EFFECT),
    )(g_thru, land_thru, send_sem, recv_sem, after)


def _add_half(g, r, place_idx):
    J, R, C = g.shape
    half = R // 2
    tr = _pick(half, max(16, (2 << 20) // (C * 4)), 16)
    nb = half // tr

    def body(s_ref, g_ref, r_ref, o_ref, ob_ref):
        acc = g_ref[...] + r_ref[...]
        o_ref[...] = acc
        ob_ref[...] = acc.astype(ob_ref.dtype)

    spec = pl.BlockSpec((None, tr, C), lambda j, i, s: (j, i, 0))
    return pl.pallas_call(
        body,
        grid_spec=pltpu.PrefetchScalarGridSpec(
            num_scalar_prefetch=1, grid=(J, nb),
            in_specs=[pl.BlockSpec((None, tr, C), lambda j, i, s: (j, s[0] * nb + i, 0)), spec],
            out_specs=[spec, spec]),
        out_shape=[jax.ShapeDtypeStruct((J, half, C), F32), jax.ShapeDtypeStruct((J, half, C), BF16)],
        compiler_params=_params("parallel", "parallel"), name="add_half")(place_idx[4], g, r)


def _add_chips(h, p, place_idx):
    J, R, C = h.shape
    tr = _pick(R, max(16, (2 << 20) // (C * 4)), 16)
    nb = R // tr

    def body(s0, s1, s2, s3, s4, h_ref, p0_ref, p1_ref, p2_ref, o_ref):
        o_ref[...] = ((h_ref[...] + p0_ref[...].astype(F32)) + p1_ref[...].astype(F32)) + p2_ref[...].astype(F32)

    slot = lambda k: pl.BlockSpec((None, tr, C), lambda i, *s: (s[k][0], i, 0))
    return pl.pallas_call(
        body,
        grid_spec=pltpu.PrefetchScalarGridSpec(
            num_scalar_prefetch=5, grid=(nb,),
            in_specs=[slot(0), slot(1), slot(2), slot(3)],
            out_specs=pl.BlockSpec((tr, C), lambda i, *s: (s[4][0] * nb + i, 0))),
        out_shape=jax.ShapeDtypeStruct((2 * R, C), F32),
        compiler_params=_params("parallel"), name="add_chips")(*place_idx, h, p, p, p)


def _join_halves(s):
    R, C = s.shape
    half = R // 2

    def body(in_ref, o_ref, send_sem, recv_sem):
        x, y, c = _place()
        mine = o_ref.at[pl.ds(c * half, half)]
        cp = pltpu.make_async_remote_copy(src_ref=mine, dst_ref=mine, send_sem=send_sem, recv_sem=recv_sem,
                                          device_id=(x, y, 1 - c), device_id_type=MESH)
        cp.start()
        cp.wait()

    hbm = pl.BlockSpec(memory_space=pl.ANY)
    return pl.pallas_call(
        body, in_specs=[hbm], out_specs=hbm, out_shape=jax.ShapeDtypeStruct((R, C), s.dtype),
        input_output_aliases={0: 0},
        scratch_shapes=[pltpu.SemaphoreType.DMA, pltpu.SemaphoreType.DMA], name="join_halves")(s)


def _reduce_scatter_start(swap_state, place_idx, after, name):
    g, r = _swap_wait(*swap_state, after, name + "_swap_wait")
    h, hb = _add_half(g, r, place_idx)
    send_sem, recv_sem, hb_thru, land_thru, token = _scatter_start(hb, name + "_start")
    return (h, send_sem, recv_sem, hb_thru, land_thru), token


def _reduce_scatter_finish(state, place_idx, after, name):
    h, send_sem, recv_sem, hb_thru, land_thru = state
    p = _scatter_wait(send_sem, recv_sem, hb_thru, land_thru, after, name + "_wait")
    return _join_halves(_add_chips(h, p, place_idx))


def _pack(parts):
    flat = jnp.concatenate([p.reshape(-1).astype(F32) for p in parts])
    n = flat.shape[0]
    rows = -(-n // (8 * LANES)) * 8
    return jnp.pad(flat, (0, rows * LANES - n)).reshape(rows, LANES)


def _unpack(buf, shapes):
    flat = buf.reshape(buf.shape[:-2] + (-1,))
    out, off = [], 0
    for s in shapes:
        n = math.prod(s)
        out.append(flat[..., off:off + n].reshape(buf.shape[:-2] + tuple(s)))
        off += n
    return out


def _mod_table(mod_x, mod_c):
    return jnp.stack([mod_x.reshape(6, 1, -1), mod_c.reshape(6, 1, -1)])


def kernel(x, c, ctx, c_ctx, ada_w, ada_b, ln_g, ln_b, ret_w_qkv, ret_w_g, ret_decay_logit, ret_w_o, mla_w_dq, mla_g_q, mla_w_uq, mla_w_dkv, mla_g_kv, mla_w_ukv, mla_w_o, ffn_w_in, ffn_w_out, loss_target, m_c_ctx, m_ada_w, m_ada_b, m_ln_g, m_ln_b, m_ret_w_qkv, m_ret_w_g, m_ret_decay_logit, m_ret_w_o, m_mla_w_dq, m_mla_g_q, m_mla_w_uq, m_mla_w_dkv, m_mla_g_kv, m_mla_w_ukv, m_mla_w_o, m_ffn_w_in, m_ffn_w_out, v_c_ctx, v_ada_w, v_ada_b, v_ln_g, v_ln_b, v_ret_w_qkv, v_ret_w_g, v_ret_decay_logit, v_ret_w_o, v_mla_w_dq, v_mla_g_q, v_mla_w_uq, v_mla_w_dkv, v_mla_g_kv, v_mla_w_ukv, v_mla_w_o, v_ffn_w_in, v_ffn_w_out):
    Nx, D = x.shape[1], x.shape[2]
    Nc = ctx.shape[1]
    T = Nx + Nc
    tr = 256 if (Nx % 256 == 0 and Nc % 256 == 0) else 128
    n_xt = Nx // tr
    C = RET_CHUNK
    ncx, ncc = Nx // C, Nc // C
    H = RET_HEADS
    dk, dv = D // H, 2 * D // H
    L = MLA_LORA
    HM = MLA_HEADS
    D6 = 6 * D
    n6 = D6 // N_CHIPS
    Dq = D // N_CHIPS
    xi, yi, ci = lax.axis_index("x"), lax.axis_index("y"), lax.axis_index("c")
    chip = 2 * xi + yi
    dev = 4 * xi + 2 * yi + ci
    as_index = lambda s: jnp.reshape(s, (1,)).astype(jnp.int32)
    place_idx = (as_index(chip), as_index(2 * (1 - xi) + yi), as_index(2 * xi + 1 - yi),
                 as_index(2 * (1 - xi) + 1 - yi), as_index(ci))
    chip_idx = place_idx[0]
    tile = dict(T=T, tr=tr, n_xt=n_xt)

    shapes1 = [(D,), (DEPTH, 2, Dq), (DEPTH, 2, Dq), (2, L // N_CHIPS), (2, L // N_CHIPS)]
    g1, _ = _all_gather8(_pack([c[0], ln_g, ln_b, mla_g_q, mla_g_kv]))
    c_all, lng_s, lnb_s, gq_s, gkv_s = _unpack(g1, shapes1)
    by_chip = lambda t: jnp.moveaxis(t[0::2], 0, -2).reshape(t.shape[1:-1] + (-1,))
    ln_g_full, ln_b_full = by_chip(lng_s), by_chip(lnb_s)
    gq_full, gkv_full = by_chip(gq_s), by_chip(gkv_s)

    cond = jnp.concatenate([c_all, c_ctx[None]], 0)
    silu_cond = cond * jax.nn.sigmoid(cond)
    s16 = jnp.pad(silu_cond, ((0, 16 - (N_DEV + 1)), (0, 0))).astype(BF16)
    mods = []
    for i in range(DEPTH):
        bias = lax.dynamic_slice_in_dim(ada_b[i], chip * n6, n6)
        mods.append(_mm_nn(s16, ada_w[i][None], name="ada_fwd")[:N_DEV + 1] + bias[None])
    g2, _ = _all_gather8(_pack([jnp.stack(mods)]))
    (mod_all,) = _unpack(g2, [(DEPTH, N_DEV + 1, n6)])
    mod_all = jnp.moveaxis(mod_all[0::2], 0, -2).reshape(DEPTH, N_DEV + 1, D6)
    mod_tabs = [_mod_table(lax.dynamic_index_in_dim(mod_all[i], dev, 0, False), mod_all[i, N_DEV])
                for i in range(DEPTH)]
    lnps = [[jnp.stack([ln_g_full[i, s], ln_b_full[i, s]])[:, None, :] for s in range(2)] for i in range(DEPTH)]

    def mla_pack(j):
        return jnp.concatenate([mla_w_dq[j], mla_w_uq[j], jnp.pad(mla_w_dkv[j], ((0, 0), (0, 64))),
                                mla_w_ukv[j], mla_w_o[j]], axis=1)

    mla_cols = [L, 3 * L // 2, L + LANES, 2 * L, D]

    def mla_unpack_weights(buf):
        offs = [0]
        for wdt in mla_cols:
            offs.append(offs[-1] + wdt)
        dq_, uq_, dkv_, ukv_, wo_ = [buf[:, :, offs[k]:offs[k + 1]] for k in range(5)]
        w_dq = dq_.reshape(D, L)
        w_dkv = dkv_.reshape(D, L + LANES)
        w_lora = jnp.concatenate([w_dq, w_dkv], axis=1)
        w_uq = jnp.moveaxis(uq_, 0, 1).reshape(L, HM, MLA_D_NOPE + MLA_D_ROPE)
        wq_cat = jnp.pad(w_uq, ((0, 0), (0, 0), (0, 2 * LANES - MLA_D_NOPE - MLA_D_ROPE))).reshape(L, HM * 2 * LANES)
        w_ukv = jnp.moveaxis(ukv_, 0, 1).reshape(L, HM, MLA_D_NOPE + MLA_D_V)
        w_kn = w_ukv[:, :, :MLA_D_NOPE].reshape(L, HM * MLA_D_NOPE)
        w_v = w_ukv[:, :, MLA_D_NOPE:].reshape(L, HM * MLA_D_V)
        w_o = wo_.reshape(HM * MLA_D_V, D)
        return dict(lora=w_lora[None], q=wq_cat[None], kn=w_kn[None], v=w_v[None], o=w_o[None])

    def mla_pack_grads(g):
        d_lora = g["lora"][0]
        d_dq = d_lora[:, :L].reshape(N_CHIPS, Dq, L)
        d_dkv = d_lora[:, L:].reshape(N_CHIPS, Dq, L + LANES)
        d_uq = g["q"][0].reshape(L, HM, 2 * LANES)[:, :, :MLA_D_NOPE + MLA_D_ROPE]
        d_uq = jnp.moveaxis(d_uq.reshape(L, N_CHIPS, -1), 1, 0)
        d_ukv = jnp.concatenate([g["kn"][0].reshape(L, HM, MLA_D_NOPE), g["v"][0].reshape(L, HM, MLA_D_V)], axis=2)
        d_ukv = jnp.moveaxis(d_ukv.reshape(L, N_CHIPS, -1), 1, 0)
        d_o = g["o"][0].reshape(N_CHIPS, L, D)
        return jnp.concatenate([d_dq, d_uq, d_dkv, d_ukv, d_o], axis=2)

    assert Dq == L, "the packed MLA buffer assumes D_MODEL / 4 == 512 rows per shard"

    def layer_shards(i):
        j = i // 2
        own = dict(qkv=ret_w_qkv[j], g=ret_w_g[j], o=ret_w_o[j]) if i % 2 == 0 else dict(mla=mla_pack(j))
        own.update(w_in=ffn_w_in[i], w_out=ffn_w_out[i])
        return own

    def arrived(i, n, after, lw):
        full = _gather_wait(*in_flight[i][n], after=after, name=f"gather_wait_{i}_{n}")
        if n == "qkv":
            lw.update(q=(full, 0, 1), k=(full, 1, 1), v=(full, 2, 2))
        elif n == "g":
            lw.update(gf=(full, 0, 2), gb=(full, 2, 2))
        elif n == "o":
            lw.update(o=full.reshape(1, 2 * D, D))
        elif n == "mla":
            lw.update(mla_unpack_weights(full))
        elif n == "w_in":
            lw.update(a=(full, 0, 2), b=(full, 2, 2))
        else:
            lw.update(out=full.reshape(1, -1, D))

    W = [None] * DEPTH
    in_flight, start_tok, behind_small = [None] * DEPTH, jnp.zeros((8, LANES), F32), mod_all
    for i in range(DEPTH):
        in_flight[i] = {}
        for n, w_l in layer_shards(i).items():
            send_sem, recv_sem, thru, tok = _gather_start(_place_own(w_l, chip_idx, behind_small), name=f"gather_start_{i}_{n}")
            in_flight[i][n] = (send_sem, recv_sem, thru)
            start_tok = behind_small = start_tok + tok
    start_tok = start_tok[0, 0]

    ret_tabs = [_ret_tables(ret_decay_logit[j], Nx, Nc, dk, dv) for j in range(2)]
    mla_cos, mla_sin = _mla_tables(Nx, Nc)
    gqkv = [jnp.stack([gq_full[j], gkv_full[j]])[:, None, :] for j in range(2)]
    tq_f, tq_b = tr, tr

    h = jnp.concatenate([x[0], ctx[0]], axis=0) + start_tok
    u = _modulate(h, mod_tabs[0], **tile)
    saved = []
    for i in range(DEPTH):
        W[i] = lw = {}
        j, mod, sv = i // 2, mod_tabs[i], {}
        sv.update(h=h, u=u)
        if i % 2 == 0:
            arrived(i, "qkv", u, lw)
            q = _mm_nn(u, lw["q"], name="ret_q")
            k = _mm_nn(u, lw["k"], name="ret_k")
            v = _mm_nn(u, lw["v"], name="ret_v")
            arrived(i, "g", v, lw)
            gf = _mm_nn(u, lw["gf"], name="ret_gf")
            gb = _mm_nn(u, lw["gb"], name="ret_gb")
            o, states = _ret_fwd(q, k, v, ret_tabs[j], ncx, ncc)
            yg = _gate_fwd(gf, gb, o, dv=dv, **tile)
            arrived(i, "o", yg, lw)
            y = _mm_nn(yg, lw["o"], name="ret_o")
            sv.update(q=q, k=k, v=v, gf=gf, gb=gb, o=o, states=states, yg=yg)
        else:
            arrived(i, "mla", u, lw)
            lora = _mm_nn(u, lw["lora"], name="mla_lora")
            cqn, ckvn, kr = _lora_fwd(lora, gqkv[j], mla_cos, mla_sin, **tile)
            qcat = _mm_nn(cqn, lw["q"], name="mla_q")
            qrot = _qrope(qcat, mla_cos, mla_sin, False, **tile)
            kn = _mm_nn(ckvn, lw["kn"], out_dtype=BF16, name="mla_kn")
            vv = _mm_nn(ckvn, lw["v"], out_dtype=BF16, name="mla_v")
            att, lse = _attn_fwd(qrot, kn, vv, kr, Nx, tq_f)
            y = _mm_nn(att, lw["o"], name="mla_o")
            sv.update(lora=lora, cqn=cqn, ckvn=ckvn, kr=kr, qrot=qrot, kn=kn, vv=vv, att=att, lse=lse)
        h1, u2 = _ln_fwd(h, y, mod, lnps[i][0], 2, mod, (3, 4), **tile)
        arrived(i, "w_in", u2, lw)
        a = _mm_nn(u2, lw["a"], name="ffn_a")
        b = _mm_nn(u2, lw["b"], name="ffn_b")
        act = _swiglu_fwd(a, b, **tile)
        arrived(i, "w_out", act, lw)
        f = _mm_nn(act, lw["out"], name="ffn_out")
        last = i == DEPTH - 1
        h2, u_next = _ln_fwd(h1, f, mod, lnps[i][1], 5, None if last else mod_tabs[i + 1], (0, 1), **tile)
        sv.update(y=y, h1=h1, u2=u2, a=a, b=b, act=act, f=f)
        saved.append(sv)
        h, u = h2, u_next

    dh, err_cols = _loss_grad(h, loss_target[0], **tile)
    loss = lax.psum(0.5 * jnp.sum(err_cols) / D, ("x", "y", "c"))

    pending, swapping, order_tok = {}, [], [jnp.zeros((), F32)]

    def rs_advance(after):
        name, l, swap_state = swapping.pop(0)
        state, tok = _reduce_scatter_start(swap_state, place_idx, after, f"rs_{name}_{l}")
        pending[(name, l)] = state
        order_tok[0] = order_tok[0] + tok[0, 0]

    def rs_start(name, l, g):
        *swap_state, tok = _swap_start(g.reshape(N_CHIPS, -1, g.shape[-1]), f"rs_{name}_{l}_swap_start")
        if swapping:
            rs_advance(tok)
        swapping.append((name, l, swap_state))
        order_tok[0] = order_tok[0] + tok[0, 0]

    d_mods, d_lng, d_lnb = [None] * DEPTH, [None] * DEPTH, [None] * DEPTH
    d_gq, d_gkv, d_lam = [None] * 2, [None] * 2, [None] * 2
    for i in reversed(range(DEPTH)):
        j, lw, sv = i // 2, W[i], saved[i]
        mod = mod_tabs[i] + order_tok[0]
        dh1_res, df, dg_f, dlg1, dlb1 = _ln_bwd(sv["h1"], sv["f"], dh, mod, lnps[i][1], 5, **tile)
        dact = _mm_nt(df, lw["out"], name="ffn_out_nt")
        rs_start("ffn_w_out", i, _mm_tn(sv["act"], df, 1, name="ffn_out_tn").reshape(N_CHIPS, -1, D))
        da, db = _swiglu_bwd(sv["a"], sv["b"], dact, **tile)
        du2 = _mm_nt(db, lw["b"], add=_mm_nt(da, lw["a"], name="ffn_a_nt"), name="ffn_b_nt")
        g_in = _mm_tn(sv["u2"], da, 2, name="ffn_a_tn", slots=N_CHIPS)
        rs_start("ffn_w_in", i, _mm_tn(sv["u2"], db, 2, name="ffn_b_tn", slots=N_CHIPS, slot0=2, into=g_in))
        mod = mod_tabs[i] + order_tok[0]
        dh1, dsc_f, dsh_f = _mod_bwd(dh1_res, du2, sv["h1"], mod, 4, **tile)
        dh_res, dy, dg_a, dlg0, dlb0 = _ln_bwd(sv["h"], sv["y"], dh1, mod, lnps[i][0], 2, **tile)
        uu = sv["u"]
        if i % 2 == 0:
            dyg = _mm_nt(dy, lw["o"], name="ret_o_nt")
            rs_start("ret_w_o", j, _mm_tn(sv["yg"], dy, 1, name="ret_o_tn").reshape(N_CHIPS, -1, D))
            dgf, dgb, do_f, do_b = _gate_bwd(sv["gf"], sv["gb"], sv["o"], dyg, dv=dv, **tile)
            dq2, dk2, dv2, dlam = _ret_bwd(sv["q"], sv["k"], sv["v"], jnp.stack([do_f, do_b]), sv["states"],
                                           ret_tabs[j], ncx, ncc)
            dq, dkk, dvv = _add_dirs(dq2, dk2, dv2, **tile)
            du = _mm_nt(dq, lw["q"], name="ret_q_nt")
            du = _mm_nt(dkk, lw["k"], add=du, name="ret_k_nt")
            du = _mm_nt(dvv, lw["v"], add=du, name="ret_v_nt")
            du = _mm_nt(dgf, lw["gf"], add=du, name="ret_gf_nt")
            du = _mm_nt(dgb, lw["gb"], add=du, name="ret_gb_nt")
            g_qkv = _mm_tn(uu, dq, 1, name="ret_q_tn", slots=N_CHIPS)
            g_qkv = _mm_tn(uu, dkk, 1, name="ret_k_tn", slots=N_CHIPS, slot0=1, into=g_qkv)
            rs_start("ret_w_qkv", j, _mm_tn(uu, dvv, 2, name="ret_v_tn", slots=N_CHIPS, slot0=2, into=g_qkv))
            g_g = _mm_tn(uu, dgf, 2, name="ret_gf_tn", slots=N_CHIPS)
            rs_start("ret_w_g", j, _mm_tn(uu, dgb, 2, name="ret_gb_tn", slots=N_CHIPS, slot0=2, into=g_g))
            d_lam[j] = dlam[:, :, 0, 0]
        else:
            datt = _mm_nt(dy, lw["o"], name="mla_o_nt").astype(BF16)
            gm = dict(o=_mm_tn(sv["att"], dy, 1, name="mla_o_tn"))
            dqcat, dkn, dvv, dkr = _attn_bwd(sv["qrot"], sv["kn"], sv["vv"], sv["kr"], datt, sv["lse"], Nx, tq_b)
            dqraw = _qrope(dqcat, mla_cos, mla_sin, True, **tile)
            dcqn = _mm_nt(dqraw, lw["q"], name="mla_q_nt")
            gm["q"] = _mm_tn(sv["cqn"], dqraw, 1, name="mla_q_tn")
            dckvn = _mm_nt(dvv, lw["v"], add=_mm_nt(dkn, lw["kn"], name="mla_kn_nt"), name="mla_v_nt")
            gm["kn"] = _mm_tn(sv["ckvn"], dkn, 1, name="mla_kn_tn")
            gm["v"] = _mm_tn(sv["ckvn"], dvv, 1, name="mla_v_tn")
            dlora, dgq, dgkv = _lora_bwd(sv["lora"], dcqn, dckvn, dkr, gqkv[j], mla_cos, mla_sin, **tile)
            du = _mm_nt(dlora, lw["lora"], name="mla_lora_nt")
            gm["lora"] = _mm_tn(uu, dlora, 1, name="mla_lora_tn")
            rs_start("mla", j, mla_pack_grads(gm))
            d_gq[j], d_gkv[j] = dgq[0], dgkv[0]
        mod = mod_tabs[i] + order_tok[0]
        dh, dsc_a, dsh_a = _mod_bwd(dh_res, du, sv["h"], mod, 1, **tile)
        d_mods[i] = jnp.concatenate([dsh_a, dsc_a, dg_a, dsh_f, dsc_f, dg_f], axis=2)[:, 0, :]
        d_lng[i] = jnp.concatenate([dlg0, dlg1], 0)
        d_lnb[i] = jnp.concatenate([dlb0, dlb1], 0)

    rs_advance(dh)
    grad_x = dh[:Nx][None]

    d_mods = jnp.stack(d_mods)
    dlogit = jnp.stack([d_lam[j] * jax.nn.sigmoid(-ret_decay_logit[j]) for j in range(2)])
    shapes3 = [(DEPTH, D6), (DEPTH, D6), (DEPTH, 2, D), (DEPTH, 2, D), (2, L), (2, L), (2, 2, H)]
    g3, s3 = _all_gather8(_pack([d_mods[:, 0], d_mods[:, 1], jnp.stack(d_lng), jnp.stack(d_lnb),
                                 jnp.stack(d_gq), jnp.stack(d_gkv), dlogit]))
    dmod_x_all = _unpack(g3, shapes3)[0]
    dmod_x_sum, dmod_c_sum, g_lng, g_lnb, g_gq, g_gkv, g_decay = _unpack(s3, shapes3)
    grad_ada_b = dmod_x_sum + dmod_c_sum
    dmod9 = jnp.concatenate([jnp.moveaxis(dmod_x_all, 0, 1), dmod_c_sum[:, None]], axis=1)
    dmod16 = jnp.pad(lax.dynamic_slice_in_dim(dmod9, chip * n6, n6, axis=2), ((0, 0), (0, 16 - (N_DEV + 1)), (0, 0)))
    dmod16 = dmod16.astype(BF16)
    grad_ada_w = jnp.stack([_mm_tn(s16, dmod16[i], 1, name="ada_tn")[0] for i in range(DEPTH)])
    dsilu = _mm_nt(jnp.moveaxis(dmod16, 0, 1).reshape(16, DEPTH * n6), ada_w, name="ada_nt")
    _, s4 = _all_gather8(_pack([dsilu[N_DEV]]))
    sg = jax.nn.sigmoid(c_ctx)
    grad_c_ctx = (0.5 * s4.reshape(-1)[:D]) * (sg * (1.0 + c_ctx * (1.0 - sg)))

    my_cols = lambda t, n: lax.dynamic_slice_in_dim(t, chip * n, n, axis=t.ndim - 1)
    grad_ln_g, grad_ln_b = my_cols(g_lng, Dq), my_cols(g_lnb, Dq)
    grad_gq, grad_gkv = my_cols(g_gq, L // N_CHIPS), my_cols(g_gkv, L // N_CHIPS)

    upd_ada_w = _adamw(ada_w, grad_ada_w, m_ada_w, v_ada_w)
    behind = grad_c_ctx + upd_ada_w[0][0, 0, :D]

    def rs_done(name, l):
        return _reduce_scatter_finish(pending[(name, l)], place_idx, behind, f"rs_{name}_{l}")

    def rs(name, n_layers, shard_shape):
        return jnp.stack([rs_done(name, l).reshape(shard_shape) for l in range(n_layers)])

    grad_ret_w_qkv = rs("ret_w_qkv", 2, ret_w_qkv.shape[1:])
    grad_ret_w_g = rs("ret_w_g", 2, ret_w_g.shape[1:])
    grad_ret_w_o = rs("ret_w_o", 2, ret_w_o.shape[1:])
    grad_ffn_w_in = rs("ffn_w_in", DEPTH, ffn_w_in.shape[1:])
    grad_ffn_w_out = rs("ffn_w_out", DEPTH, ffn_w_out.shape[1:])
    mla_red = [rs_done("mla", l) for l in range(2)]
    offs = [0]
    for wdt in mla_cols:
        offs.append(offs[-1] + wdt)
    mla_parts = [jnp.stack([mla_red[l][:, offs[k]:offs[k + 1]] for l in range(2)]) for k in range(5)]
    grad_mla_w_dq, grad_mla_w_uq, grad_mla_w_ukv, grad_mla_w_o = mla_parts[0], mla_parts[1], mla_parts[3], mla_parts[4]
    grad_mla_w_dkv = mla_parts[2][:, :, :L + MLA_D_ROPE]

    grads = [grad_c_ctx, grad_ada_w, grad_ada_b, grad_ln_g, grad_ln_b, grad_ret_w_qkv, grad_ret_w_g, g_decay,
             grad_ret_w_o, grad_mla_w_dq, grad_gq, grad_mla_w_uq, grad_mla_w_dkv, grad_gkv, grad_mla_w_ukv,
             grad_mla_w_o, grad_ffn_w_in, grad_ffn_w_out]
    weights = [c_ctx, ada_w, ada_b, ln_g, ln_b, ret_w_qkv, ret_w_g, ret_decay_logit, ret_w_o, mla_w_dq, mla_g_q,
               mla_w_uq, mla_w_dkv, mla_g_kv, mla_w_ukv, mla_w_o, ffn_w_in, ffn_w_out]
    ms = [m_c_ctx, m_ada_w, m_ada_b, m_ln_g, m_ln_b, m_ret_w_qkv, m_ret_w_g, m_ret_decay_logit, m_ret_w_o,
          m_mla_w_dq, m_mla_g_q, m_mla_w_uq, m_mla_w_dkv, m_mla_g_kv, m_mla_w_ukv, m_mla_w_o, m_ffn_w_in, m_ffn_w_out]
    vs = [v_c_ctx, v_ada_w, v_ada_b, v_ln_g, v_ln_b, v_ret_w_qkv, v_ret_w_g, v_ret_decay_logit, v_ret_w_o,
          v_mla_w_dq, v_mla_g_q, v_mla_w_uq, v_mla_w_dkv, v_mla_g_kv, v_mla_w_ukv, v_mla_w_o, v_ffn_w_in, v_ffn_w_out]
    upd = [upd_ada_w if w_ is ada_w else _adamw(w_, g_, m_, v_) for w_, g_, m_, v_ in zip(weights, grads, ms, vs)]
    return (loss, grad_x, *grads, *[u_[0] for u_ in upd], *[u_[1] for u_ in upd], *[u_[2] for u_ in upd])
```
